```python
import jax, jax.numpy as jnp
from jax import lax
import numpy as np

D_MODEL = 1024
BATCH = 8
SEQ = 8192
DEPTH = 4
DEC_BATCH = 32
DEC_SEQ = 64
PAST_LEN = 4096

CHUNK = 64
D_A = 512
H_A = 8
DH_A = 64
D_B = 512
G_B = 8
C_B = 64
GMLP_CHUNK = 128
D_MIX = D_A + D_B
IN_WIDTH = 3 * D_A + H_A + 2 * D_B
SPLIT_POINTS = (D_A, 2 * D_A, 3 * D_A, 3 * D_A + H_A, 3 * D_A + H_A + D_B)
Q_BLOCK = 128
N_EXPERT_GROUPS = 4
EXPERTS_PER_GROUP = 8
N_EXPERTS = N_EXPERT_GROUPS * EXPERTS_PER_GROUP
TOP_K_IN_GROUP = 2
D_EXPERT = 512
MOE_BLOCK = 256
ALPHA = (2 * DEPTH) ** 0.25
BETA = (8 * DEPTH) ** -0.25
LN_EPS = 1e-5

kernel_name = 'hymba_fox_gmlp_hiermoe_stream_step'


def layer_norm(x, g, b):
    xf = x.astype(jnp.float32)
    mu = jnp.mean(xf, -1, keepdims=True)
    var = jnp.mean(jnp.square(xf - mu), -1, keepdims=True)
    return ((xf - mu) * lax.rsqrt(var + LN_EPS) * g.astype(jnp.float32) + b.astype(jnp.float32)).astype(x.dtype)


def rms_norm(x, g):
    xf = x.astype(jnp.float32)
    return (xf * lax.rsqrt(jnp.mean(xf * xf, -1, keepdims=True) + LN_EPS) * g.astype(jnp.float32)).astype(x.dtype)


def combined_projection(h, w_in, b_f):
    b, s, _ = h.shape
    p = jnp.einsum('bsd,de->bse', h, w_in)
    q, k, v, fl, u, gv = jnp.split(p, SPLIT_POINTS, axis=-1)
    q = q.reshape(b, s, H_A, DH_A)
    k = k.reshape(b, s, H_A, DH_A)
    v = v.reshape(b, s, H_A, DH_A)
    logf = jax.nn.log_sigmoid(fl.astype(jnp.float32) + b_f.astype(jnp.float32))
    return q, k, v, logf, u, gv


def gmlp_inputs(u, gv, ln_v_g, ln_v_b):
    return jax.nn.gelu(u), layer_norm(jax.nn.gelu(gv), ln_v_g, ln_v_b)


def fox_attend(qb, dq, qpos, k, v, dk, kpos):
    logits = jnp.einsum('bqhd,bkhd->bhqk', qb.astype(jnp.float32), k.astype(jnp.float32)) * (DH_A ** -0.5)
    logits = logits + jnp.transpose(dq, (0, 2, 1))[..., :, None] - jnp.transpose(dk, (0, 2, 1))[..., None, :]
    mask = qpos[:, None] >= kpos[None, :]
    probs = jax.nn.softmax(jnp.where(mask, logits, -jnp.inf), axis=-1)
    return jnp.einsum('bhqk,bkhd->bqhd', probs.astype(v.dtype), v)


def fox_prompt(q, k, v, logf):
    b, s = q.shape[:2]
    dcum = jnp.cumsum(logf, axis=1)
    kpos = jnp.arange(s)

    def q_block(i):
        s0 = i * Q_BLOCK
        qb = lax.dynamic_slice_in_dim(q, s0, Q_BLOCK, axis=1)
        dq = lax.dynamic_slice_in_dim(dcum, s0, Q_BLOCK, axis=1)
        return fox_attend(qb, dq, s0 + jnp.arange(Q_BLOCK), k, v, dcum, kpos)

    out = lax.map(q_block, jnp.arange(s // Q_BLOCK))
    return jnp.transpose(out, (1, 0, 2, 3, 4)).reshape(b, s, D_A)


def fox_sample(q, k, v, logf, cache_k, cache_v, cache_logf):
    b, t = q.shape[:2]
    p = cache_k.shape[1]
    k_all = jnp.concatenate([cache_k.astype(k.dtype), k], axis=1)
    v_all = jnp.concatenate([cache_v.astype(v.dtype), v], axis=1)
    dcum = jnp.cumsum(jnp.concatenate([cache_logf.astype(jnp.float32), logf], axis=1), axis=1)
    out = fox_attend(q, dcum[:, p:], p + jnp.arange(t), k_all, v_all, dcum, jnp.arange(p + t))
    return out.reshape(b, t, D_A)


def causal_spatial_weights(w_s):
    return w_s * jnp.tril(jnp.ones((GMLP_CHUNK, GMLP_CHUNK), w_s.dtype))


def gmlp_prompt(u, gv, w_s, b_s):
    b, s, _ = u.shape
    nc = s // GMLP_CHUNK
    wm = causal_spatial_weights(w_s)
    vg = gv.reshape(b, nc, GMLP_CHUNK, G_B, C_B)
    mixed = jnp.einsum('gts,bnsgc->bntgc', wm, vg) + b_s.T[None, None, :, :, None]
    return u * mixed.reshape(b, s, D_B)


def gmlp_sample(u, gv, w_s, b_s):
    b, t, _ = u.shape
    wm = causal_spatial_weights(w_s)[:, :t, :t]
    vg = gv.reshape(b, t, G_B, C_B)
    mixed = jnp.einsum('gts,bsgc->btgc', wm, vg) + b_s[:, :t].T[None, :, :, None]
    return u * mixed.reshape(b, t, D_B)


def hierarchical_moe(x, w_gr, b_gr, w_er, b_er, w1, w3, w2):
    lead = x.shape[:-1]
    xt = x.reshape(-1, D_MODEL)
    n = xt.shape[0]
    gl = (jnp.einsum('nd,dg->ng', xt, w_gr) + b_gr).astype(jnp.float32)
    g_prob, g_idx = lax.top_k(jax.nn.softmax(gl, axis=-1), 1)
    el = (jnp.einsum('nd,de->ne', xt, w_er) + b_er).astype(jnp.float32).reshape(n, N_EXPERT_GROUPS, EXPERTS_PER_GROUP)
    el_sel = jnp.take_along_axis(el, jnp.broadcast_to(g_idx[:, :, None], (n, 1, EXPERTS_PER_GROUP)), axis=1)[:, 0]
    e_logit, e_idx = lax.top_k(el_sel, TOP_K_IN_GROUP)
    gate = g_prob * jax.nn.softmax(e_logit, axis=-1)
    expert_id = g_idx * EXPERTS_PER_GROUP + e_idx

    n_assign = n * TOP_K_IN_GROUP
    eid = expert_id.reshape(n_assign).astype(jnp.int32)
    tok = jnp.repeat(jnp.arange(n, dtype=jnp.int32), TOP_K_IN_GROUP)
    gw = gate.reshape(n_assign)
    order = jnp.argsort(eid)
    eid_s, tok_s, gw_s = eid[order], tok[order], gw[order]
    counts = jnp.zeros((N_EXPERTS,), jnp.int32).at[eid].add(1)
    starts = jnp.cumsum(counts) - counts
    padded = (counts + MOE_BLOCK - 1) // MOE_BLOCK * MOE_BLOCK
    pad_ends = jnp.cumsum(padded)
    pad_starts = pad_ends - padded
    slot = pad_starts[eid_s] + jnp.arange(n_assign, dtype=jnp.int32) - starts[eid_s]
    n_blocks = -(-(n_assign + N_EXPERTS * (MOE_BLOCK - 1)) // MOE_BLOCK)
    n_slots = n_blocks * MOE_BLOCK
    slot_tok = jnp.full((n_slots,), n, jnp.int32).at[slot].set(tok_s)
    slot_gate = jnp.zeros((n_slots,), jnp.float32).at[slot].set(gw_s)
    block_start = jnp.arange(n_blocks, dtype=jnp.int32) * MOE_BLOCK
    block_expert = jnp.minimum(jnp.searchsorted(pad_ends, block_start, side='right'), N_EXPERTS - 1)
    x_pad = jnp.concatenate([xt, jnp.zeros((1, D_MODEL), xt.dtype)], axis=0)
    xb = x_pad[slot_tok].reshape(n_blocks, MOE_BLOCK, D_MODEL)

    def expert_block(args):
        xblk, e = args
        h = jax.nn.silu(xblk @ w1[e]) * (xblk @ w3[e])
        return h @ w2[e]

    yb = lax.map(expert_block, (xb, block_expert)).reshape(n_slots, D_MODEL)
    y = jax.ops.segment_sum(yb * slot_gate[:, None].astype(yb.dtype), slot_tok, num_segments=n + 1)[:n]
    return y.reshape(lead + (D_MODEL,))


def layer_tail(x, a_out, b_out, g_norm_a, g_norm_b, w_out, ln1_g, ln1_b,
               w_gr, b_gr, w_er, b_er, w1, w3, w2, ln2_g, ln2_b):
    mix = jnp.concatenate([rms_norm(a_out, g_norm_a), rms_norm(b_out, g_norm_b)], axis=-1)
    x = layer_norm(ALPHA * x + jnp.einsum('bse,ed->bsd', mix, w_out), ln1_g, ln1_b)
    return layer_norm(ALPHA * x + hierarchical_moe(x, w_gr, b_gr, w_er, b_er, w1, w3, w2), ln2_g, ln2_b)


def setup_inputs(seed: int = 0) -> dict:
    key = jax.random.key(seed)
    ks = jax.random.split(key, 28)
    f32 = jnp.float32

    def nrm(k, shape, scale):
        return jax.random.normal(k, shape, f32) * scale

    col_scale = jnp.concatenate([
        jnp.ones((2 * D_A,), f32),
        jnp.full((D_A,), BETA, f32),
        jnp.full((H_A,), 0.1, f32),
        jnp.full((D_B,), BETA, f32),
        jnp.ones((D_B,), f32)])
    return {
        'x_prompt': nrm(ks[0], (BATCH, SEQ, D_MODEL), 1.0),
        'x_sample': nrm(ks[1], (DEC_BATCH, DEC_SEQ, D_MODEL), 1.0),
        'cache_k': nrm(ks[2], (DEPTH, DEC_BATCH, PAST_LEN, H_A, DH_A), 1.0),
        'cache_v': nrm(ks[3], (DEPTH, DEC_BATCH, PAST_LEN, H_A, DH_A), BETA),
        'cache_logf': jax.nn.log_sigmoid(jax.random.uniform(ks[4], (DEPTH, DEC_BATCH, PAST_LEN, H_A), f32, 1.0, 6.0)),
        'w_in': nrm(ks[5], (DEPTH, D_MODEL, IN_WIDTH), D_MODEL ** -0.5) * col_scale,
        'b_f': jax.random.uniform(ks[6], (DEPTH, H_A), f32, 1.0, 6.0),
        'ln_v_g': 1.0 + nrm(ks[7], (DEPTH, D_B), 0.02),
        'ln_v_b': nrm(ks[8], (DEPTH, D_B), 0.02),
        'w_s': nrm(ks[9], (DEPTH, G_B, GMLP_CHUNK, GMLP_CHUNK), GMLP_CHUNK ** -0.5),
        'b_s': 1.0 + nrm(ks[10], (DEPTH, G_B, GMLP_CHUNK), 0.02),
        'g_norm_a': 1.0 + nrm(ks[11], (DEPTH, D_A), 0.02),
        'g_norm_b': 1.0 + nrm(ks[12], (DEPTH, D_B), 0.02),
        'w_out': nrm(ks[13], (DEPTH, D_MIX, D_MODEL), D_MIX ** -0.5 * BETA),
        'ln1_g': 1.0 + nrm(ks[14], (DEPTH, D_MODEL), 0.02),
        'ln1_b': nrm(ks[15], (DEPTH, D_MODEL), 0.02),
        'w_gr': nrm(ks[16], (DEPTH, D_MODEL, N_EXPERT_GROUPS), D_MODEL ** -0.5),
        'b_gr': nrm(ks[17], (DEPTH, N_EXPERT_GROUPS), 0.01),
        'w_er': nrm(ks[18], (DEPTH, D_MODEL, N_EXPERTS), D_MODEL ** -0.5),
        'b_er': nrm(ks[19], (DEPTH, N_EXPERTS), 0.01),
        'w1': nrm(ks[20], (DEPTH, N_EXPERTS, D_MODEL, D_EXPERT), D_MODEL ** -0.5 * BETA),
        'w3': nrm(ks[21], (DEPTH, N_EXPERTS, D_MODEL, D_EXPERT), D_MODEL ** -0.5 * BETA),
        'w2': nrm(ks[22], (DEPTH, N_EXPERTS, D_EXPERT, D_MODEL), D_EXPERT ** -0.5 * BETA),
        'ln2_g': 1.0 + nrm(ks[23], (DEPTH, D_MODEL), 0.02),
        'ln2_b': nrm(ks[24], (DEPTH, D_MODEL), 0.02),
    }


def reference(x_prompt, x_sample, cache_k, cache_v, cache_logf, w_in, b_f, ln_v_g, ln_v_b, w_s, b_s,
              g_norm_a, g_norm_b, w_out, ln1_g, ln1_b, w_gr, b_gr, w_er, b_er, w1, w3, w2, ln2_g, ln2_b):
    assert x_sample.shape[1] <= CHUNK
    xp, xs = x_prompt, x_sample
    kp, vp, lfp, ksn, vsn, lfs, gvs = [], [], [], [], [], [], []
    for l in range(DEPTH):
        tail = (g_norm_a[l], g_norm_b[l], w_out[l], ln1_g[l], ln1_b[l], w_gr[l], b_gr[l], w_er[l], b_er[l],
                w1[l], w3[l], w2[l], ln2_g[l], ln2_b[l])
        q, k, v, logf, u, gv = combined_projection(xp, w_in[l], b_f[l])
        u, gv = gmlp_inputs(u, gv, ln_v_g[l], ln_v_b[l])
        a_out = fox_prompt(q, k, v, logf)
        b_out = gmlp_prompt(u, gv, w_s[l], b_s[l])
        xp = layer_tail(xp, a_out, b_out, *tail)
        kp.append(k)
        vp.append(v)
        lfp.append(logf)
        q, k, v, logf, u, gv = combined_projection(xs, w_in[l], b_f[l])
        u, gv = gmlp_inputs(u, gv, ln_v_g[l], ln_v_b[l])
        a_out = fox_sample(q, k, v, logf, cache_k[l], cache_v[l], cache_logf[l])
        b_out = gmlp_sample(u, gv, w_s[l], b_s[l])
        xs = layer_tail(xs, a_out, b_out, *tail)
        ksn.append(k)
        vsn.append(v)
        lfs.append(logf)
        gvs.append(gv)
    new_k_prompt = jnp.stack(kp)
    new_v_prompt = jnp.stack(vp)
    new_logf_prompt = jnp.stack(lfp)
    new_k_sample = jnp.stack(ksn)
    new_v_sample = jnp.stack(vsn)
    new_logf_sample = jnp.stack(lfs)
    new_gmlp_v_sample = jnp.stack(gvs)
    return (xp, xs, new_k_prompt, new_v_prompt, new_logf_prompt, new_k_sample, new_v_sample, new_logf_sample, new_gmlp_v_sample)
```

```python
import functools

import jax
import jax.numpy as jnp
from jax import lax
from jax.experimental import pallas as pl
from jax.experimental.pallas import tpu as pltpu

F32 = jnp.float32
BF16 = jnp.bfloat16

D_MODEL = 1024
D_A = 512
H_A = 8
DH_A = 64
D_B = 512
G_B = 8
C_B = 64
GMLP_CHUNK = 128
N_GROUPS = 4
EXPERTS_PER_GROUP = 8
N_EXPERTS = N_GROUPS * EXPERTS_PER_GROUP
D_EXPERT = 512
LN_EPS = 1e-5
HEAD_PAIRS = H_A // 2

LANES = 128
SUBLANES = 8
VMEM_LIMIT_BYTES = 56 * 1024 * 1024

TOKEN_TILE = 512
MOE_ROWS = 512
NEG_BIG = -1e30


def _cparams(*sem):
    return pltpu.CompilerParams(dimension_semantics=sem, vmem_limit_bytes=VMEM_LIMIT_BYTES)


def _split3(x):
    hi = x.astype(BF16)
    r1 = x - hi.astype(F32)
    mid = r1.astype(BF16)
    lo = (r1 - mid.astype(F32)).astype(BF16)
    return hi, mid, lo


def _gelu_tanh(x):
    return 0.5 * x * (1.0 + jnp.tanh(0.7978845608028654 * (x + 0.044715 * (x * x * x))))


def _log_sigmoid(z):
    return jnp.minimum(z, 0.0) - jnp.log(1.0 + jnp.exp(-jnp.abs(z)))


def _layer_norm(x, g, b):
    mu = jnp.mean(x, axis=-1, keepdims=True)
    xc = x - mu
    var = jnp.mean(xc * xc, axis=-1, keepdims=True)
    return xc * lax.rsqrt(var + LN_EPS) * g + b


def _rms_norm(x, g):
    return x * lax.rsqrt(jnp.mean(x * x, axis=-1, keepdims=True) + LN_EPS) * g


def _in_proj_kernel(n_prompt_tiles, tiles_per_seq,
                    x_ref, w_ref, bf_ref, lnvg_ref, lnvb_ref, wmix_ref, bmix_ref, gnb_ref, tri_ref,
                    kp_in, vp_in, ks_in, vs_in,
                    q_ref, kp_ref, vp_ref, ks_ref, vs_ref, logf_ref, dT_ref, dcol_ref, bn_ref, gvn_ref,
                    carry_ref):
    del kp_in, vp_in, ks_in, vs_in
    i = pl.program_id(0)
    tm = x_ref.shape[0]
    is_sample = i >= n_prompt_tiles
    var = is_sample.astype(jnp.int32)

    p = jnp.dot(x_ref[...].astype(BF16), w_ref[...], preferred_element_type=F32)
    q = p[:, 0:D_A]
    k = p[:, D_A:2 * D_A]
    v = p[:, 2 * D_A:3 * D_A]
    u = p[:, 3 * D_A:3 * D_A + D_B]
    gv = p[:, 3 * D_A + D_B:3 * D_A + 2 * D_B]
    fl = p[:, 3 * D_A + 2 * D_B:]

    q_ref[...] = q.astype(BF16)

    @pl.when(jnp.logical_not(is_sample))
    def _():
        kp_ref[0] = k
        vp_ref[0] = v

    @pl.when(is_sample)
    def _():
        ks_ref[0] = k
        vs_ref[0] = v

    logf = _log_sigmoid(fl + bf_ref[...])
    logf_ref[...] = logf[:, 0:H_A]
    hi, mid, lo = _split3(logf)
    parts = jnp.concatenate([hi, mid, lo], axis=1)
    cs = jnp.dot(tri_ref[var], parts, preferred_element_type=F32)
    cs = cs[:, 0:LANES] + cs[:, LANES:2 * LANES] + cs[:, 2 * LANES:3 * LANES]

    @pl.when(jnp.logical_or(is_sample, i % tiles_per_seq == 0))
    def _():
        carry_ref[...] = jnp.zeros_like(carry_ref)

    d = cs + carry_ref[...]
    carry_ref[...] = d[tm - 1:tm, :]
    dcol_ref[...] = d[:, 0:H_A]
    dT_ref[...] = d.T[0:H_A, :]

    ug = _gelu_tanh(u)
    vn = _layer_norm(_gelu_tanh(gv), lnvg_ref[...], lnvb_ref[...])

    @pl.when(is_sample)
    def _():
        gvn_ref[...] = vn

    r_io = lax.broadcasted_iota(jnp.int32, (GMLP_CHUNK, GMLP_CHUNK), 0)
    c_io = lax.broadcasted_iota(jnp.int32, (GMLP_CHUNK, GMLP_CHUNK), 1)
    causal = r_io >= c_io
    half = GMLP_CHUNK // 2
    same_half = (r_io >= half) == (c_io >= half)
    keep = jnp.logical_and(causal, jnp.logical_or(jnp.logical_not(is_sample), same_half))
    lane = lax.broadcasted_iota(jnp.int32, (GMLP_CHUNK, LANES), 1)
    low_lanes = lane < C_B
    vnb = vn.astype(BF16)
    bias = bmix_ref[var]
    mixed_rows = []
    for r in range(tm // GMLP_CHUNK):
        rows = slice(r * GMLP_CHUNK, (r + 1) * GMLP_CHUNK)
        cols_out = []
        for j in range(G_B // 2):
            vj = vnb[rows, j * LANES:(j + 1) * LANES]
            m0 = jnp.where(keep, wmix_ref[var, 2 * j], 0.0).astype(BF16)
            m1 = jnp.where(keep, wmix_ref[var, 2 * j + 1], 0.0).astype(BF16)
            y0 = jnp.dot(m0, vj, preferred_element_type=F32)
            y1 = jnp.dot(m1, vj, preferred_element_type=F32)
            cols_out.append(jnp.where(low_lanes, y0, y1))
        mixed_rows.append(jnp.concatenate(cols_out, axis=1) + bias)
    mixed = jnp.concatenate(mixed_rows, axis=0)
    b_out = ug * mixed
    bn_ref[...] = _rms_norm(b_out, gnb_ref[...]).astype(BF16)


def _in_proj(layer, x, w, bf, lnvg, lnvb, wmix, bmix, gnb, tri, kp, vp, ks, vs, n_prompt, seq):
    t_rows = x.shape[0]
    tm = TOKEN_TILE
    n_tiles = t_rows // tm
    npt = n_prompt // tm
    n_sample = t_rows - n_prompt
    tiles_per_seq = seq // tm
    full = lambda a: pl.BlockSpec(a.shape, lambda i: (0,) * a.ndim)
    any_spec = pl.BlockSpec(memory_space=pl.ANY)
    p_idx = lambda i: (layer, jnp.minimum(i, npt - 1), 0)
    s_idx = lambda i: (layer, jnp.maximum(i - npt, 0), 0)
    out_shapes = (
        jax.ShapeDtypeStruct((t_rows, D_A), BF16),
        jax.ShapeDtypeStruct(kp.shape, F32),
        jax.ShapeDtypeStruct(vp.shape, F32),
        jax.ShapeDtypeStruct(ks.shape, F32),
        jax.ShapeDtypeStruct(vs.shape, F32),
        jax.ShapeDtypeStruct((t_rows, H_A), F32),
        jax.ShapeDtypeStruct((H_A, t_rows), F32),
        jax.ShapeDtypeStruct((t_rows, H_A), F32),
        jax.ShapeDtypeStruct((t_rows, D_B), BF16),
        jax.ShapeDtypeStruct((n_sample, D_B), F32),
    )
    out_specs = (
        pl.BlockSpec((tm, D_A), lambda i: (i, 0)),
        pl.BlockSpec((1, tm, D_A), p_idx),
        pl.BlockSpec((1, tm, D_A), p_idx),
        pl.BlockSpec((1, tm, D_A), s_idx),
        pl.BlockSpec((1, tm, D_A), s_idx),
        pl.BlockSpec((tm, H_A), lambda i: (i, 0)),
        pl.BlockSpec((H_A, tm), lambda i: (0, i)),
        pl.BlockSpec((tm, H_A), lambda i: (i, 0)),
        pl.BlockSpec((tm, D_B), lambda i: (i, 0)),
        pl.BlockSpec((tm, D_B), lambda i: (jnp.maximum(i - npt, 0), 0)),
    )
    in_specs = [pl.BlockSpec((tm, D_MODEL), lambda i: (i, 0)),
                full(w), full(bf), full(lnvg), full(lnvb), full(wmix), full(bmix), full(gnb), full(tri),
                any_spec, any_spec, any_spec, any_spec]
    return pl.pallas_call(
        functools.partial(_in_proj_kernel, npt, tiles_per_seq),
        grid=(n_tiles,),
        in_specs=in_specs,
        out_specs=out_specs,
        out_shape=out_shapes,
        scratch_shapes=[pltpu.VMEM((1, LANES), F32)],
        input_output_aliases={9: 1, 10: 2, 11: 3, 12: 4},
        compiler_params=_cparams("arbitrary"),
        name="in_proj",
    )(x, w, bf, lnvg, lnvb, wmix, bmix, gnb, tri, kp, vp, ks, vs)


def _head_masks(rows):
    lane = lax.broadcasted_iota(jnp.int32, (rows, LANES), 1)
    return lane < DH_A


def _flash_update(q2, k2, v2, bias_a, bias_b, mask, m_ref, l_ref, acc_ref):
    tq = q2.shape[0]
    low = _head_masks(tq)
    zero = jnp.zeros_like(q2)
    nt = (((1,), (1,)), ((), ()))
    pvs = []
    alphas = []
    for h, (qh, bias) in enumerate(((jnp.where(low, q2, zero), bias_a), (jnp.where(low, zero, q2), bias_b))):
        s = lax.dot_general(qh, k2, nt, preferred_element_type=F32) + bias
        if mask is not None:
            s = jnp.where(mask, s, NEG_BIG)
        m_prev = m_ref[h]
        m_cur = jnp.max(s, axis=1, keepdims=True)
        m_new = jnp.maximum(m_prev, m_cur)
        alpha = jnp.exp(m_prev - m_new)
        p = jnp.exp(s - m_new[:, 0:1])
        l_ref[h] = alpha * l_ref[h] + jnp.sum(p, axis=1, keepdims=True)
        m_ref[h] = m_new
        pvs.append(jnp.dot(p.astype(BF16), v2, preferred_element_type=F32))
        alphas.append(alpha)
    acc_ref[...] = jnp.where(low, alphas[0], alphas[1]) * acc_ref[...] + jnp.where(low, pvs[0], pvs[1])


def _flash_init(m_ref, l_ref, acc_ref):
    m_ref[...] = jnp.full_like(m_ref, -jnp.inf)
    l_ref[...] = jnp.zeros_like(l_ref)
    acc_ref[...] = jnp.zeros_like(acc_ref)


def _flash_finish(o_ref, l_ref, acc_ref):
    low = _head_masks(acc_ref.shape[0])
    o_ref[...] = (acc_ref[...] / jnp.where(low, l_ref[0], l_ref[1])).astype(o_ref.dtype)


def _fox_prompt_kernel(qi_tab, ki_tab, q_ref, k_ref, v_ref, dq_ref, dk_ref, o_ref,
                       m_ref, l_ref, acc_ref, dqc_ref):
    t = pl.program_id(2)
    qi = qi_tab[t]
    ki = ki_tab[t]
    tq = q_ref.shape[0]
    tk = k_ref.shape[1]

    @pl.when(ki == 0)
    def _():
        _flash_init(m_ref, l_ref, acc_ref)
        dq8 = dq_ref[0]
        dqc_ref[...] = jnp.concatenate([dq8, jnp.zeros((LANES - SUBLANES, tq), F32)], axis=0).T

    dk = dk_ref[0]
    bias_a = dqc_ref[:, 0:1] - dk[0:1, :]
    bias_b = dqc_ref[:, 1:2] - dk[1:2, :]
    k2 = k_ref[0].astype(BF16)
    v2 = v_ref[0].astype(BF16)

    @pl.when(ki < qi)
    def _():
        _flash_update(q_ref[...], k2, v2, bias_a, bias_b, None, m_ref, l_ref, acc_ref)

    @pl.when(ki == qi)
    def _():
        row = lax.broadcasted_iota(jnp.int32, (tq, tk), 0)
        col = lax.broadcasted_iota(jnp.int32, (tq, tk), 1)
        _flash_update(q_ref[...], k2, v2, bias_a, bias_b, row >= col, m_ref, l_ref, acc_ref)
        _flash_finish(o_ref, l_ref, acc_ref)


def _fox_prompt(layer, q, kp, vp, dpair, batch, seq, tile):
    t_rows = q.shape[0]
    nq = seq // tile
    pairs = [(a, b) for a in range(nq) for b in range(a + 1)]
    qi_tab = jnp.asarray([a for a, _ in pairs], jnp.int32)
    ki_tab = jnp.asarray([b for _, b in pairs], jnp.int32)
    grid_spec = pltpu.PrefetchScalarGridSpec(
        num_scalar_prefetch=2,
        grid=(batch, HEAD_PAIRS, len(pairs)),
        in_specs=[
            pl.BlockSpec((tile, LANES), lambda b, hp, t, qt, kt: (b * nq + qt[t], hp)),
            pl.BlockSpec((1, tile, LANES), lambda b, hp, t, qt, kt: (layer, b * nq + kt[t], hp)),
            pl.BlockSpec((1, tile, LANES), lambda b, hp, t, qt, kt: (layer, b * nq + kt[t], hp)),
            pl.BlockSpec((1, SUBLANES, tile), lambda b, hp, t, qt, kt: (hp, 0, b * nq + qt[t])),
            pl.BlockSpec((1, SUBLANES, tile), lambda b, hp, t, qt, kt: (hp, 0, b * nq + kt[t])),
        ],
        out_specs=pl.BlockSpec((tile, LANES), lambda b, hp, t, qt, kt: (b * nq + qt[t], hp)),
        scratch_shapes=[pltpu.VMEM((2, tile, LANES), F32), pltpu.VMEM((2, tile, LANES), F32),
                        pltpu.VMEM((tile, LANES), F32), pltpu.VMEM((tile, LANES), F32)],
    )
    return pl.pallas_call(
        _fox_prompt_kernel,
        grid_spec=grid_spec,
        out_shape=jax.ShapeDtypeStruct((t_rows, D_A), BF16),
        compiler_params=_cparams("parallel", "parallel", "arbitrary"),
        name="fox_prompt",
    )(qi_tab, ki_tab, q, kp, vp, dpair, dpair)


def _fox_sample_kernel(n_cache_tiles, q_ref, ck_ref, cv_ref, r_ref, nk_ref, nv_ref, dqc_ref, dqr_ref,
                       a_in, o_ref, m_ref, l_ref, acc_ref):
    del a_in
    s_idx = pl.program_id(2)
    tq = q_ref.shape[0]
    dq = dqc_ref[0, 0]

    @pl.when(s_idx == 0)
    def _():
        _flash_init(m_ref, l_ref, acc_ref)

    @pl.when(s_idx < n_cache_tiles)
    def _():
        r = r_ref[0, 0, 0]
        bias_a = dq[:, 0:1] + r[0:1, :]
        bias_b = dq[:, 1:2] + r[1:2, :]
        _flash_update(q_ref[...], ck_ref[0, 0].astype(BF16), cv_ref[0, 0].astype(BF16),
                      bias_a, bias_b, None, m_ref, l_ref, acc_ref)

    @pl.when(s_idx == n_cache_tiles)
    def _():
        dk = dqr_ref[0, 0]
        bias_a = dq[:, 0:1] - dk[0:1, :]
        bias_b = dq[:, 1:2] - dk[1:2, :]
        row = lax.broadcasted_iota(jnp.int32, (tq, tq), 0)
        col = lax.broadcasted_iota(jnp.int32, (tq, tq), 1)
        _flash_update(q_ref[...], nk_ref[0].astype(BF16), nv_ref[0].astype(BF16),
                      bias_a, bias_b, row >= col, m_ref, l_ref, acc_ref)
        _flash_finish(o_ref, l_ref, acc_ref)


def _fox_sample(layer, q, cache_k, cache_v, rsuf, ks, vs, dcol_s, drow_s, a_buf, n_prompt, tile):
    depth, dec_batch, past, _ = cache_k.shape
    dec_seq = ks.shape[1] // dec_batch
    nct = past // tile
    q0 = n_prompt // dec_seq
    ci = lambda b, hp, s: (layer, b, jnp.minimum(s, nct - 1), hp)
    return pl.pallas_call(
        functools.partial(_fox_sample_kernel, nct),
        grid=(dec_batch, HEAD_PAIRS, nct + 1),
        in_specs=[
            pl.BlockSpec((dec_seq, LANES), lambda b, hp, s: (q0 + b, hp)),
            pl.BlockSpec((1, 1, tile, LANES), ci),
            pl.BlockSpec((1, 1, tile, LANES), ci),
            pl.BlockSpec((1, 1, 1, 2, tile), lambda b, hp, s: (layer, b, hp, 0, jnp.minimum(s, nct - 1))),
            pl.BlockSpec((1, dec_seq, LANES), lambda b, hp, s: (layer, b, hp)),
            pl.BlockSpec((1, dec_seq, LANES), lambda b, hp, s: (layer, b, hp)),
            pl.BlockSpec((1, 1, dec_seq, 2), lambda b, hp, s: (b, hp, 0, 0)),
            pl.BlockSpec((1, 1, 2, dec_seq), lambda b, hp, s: (b, hp, 0, 0)),
            pl.BlockSpec(memory_space=pl.ANY),
        ],
        out_specs=pl.BlockSpec((dec_seq, LANES), lambda b, hp, s: (q0 + b, hp)),
        out_shape=jax.ShapeDtypeStruct(a_buf.shape, a_buf.dtype),
        scratch_shapes=[pltpu.VMEM((2, dec_seq, LANES), F32), pltpu.VMEM((2, dec_seq, LANES), F32),
                        pltpu.VMEM((dec_seq, LANES), F32)],
        input_output_aliases={8: 0},
        compiler_params=_cparams("parallel", "parallel", "arbitrary"),
        name="fox_sample",
    )(q, cache_k, cache_v, rsuf, ks, vs, dcol_s, drow_s, a_buf)


def _suffix_sum_kernel(x_ref, u_ref, o_ref, carry_ref):
    j = pl.program_id(0)

    @pl.when(j == 0)
    def _():
        carry_ref[...] = jnp.zeros_like(carry_ref)

    x = x_ref[...]
    hi, mid, lo = _split3(x)
    u = u_ref[...]
    loc = (jnp.dot(hi, u, preferred_element_type=F32) + jnp.dot(mid, u, preferred_element_type=F32)
           + jnp.dot(lo, u, preferred_element_type=F32))
    o_ref[...] = loc + carry_ref[:, 0:1]
    carry_ref[...] = carry_ref[...] + jnp.sum(x, axis=1, keepdims=True)


def _suffix_sums(x):
    rows, n = x.shape
    tb = min(512, n)
    nb = n // tb
    u = (lax.broadcasted_iota(jnp.int32, (tb, tb), 0) > lax.broadcasted_iota(jnp.int32, (tb, tb), 1)).astype(BF16)
    return pl.pallas_call(
        _suffix_sum_kernel,
        grid=(nb,),
        in_specs=[pl.BlockSpec((rows, tb), lambda j: (0, nb - 1 - j)),
                  pl.BlockSpec((tb, tb), lambda j: (0, 0))],
        out_specs=pl.BlockSpec((rows, tb), lambda j: (0, nb - 1 - j)),
        out_shape=jax.ShapeDtypeStruct((rows, n), F32),
        scratch_shapes=[pltpu.VMEM((rows, LANES), F32)],
        compiler_params=_cparams("arbitrary"),
        name="cache_suffix_sums",
    )(x, u)


def _out_proj_kernel(alpha, a_ref, bn_ref, x_ref, gna_ref, wo_ref, g1_ref, b1_ref, wr_ref, br_ref, ls_ref,
                     x1_ref, x1b_ref, route_ref, counts_ref, carry_ref):
    i = pl.program_id(0)
    tm = x_ref.shape[0]

    @pl.when(i == 0)
    def _():
        carry_ref[...] = jnp.zeros_like(carry_ref)

    an = _rms_norm(a_ref[...].astype(F32), gna_ref[...]).astype(BF16)
    mix = (jnp.dot(an, wo_ref[0:D_A, :], preferred_element_type=F32)
           + jnp.dot(bn_ref[...], wo_ref[D_A:, :], preferred_element_type=F32))
    x1 = _layer_norm(alpha * x_ref[...] + mix, g1_ref[...], b1_ref[...])
    x1_ref[...] = x1
    x1b_ref[...] = x1.astype(BF16)

    logits = jnp.dot(x1, wr_ref[...], preferred_element_type=F32, precision=lax.Precision.HIGHEST) + br_ref[...]
    lane = lax.broadcasted_iota(jnp.int32, (tm, LANES), 1)
    is_group = lane < N_GROUPS
    gl = jnp.where(is_group, logits, NEG_BIG)
    gmax = jnp.max(gl, axis=1, keepdims=True)
    g_idx = jnp.min(jnp.where(gl == gmax, lane, LANES), axis=1, keepdims=True)
    g_prob = 1.0 / jnp.sum(jnp.exp(gl - gmax), axis=1, keepdims=True)
    in_group = jnp.logical_and(lane >= N_GROUPS, lane < N_GROUPS + N_EXPERTS)
    in_group = jnp.logical_and(in_group, lax.shift_right_arithmetic(lane - N_GROUPS, 3) == g_idx)
    el = jnp.where(in_group, logits, NEG_BIG)
    e1 = jnp.max(el, axis=1, keepdims=True)
    i1 = jnp.min(jnp.where(el == e1, lane, LANES), axis=1, keepdims=True)
    el2 = jnp.where(lane == i1, NEG_BIG, el)
    e2 = jnp.max(el2, axis=1, keepdims=True)
    i2 = jnp.min(jnp.where(el2 == e2, lane, LANES), axis=1, keepdims=True)
    t2 = jnp.exp(e2 - e1)
    w1 = 1.0 / (1.0 + t2)
    gate1 = g_prob * w1
    gate2 = g_prob * (t2 * w1)

    oh1 = lane == i1
    oh2 = lane == i2
    both = (oh1.astype(F32) + oh2.astype(F32)).astype(BF16)
    before = jnp.dot(ls_ref[...], both, preferred_element_type=F32) + carry_ref[...]
    rank1 = jnp.sum(jnp.where(oh1, before, 0.0), axis=1, keepdims=True)
    rank2 = jnp.sum(jnp.where(oh2, before, 0.0), axis=1, keepdims=True)
    carry_ref[...] = carry_ref[...] + jnp.sum(both.astype(F32), axis=0, keepdims=True)
    counts_ref[...] = carry_ref[...]

    route = jnp.where(lane == 0, (i1 - N_GROUPS).astype(F32), 0.0)
    route = jnp.where(lane == 1, (i2 - N_GROUPS).astype(F32), route)
    route = jnp.where(lane == 2, rank1, route)
    route = jnp.where(lane == 3, rank2, route)
    route = jnp.where(lane == 4, gate1, route)
    route = jnp.where(lane == 5, gate2, route)
    route_ref[...] = route


def _out_proj(alpha, a, bn, x, gna, wo, g1, b1, wr, br, lstrict):
    t_rows = x.shape[0]
    tm = TOKEN_TILE
    full = lambda arr: pl.BlockSpec(arr.shape, lambda i: (0,) * arr.ndim)
    row = lambda w: pl.BlockSpec((tm, w), lambda i: (i, 0))
    return pl.pallas_call(
        functools.partial(_out_proj_kernel, alpha),
        grid=(t_rows // tm,),
        in_specs=[row(D_A), row(D_B), row(D_MODEL), full(gna), full(wo), full(g1), full(b1), full(wr), full(br),
                  full(lstrict)],
        out_specs=(row(D_MODEL), row(D_MODEL), row(LANES), pl.BlockSpec((1, LANES), lambda i: (0, 0))),
        out_shape=(jax.ShapeDtypeStruct((t_rows, D_MODEL), F32),
                   jax.ShapeDtypeStruct((t_rows, D_MODEL), BF16),
                   jax.ShapeDtypeStruct((t_rows, LANES), F32),
                   jax.ShapeDtypeStruct((1, LANES), F32)),
        scratch_shapes=[pltpu.VMEM((1, LANES), F32)],
        compiler_params=_cparams("arbitrary"),
        name="out_proj_router",
    )(a, bn, x, gna, wo, g1, b1, wr, br, lstrict)


def _expert_kernel(be_ref, nu_ref, x_ref, w1_ref, w3_ref, w2_ref, o_ref):
    i = pl.program_id(0)

    @pl.when(i < nu_ref[0])
    def _():
        x = x_ref[...]
        h1 = jnp.dot(x, w1_ref[0, 0], preferred_element_type=F32)
        h3 = jnp.dot(x, w3_ref[0, 0], preferred_element_type=F32)
        h = (h1 * (1.0 / (1.0 + jnp.exp(-h1))) * h3).astype(BF16)
        o_ref[...] = jnp.dot(h, w2_ref[0, 0], preferred_element_type=F32).astype(o_ref.dtype)


def _experts(layer, block_expert, n_used, xs, w1, w3, w2):
    n_slots = xs.shape[0]
    bm = MOE_ROWS
    nblk = n_slots // bm
    blk = lambda i, be, nu: (jnp.minimum(i, nu[0] - 1), 0)
    wsel = lambda i, be, nu: (layer, be[jnp.minimum(i, nu[0] - 1)], 0, 0)
    grid_spec = pltpu.PrefetchScalarGridSpec(
        num_scalar_prefetch=2,
        grid=(nblk,),
        in_specs=[pl.BlockSpec((bm, D_MODEL), blk),
                  pl.BlockSpec((1, 1, D_MODEL, D_EXPERT), wsel),
                  pl.BlockSpec((1, 1, D_MODEL, D_EXPERT), wsel),
                  pl.BlockSpec((1, 1, D_EXPERT, D_MODEL), wsel)],
        out_specs=pl.BlockSpec((bm, D_MODEL), blk),
    )
    return pl.pallas_call(
        _expert_kernel,
        grid_spec=grid_spec,
        out_shape=jax.ShapeDtypeStruct((n_slots, D_MODEL), BF16),
        compiler_params=_cparams("arbitrary"),
        name="experts",
    )(block_expert, n_used, xs, w1, w3, w2)


def _combine_kernel(alpha, x1_ref, y0_ref, y1_ref, route_ref, g2_ref, b2_ref, o_ref):
    route = route_ref[...]
    y = route[:, 4:5] * y0_ref[...].astype(F32) + route[:, 5:6] * y1_ref[...].astype(F32)
    o_ref[...] = _layer_norm(alpha * x1_ref[...] + y, g2_ref[...], b2_ref[...])


def _combine(alpha, x1, y0, y1, route, g2, b2):
    t_rows = x1.shape[0]
    tm = TOKEN_TILE
    full = lambda arr: pl.BlockSpec(arr.shape, lambda i: (0,) * arr.ndim)
    row = lambda w: pl.BlockSpec((tm, w), lambda i: (i, 0))
    return pl.pallas_call(
        functools.partial(_combine_kernel, alpha),
        grid=(t_rows // tm,),
        in_specs=[row(D_MODEL), row(D_MODEL), row(D_MODEL), row(LANES), full(g2), full(b2)],
        out_specs=row(D_MODEL),
        out_shape=jax.ShapeDtypeStruct((t_rows, D_MODEL), F32),
        compiler_params=_cparams("parallel"),
        name="moe_combine",
    )(x1, y0, y1, route, g2, b2)


def _attention_tile(seq):
    for t in (512, 256, 128):
        if seq % t == 0:
            return t
    raise ValueError("sequence length must be a multiple of 128")


def kernel(x_prompt, x_sample, cache_k, cache_v, cache_logf, w_in, b_f, ln_v_g, ln_v_b, w_s, b_s,
           g_norm_a, g_norm_b, w_out, ln1_g, ln1_b, w_gr, b_gr, w_er, b_er, w1, w3, w2, ln2_g, ln2_b):
    batch, seq, _ = x_prompt.shape
    dec_batch, dec_seq, _ = x_sample.shape
    depth = w_in.shape[0]
    past = cache_k.shape[2]
    n_prompt = batch * seq
    n_sample = dec_batch * dec_seq
    t_rows = n_prompt + n_sample
    alpha = float((2 * depth) ** 0.25)
    tm = TOKEN_TILE
    assert seq % tm == 0 and n_sample % tm == 0 and tm % dec_seq == 0 and dec_seq == GMLP_CHUNK // 2
    assert past % 128 == 0

    x = jnp.concatenate([x_prompt.reshape(n_prompt, D_MODEL), x_sample.reshape(n_sample, D_MODEL)], axis=0)

    sp = (D_A, 2 * D_A, 3 * D_A, 3 * D_A + H_A, 3 * D_A + H_A + D_B)
    wq, wk, wv, wf, wu, wgv = (w_in[..., a:b] for a, b in zip((0,) + sp, sp + (w_in.shape[-1],)))
    wf_pad = jnp.pad(wf, ((0, 0), (0, 0), (0, LANES - H_A)))
    w_cat = jnp.concatenate([wq * (DH_A ** -0.5), wk, wv, wu, wgv, wf_pad], axis=-1).astype(BF16)
    bf_pad = jnp.pad(b_f, ((0, 0), (0, LANES - H_A)))[:, None, :]
    half = GMLP_CHUNK // 2
    wmix = jnp.stack([w_s, jnp.tile(w_s[:, :, :half, :half], (1, 1, 2, 2))], axis=1)
    bs_t = jnp.swapaxes(b_s, 1, 2)
    bs_var = jnp.stack([bs_t, jnp.tile(bs_t[:, :half], (1, 2, 1))], axis=1)
    bmix = jnp.repeat(bs_var, C_B, axis=-1)
    wo_b = w_out.astype(BF16)
    wr = jnp.pad(jnp.concatenate([w_gr, w_er], axis=-1), ((0, 0), (0, 0), (0, LANES - N_GROUPS - N_EXPERTS)))
    br = jnp.pad(jnp.concatenate([b_gr, b_er], axis=-1), ((0, 0), (0, LANES - N_GROUPS - N_EXPERTS)))[:, None, :]
    w1b, w3b, w2b = w1.astype(BF16), w3.astype(BF16), w2.astype(BF16)
    row2 = lambda a: a[:, None, :]

    ri = lax.broadcasted_iota(jnp.int32, (tm, tm), 0)
    ci = lax.broadcasted_iota(jnp.int32, (tm, tm), 1)
    tri = jnp.stack([ri >= ci, jnp.logical_and(ri >= ci, ri // dec_seq == ci // dec_seq)]).astype(BF16)
    lstrict = (ri > ci).astype(BF16)

    clf = jnp.transpose(cache_logf, (0, 1, 3, 2)).reshape(depth * dec_batch * H_A, past)
    rsuf = _suffix_sums(clf).reshape(depth, dec_batch, HEAD_PAIRS, 2, past)
    ck = cache_k.reshape(depth, dec_batch, past, D_A)
    cv = cache_v.reshape(depth, dec_batch, past, D_A)

    kp = jnp.zeros((depth, n_prompt, D_A), F32)
    vp = jnp.zeros((depth, n_prompt, D_A), F32)
    ks = jnp.zeros((depth, n_sample, D_A), F32)
    vs = jnp.zeros((depth, n_sample, D_A), F32)
    logfs, gvns = [], []

    bm = MOE_ROWS
    n_assign = 2 * t_rows
    nblk = -(-(n_assign + N_EXPERTS * (bm - 1)) // bm)
    n_slots = nblk * bm
    a_tile = _attention_tile(seq)
    c_tile = _attention_tile(past)

    for l in range(depth):
        q, kp, vp, ks, vs, logf, dT, dcol, bn, gvn = _in_proj(
            l, x, w_cat[l], bf_pad[l], row2(ln_v_g)[l], row2(ln_v_b)[l], wmix[l], bmix[l], row2(g_norm_b)[l], tri,
            kp, vp, ks, vs, n_prompt, seq)
        logfs.append(logf)
        gvns.append(gvn)

        dpair = jnp.pad(dT.reshape(HEAD_PAIRS, 2, t_rows), ((0, 0), (0, SUBLANES - 2), (0, 0)))
        a = _fox_prompt(l, q, kp, vp, dpair, batch, seq, a_tile)
        dcol_s = jnp.transpose(dcol[n_prompt:].reshape(dec_batch, dec_seq, HEAD_PAIRS, 2), (0, 2, 1, 3))
        drow_s = jnp.transpose(dT[:, n_prompt:].reshape(HEAD_PAIRS, 2, dec_batch, dec_seq), (2, 0, 1, 3))
        a = _fox_sample(l, q, ck, cv, rsuf, ks.reshape(depth, n_sample, D_A), vs, dcol_s, drow_s, a,
                        n_prompt, c_tile)

        x1, x1b, route, counts = _out_proj(alpha, a, bn, x, row2(g_norm_a)[l], wo_b[l], row2(ln1_g)[l],
                                           row2(ln1_b)[l], wr[l], br[l], lstrict)

        cnt = counts[0, N_GROUPS:N_GROUPS + N_EXPERTS].astype(jnp.int32)
        padded = (cnt + bm - 1) // bm * bm
        pad_ends = jnp.cumsum(padded)
        pad_starts = pad_ends - padded
        eid = route[:, 0:2].astype(jnp.int32)
        rank = route[:, 2:4].astype(jnp.int32)
        slot = pad_starts[eid] + rank
        block_expert = jnp.minimum(
            jnp.searchsorted(pad_ends, jnp.arange(nblk, dtype=jnp.int32) * bm, side='right'), N_EXPERTS - 1
        ).astype(jnp.int32)
        n_used = (pad_ends[-1:] // bm).astype(jnp.int32)

        xs = jnp.zeros((n_slots, D_MODEL), BF16)
        xs = xs.at[slot[:, 0]].set(x1b).at[slot[:, 1]].set(x1b)
        yb = _experts(l, block_expert, n_used, xs, w1b, w3b, w2b)
        y0 = jnp.take(yb, slot[:, 0], axis=0)
        y1 = jnp.take(yb, slot[:, 1], axis=0)
        x = _combine(alpha, x1, y0, y1, route, row2(ln2_g)[l], row2(ln2_b)[l])

    y_prompt = x[:n_prompt].reshape(batch, seq, D_MODEL)
    y_sample = x[n_prompt:].reshape(dec_batch, dec_seq, D_MODEL)
    logf_all = jnp.stack(logfs)
    return (y_prompt, y_sample,
            kp.reshape(depth, batch, seq, H_A, DH_A), vp.reshape(depth, batch, seq, H_A, DH_A),
            logf_all[:, :n_prompt].reshape(depth, batch, seq, H_A),
            ks.reshape(depth, dec_batch, dec_seq, H_A, DH_A), vs.reshape(depth, dec_batch, dec_seq, H_A, DH_A),
            logf_all[:, n_prompt:].reshape(depth, dec_batch, dec_seq, H_A),
            jnp.stack(gvns).reshape(depth, dec_batch, dec_seq, D_B))
```

```python
import functools

import jax
import jax.numpy as jnp
from jax import lax
from jax.experimental import pallas as pl
from jax.experimental.pallas import tpu as pltpu
from jax.experimental.pallas import tpu_sc as plsc

F32 = jnp.float32
BF16 = jnp.bfloat16

D_MODEL = 1024
D_A = 512
H_A = 8
DH_A = 64
D_B = 512
G_B = 8
C_B = 64
GMLP_CHUNK = 128
N_GROUPS = 4
EXPERTS_PER_GROUP = 8
N_EXPERTS = N_GROUPS * EXPERTS_PER_GROUP
D_EXPERT = 512
LN_EPS = 1e-5
HEAD_PAIRS = H_A // 2
HALF_MODEL = D_MODEL // 2

LANES = 128
SUBLANES = 8
VMEM_LIMIT_BYTES = 56 * 1024 * 1024
SC_CORES = 2
SC_SUBCORES = 16
SC_GATHER_ROWS = 64

TOKEN_TILE = 512
MOE_ROWS = 512
PROMPT_Q_TILE = 1024
PROMPT_K_TILE = 1024
SAMPLE_K_TILE = 2048
NEG_BIG = -1e30
LOG2E = 1.4426950408889634


def _cparams(*sem):
    return pltpu.CompilerParams(dimension_semantics=sem, vmem_limit_bytes=VMEM_LIMIT_BYTES)


def _split3(x):
    hi = x.astype(BF16)
    r1 = x - hi.astype(F32)
    mid = r1.astype(BF16)
    lo = (r1 - mid.astype(F32)).astype(BF16)
    return hi, mid, lo


def _gelu_tanh(x):
    return 0.5 * x * (1.0 + jnp.tanh(0.7978845608028654 * (x + 0.044715 * (x * x * x))))


def _log_sigmoid(z):
    return jnp.minimum(z, 0.0) - jnp.log(1.0 + jnp.exp(-jnp.abs(z)))


def _layer_norm(x, g, b):
    mu = jnp.mean(x, axis=-1, keepdims=True)
    xc = x - mu
    var = jnp.mean(xc * xc, axis=-1, keepdims=True)
    return xc * lax.rsqrt(var + LN_EPS) * g + b


def _rms_norm(x, g):
    return x * lax.rsqrt(jnp.mean(x * x, axis=-1, keepdims=True) + LN_EPS) * g


def _in_proj_kernel(n_prompt_tiles, tiles_per_seq,
                    x_ref, w_ref, bf_ref, lnvg_ref, lnvb_ref, wmix_ref, bmix_ref, gnb_ref, tri_ref, place_ref,
                    kp_in, vp_in, ks_in, vs_in,
                    q_ref, kp_ref, vp_ref, ks_ref, vs_ref, kaug_ref, vt2_ref, logf_ref, dT_ref, dcol_ref, bn_ref,
                    gvn_ref, carry_ref):
    del kp_in, vp_in, ks_in, vs_in
    i = pl.program_id(0)
    tm = x_ref.shape[0]
    is_sample = i >= n_prompt_tiles
    var = is_sample.astype(jnp.int32)

    p = jnp.dot(x_ref[...].astype(BF16), w_ref[...], preferred_element_type=F32)
    q = p[:, 0:D_A]
    k = p[:, D_A:2 * D_A]
    v = p[:, 2 * D_A:3 * D_A]
    u = p[:, 3 * D_A:3 * D_A + D_B]
    gv = p[:, 3 * D_A + D_B:3 * D_A + 2 * D_B]
    fl = p[:, 3 * D_A + 2 * D_B:]

    q_ref[...] = q.astype(BF16)
    for j in range(HEAD_PAIRS):
        vt2_ref[j] = v[:, j * LANES:(j + 1) * LANES].T.astype(BF16)

    @pl.when(jnp.logical_not(is_sample))
    def _():
        kp_ref[0] = k
        vp_ref[0] = v

    @pl.when(is_sample)
    def _():
        ks_ref[0] = k
        vs_ref[0] = v

    logf = _log_sigmoid(fl + bf_ref[...])
    logf_ref[...] = logf[:, 0:H_A]
    hi, mid, lo = _split3(logf)
    parts = jnp.concatenate([hi, mid, lo], axis=1)
    cs = jnp.dot(tri_ref[var], parts, preferred_element_type=F32)
    cs = cs[:, 0:LANES] + cs[:, LANES:2 * LANES] + cs[:, 2 * LANES:3 * LANES]

    @pl.when(jnp.logical_or(is_sample, i % tiles_per_seq == 0))
    def _():
        carry_ref[...] = jnp.zeros_like(carry_ref)

    d = cs + carry_ref[...]
    carry_ref[...] = d[tm - 1:tm, :]
    d2 = d * LOG2E
    dcol_ref[...] = d2[:, 0:H_A]
    dT_ref[...] = d2.T[0:H_A, :]

    nh, nm, nl = _split3(-d2)
    lane_t = lax.broadcasted_iota(jnp.int32, (tm, LANES), 1)
    dparts = jnp.where(lane_t < H_A, nh,
                       jnp.where(lane_t < 2 * H_A, pltpu.roll(nm, H_A, 1),
                                 jnp.where(lane_t < 3 * H_A, pltpu.roll(nl, 2 * H_A, 1),
                                           jnp.where(lane_t == 3 * H_A, 1.0, 0.0).astype(BF16))))
    aug = jnp.dot(dparts, place_ref[...], preferred_element_type=F32)
    low_t = lane_t < DH_A
    for h in range(H_A):
        kh = k[:, (h // 2) * LANES:(h // 2 + 1) * LANES]
        own = low_t if h % 2 == 0 else jnp.logical_not(low_t)
        kaug_ref[h] = jnp.where(own, kh, aug[:, h * LANES:(h + 1) * LANES]).astype(BF16)

    ug = _gelu_tanh(u)
    vn = _layer_norm(_gelu_tanh(gv), lnvg_ref[...], lnvb_ref[...])

    @pl.when(is_sample)
    def _():
        gvn_ref[...] = vn

    r_io = lax.broadcasted_iota(jnp.int32, (GMLP_CHUNK, GMLP_CHUNK), 0)
    c_io = lax.broadcasted_iota(jnp.int32, (GMLP_CHUNK, GMLP_CHUNK), 1)
    causal = r_io >= c_io
    half = GMLP_CHUNK // 2
    same_half = (r_io >= half) == (c_io >= half)
    keep = jnp.logical_and(causal, jnp.logical_or(jnp.logical_not(is_sample), same_half))
    lane = lax.broadcasted_iota(jnp.int32, (GMLP_CHUNK, LANES), 1)
    low_lanes = lane < C_B
    vnb = vn.astype(BF16)
    bias = bmix_ref[var]
    mixed_rows = []
    for r in range(tm // GMLP_CHUNK):
        rows = slice(r * GMLP_CHUNK, (r + 1) * GMLP_CHUNK)
        cols_out = []
        for j in range(G_B // 2):
            vj = vnb[rows, j * LANES:(j + 1) * LANES]
            m0 = jnp.where(keep, wmix_ref[var, 2 * j], 0.0).astype(BF16)
            m1 = jnp.where(keep, wmix_ref[var, 2 * j + 1], 0.0).astype(BF16)
            y0 = jnp.dot(m0, vj, preferred_element_type=F32)
            y1 = jnp.dot(m1, vj, preferred_element_type=F32)
            cols_out.append(jnp.where(low_lanes, y0, y1))
        mixed_rows.append(jnp.concatenate(cols_out, axis=1) + bias)
    mixed = jnp.concatenate(mixed_rows, axis=0)
    b_out = ug * mixed
    bn_ref[...] = _rms_norm(b_out, gnb_ref[...]).astype(BF16)


def _in_proj(layer, x, w, bf, lnvg, lnvb, wmix, bmix, gnb, tri, place, kp, vp, ks, vs, n_prompt, seq):
    t_rows = x.shape[0]
    tm = TOKEN_TILE
    n_tiles = t_rows // tm
    npt = n_prompt // tm
    n_sample = t_rows - n_prompt
    tiles_per_seq = seq // tm
    full = lambda a: pl.BlockSpec(a.shape, lambda i: (0,) * a.ndim)
    any_spec = pl.BlockSpec(memory_space=pl.ANY)
    p_idx = lambda i: (layer, jnp.minimum(i, npt - 1), 0)
    s_idx = lambda i: (layer, jnp.maximum(i - npt, 0), 0)
    out_shapes = (
        jax.ShapeDtypeStruct((t_rows, D_A), BF16),
        jax.ShapeDtypeStruct(kp.shape, F32),
        jax.ShapeDtypeStruct(vp.shape, F32),
        jax.ShapeDtypeStruct(ks.shape, F32),
        jax.ShapeDtypeStruct(vs.shape, F32),
        jax.ShapeDtypeStruct((H_A, t_rows, LANES), BF16),
        jax.ShapeDtypeStruct((HEAD_PAIRS, LANES, t_rows), BF16),
        jax.ShapeDtypeStruct((t_rows, H_A), F32),
        jax.ShapeDtypeStruct((H_A, t_rows), F32),
        jax.ShapeDtypeStruct((t_rows, H_A), F32),
        jax.ShapeDtypeStruct((t_rows, D_B), BF16),
        jax.ShapeDtypeStruct((n_sample, D_B), F32),
    )
    out_specs = (
        pl.BlockSpec((tm, D_A), lambda i: (i, 0)),
        pl.BlockSpec((1, tm, D_A), p_idx),
        pl.BlockSpec((1, tm, D_A), p_idx),
        pl.BlockSpec((1, tm, D_A), s_idx),
        pl.BlockSpec((1, tm, D_A), s_idx),
        pl.BlockSpec((H_A, tm, LANES), lambda i: (0, i, 0)),
        pl.BlockSpec((HEAD_PAIRS, LANES, tm), lambda i: (0, 0, i)),
        pl.BlockSpec((tm, H_A), lambda i: (i, 0)),
        pl.BlockSpec((H_A, tm), lambda i: (0, i)),
        pl.BlockSpec((tm, H_A), lambda i: (i, 0)),
        pl.BlockSpec((tm, D_B), lambda i: (i, 0)),
        pl.BlockSpec((tm, D_B), lambda i: (jnp.maximum(i - npt, 0), 0)),
    )
    in_specs = [pl.BlockSpec((tm, D_MODEL), lambda i: (i, 0)),
                full(w), full(bf), full(lnvg), full(lnvb), full(wmix), full(bmix), full(gnb), full(tri), full(place),
                any_spec, any_spec, any_spec, any_spec]
    return pl.pallas_call(
        functools.partial(_in_proj_kernel, npt, tiles_per_seq),
        grid=(n_tiles,),
        in_specs=in_specs,
        out_specs=out_specs,
        out_shape=out_shapes,
        scratch_shapes=[pltpu.VMEM((1, LANES), F32)],
        input_output_aliases={10: 1, 11: 2, 12: 3, 13: 4},
        compiler_params=_cparams("arbitrary"),
        name="in_proj",
    )(x, w, bf, lnvg, lnvb, wmix, bmix, gnb, tri, place, kp, vp, ks, vs)


def _prompt_tile(qa_ref, ka, vt_prev, key_off, masked, m_ref, l_ref, acc_ref, p_ref, alpha_ref):
    tk, tq = ka.shape[1], qa_ref.shape[1]
    nt = (((1,), (1,)), ((), ()))
    if masked:
        rel = lax.broadcasted_iota(jnp.int32, (tk, tq), 0) - lax.broadcasted_iota(jnp.int32, (tk, tq), 1)
        keep = rel <= key_off
    _prompt_pv(vt_prev, p_ref, alpha_ref, acc_ref)
    for h in range(2):
        s = lax.dot_general(ka[h], qa_ref[h], nt, preferred_element_type=F32)
        if masked:
            s = jnp.where(keep, s, NEG_BIG)
        m_prev = m_ref[h]
        m_new = jnp.maximum(m_prev, jnp.max(s, axis=0, keepdims=True))
        alpha = jnp.exp2(m_prev - m_new)
        p = jnp.exp2(s - m_new[0:1, :])
        l_ref[h] = alpha * l_ref[h] + jnp.sum(p, axis=0, keepdims=True)
        m_ref[h] = m_new
        p_ref[h] = p.astype(BF16)
        alpha_ref[h] = alpha


def _prompt_pv(vt, p_ref, alpha_ref, acc_ref):
    tq = acc_ref.shape[1]
    top = lax.broadcasted_iota(jnp.int32, (LANES, tq), 0) < DH_A
    pv0 = jnp.dot(vt, p_ref[0], preferred_element_type=F32)
    pv1 = jnp.dot(vt, p_ref[1], preferred_element_type=F32)
    acc_ref[...] = (jnp.where(top, alpha_ref[0, 0:1, :], alpha_ref[1, 0:1, :]) * acc_ref[...]
                    + jnp.where(top, pv0, pv1))


def _fox_prompt_kernel(kv_per_q, qi_tab, ki_tab, q_ref, ka_ref, vt_ref, dq_ref, a_in, o_ref,
                       qa_ref, m_ref, l_ref, acc_ref, p_ref, alpha_ref, vtp_ref):
    del a_in
    t = pl.program_id(2)
    qi = qi_tab[t]
    ki = ki_tab[t]
    tq = q_ref.shape[0]
    tk = ka_ref.shape[1]

    @pl.when(ki == 0)
    def _():
        m_ref[...] = jnp.full_like(m_ref, -jnp.inf)
        l_ref[...] = jnp.zeros_like(l_ref)
        acc_ref[...] = jnp.zeros_like(acc_ref)
        p_ref[...] = jnp.zeros_like(p_ref)
        alpha_ref[...] = jnp.zeros_like(alpha_ref)
        vtp_ref[...] = jnp.zeros_like(vtp_ref)
        dqc = jnp.concatenate([dq_ref[0], jnp.zeros((LANES - SUBLANES, tq), F32)], axis=0).T
        parts = [p.astype(F32) for p in _split3(dqc)]
        lane = lax.broadcasted_iota(jnp.int32, (tq, LANES), 1)
        q2 = q_ref[...].astype(F32)
        for h in range(2):
            base = DH_A if h == 0 else 0
            own = (lane < DH_A) if h == 0 else (lane >= DH_A)
            extra = jnp.where(jnp.logical_and(lane >= base + 3, lane < base + 6), 1.0, 0.0)
            for c in range(3):
                extra = jnp.where(lane == base + c, parts[c][:, h:h + 1], extra)
            qa_ref[h] = jnp.where(own, q2, extra).astype(BF16)

    key_off = qi * tq - ki * tk
    unmasked = (ki + 1) * tk <= qi * tq

    @pl.when(unmasked)
    def _():
        _prompt_tile(qa_ref, ka_ref[...], vtp_ref[...], key_off, False, m_ref, l_ref, acc_ref, p_ref, alpha_ref)
        vtp_ref[...] = vt_ref[0]

    @pl.when(jnp.logical_not(unmasked))
    def _():
        _prompt_tile(qa_ref, ka_ref[...], vtp_ref[...], key_off, True, m_ref, l_ref, acc_ref, p_ref, alpha_ref)
        vtp_ref[...] = vt_ref[0]

    @pl.when(ki == (qi + 1) * kv_per_q - 1)
    def _():
        _prompt_pv(vt_ref[0], p_ref, alpha_ref, acc_ref)
        top = lax.broadcasted_iota(jnp.int32, (LANES, tq), 0) < DH_A
        out_t = acc_ref[...] / jnp.where(top, l_ref[0, 0:1, :], l_ref[1, 0:1, :])
        o_ref[...] = out_t.T.astype(o_ref.dtype)


def _fox_prompt(q, kaug, vt2, dpair, batch, seq, tq, tk):
    t_rows = q.shape[0]
    nq, nk = seq // tq, seq // tk
    kv_per_q = tq // tk
    pairs = [(a, b) for a in range(nq) for b in range((a + 1) * kv_per_q)]
    qi_tab = jnp.asarray([a for a, _ in pairs], jnp.int32)
    ki_tab = jnp.asarray([b for _, b in pairs], jnp.int32)
    grid_spec = pltpu.PrefetchScalarGridSpec(
        num_scalar_prefetch=2,
        grid=(batch, HEAD_PAIRS, len(pairs)),
        in_specs=[
            pl.BlockSpec((tq, LANES), lambda b, hp, t, qt, kt: (b * nq + qt[t], hp)),
            pl.BlockSpec((2, tk, LANES), lambda b, hp, t, qt, kt: (hp, b * nk + kt[t], 0)),
            pl.BlockSpec((1, LANES, tk), lambda b, hp, t, qt, kt: (hp, 0, b * nk + kt[t])),
            pl.BlockSpec((1, SUBLANES, tq), lambda b, hp, t, qt, kt: (hp, 0, b * nq + qt[t])),
            pl.BlockSpec(memory_space=pl.ANY),
        ],
        out_specs=pl.BlockSpec((tq, LANES), lambda b, hp, t, qt, kt: (b * nq + qt[t], hp)),
        scratch_shapes=[pltpu.VMEM((2, tq, LANES), BF16),
                        pltpu.VMEM((2, SUBLANES, tq), F32), pltpu.VMEM((2, SUBLANES, tq), F32),
                        pltpu.VMEM((LANES, tq), F32),
                        pltpu.VMEM((2, tk, tq), BF16), pltpu.VMEM((2, SUBLANES, tq), F32),
                        pltpu.VMEM((LANES, tk), BF16)],
    )
    return pl.pallas_call(
        functools.partial(_fox_prompt_kernel, kv_per_q),
        grid_spec=grid_spec,
        out_shape=jax.ShapeDtypeStruct((t_rows, D_A), BF16),
        input_output_aliases={6: 0},
        compiler_params=_cparams("parallel", "parallel", "arbitrary"),
        name="fox_prompt",
    )(qi_tab, ki_tab, q, kaug, vt2, dpair, jnp.zeros((t_rows, D_A), BF16))


def _sample_tile(q2, k2, v2, bias_a, bias_b, mask, m_ref, l_ref, acc_ref):
    tq = q2.shape[0]
    low = lax.broadcasted_iota(jnp.int32, (tq, LANES), 1) < DH_A
    zero = jnp.zeros_like(q2)
    nt = (((1,), (1,)), ((), ()))
    pvs, alphas = [], []
    for h, (qh, bias) in enumerate(((jnp.where(low, q2, zero), bias_a), (jnp.where(low, zero, q2), bias_b))):
        s = lax.dot_general(qh, k2, nt, preferred_element_type=F32) + bias
        if mask is not None:
            s = jnp.where(mask, s, NEG_BIG)
        m_prev = m_ref[h]
        m_new = jnp.maximum(m_prev, jnp.max(s, axis=1, keepdims=True))
        alpha = jnp.exp2(m_prev - m_new)
        p = jnp.exp2(s - m_new[:, 0:1])
        l_ref[h] = alpha * l_ref[h] + jnp.sum(p, axis=1, keepdims=True)
        m_ref[h] = m_new
        pvs.append(jnp.dot(p.astype(BF16), v2, preferred_element_type=F32))
        alphas.append(alpha)
    acc_ref[...] = jnp.where(low, alphas[0], alphas[1]) * acc_ref[...] + jnp.where(low, pvs[0], pvs[1])


def _fox_sample_kernel(n_cache_tiles, q_ref, ck_ref, cv_ref, r_ref, nk_ref, nv_ref, dqc_ref, dqr_ref,
                       a_in, o_ref, m_ref, l_ref, acc_ref):
    del a_in
    s_idx = pl.program_id(1)
    tq = q_ref.shape[0]

    @pl.when(s_idx == 0)
    def _():
        m_ref[...] = jnp.full_like(m_ref, -jnp.inf)
        l_ref[...] = jnp.zeros_like(l_ref)
        acc_ref[...] = jnp.zeros_like(acc_ref)

    dq = dqc_ref[...]

    def pair_refs(p):
        cols = pl.ds(p * LANES, LANES)
        return m_ref.at[p], l_ref.at[p], acc_ref.at[:, cols]

    @pl.when(s_idx < n_cache_tiles)
    def _():
        r = r_ref[0, 0]
        for p in range(HEAD_PAIRS):
            cols = slice(p * LANES, (p + 1) * LANES)
            bias_a = dq[:, 2 * p:2 * p + 1] + r[2 * p:2 * p + 1, :]
            bias_b = dq[:, 2 * p + 1:2 * p + 2] + r[2 * p + 1:2 * p + 2, :]
            _sample_tile(q_ref[:, cols], ck_ref[0, 0, :, cols].astype(BF16), cv_ref[0, 0, :, cols].astype(BF16),
                         bias_a, bias_b, None, *pair_refs(p))

    @pl.when(s_idx == n_cache_tiles)
    def _():
        dk = dqr_ref[0]
        causal = lax.broadcasted_iota(jnp.int32, (tq, tq), 0) >= lax.broadcasted_iota(jnp.int32, (tq, tq), 1)
        for p in range(HEAD_PAIRS):
            cols = slice(p * LANES, (p + 1) * LANES)
            bias_a = dq[:, 2 * p:2 * p + 1] - dk[2 * p:2 * p + 1, :]
            bias_b = dq[:, 2 * p + 1:2 * p + 2] - dk[2 * p + 1:2 * p + 2, :]
            _sample_tile(q_ref[:, cols], nk_ref[0, :, cols].astype(BF16), nv_ref[0, :, cols].astype(BF16),
                         bias_a, bias_b, causal, *pair_refs(p))
        low = lax.broadcasted_iota(jnp.int32, (tq, LANES), 1) < DH_A
        for p in range(HEAD_PAIRS):
            cols = slice(p * LANES, (p + 1) * LANES)
            o_ref[:, cols] = (acc_ref[:, cols] / jnp.where(low, l_ref[p, 0], l_ref[p, 1])).astype(o_ref.dtype)


def _fox_sample(layer, q, cache_k, cache_v, rsuf, ks, vs, dcol, drow_s, a_buf, n_prompt, tile):
    depth, dec_batch, past, _ = cache_k.shape
    dec_seq = ks.shape[1] // dec_batch
    nct = past // tile
    q0 = n_prompt // dec_seq
    ci = lambda b, s: (layer, b, jnp.minimum(s, nct - 1), 0)
    return pl.pallas_call(
        functools.partial(_fox_sample_kernel, nct),
        grid=(dec_batch, nct + 1),
        in_specs=[
            pl.BlockSpec((dec_seq, D_A), lambda b, s: (q0 + b, 0)),
            pl.BlockSpec((1, 1, tile, D_A), ci),
            pl.BlockSpec((1, 1, tile, D_A), ci),
            pl.BlockSpec((1, 1, H_A, tile), lambda b, s: (layer, b, 0, jnp.minimum(s, nct - 1))),
            pl.BlockSpec((1, dec_seq, D_A), lambda b, s: (layer, b, 0)),
            pl.BlockSpec((1, dec_seq, D_A), lambda b, s: (layer, b, 0)),
            pl.BlockSpec((dec_seq, H_A), lambda b, s: (q0 + b, 0)),
            pl.BlockSpec((1, H_A, dec_seq), lambda b, s: (b, 0, 0)),
            pl.BlockSpec(memory_space=pl.ANY),
        ],
        out_specs=pl.BlockSpec((dec_seq, D_A), lambda b, s: (q0 + b, 0)),
        out_shape=jax.ShapeDtypeStruct(a_buf.shape, a_buf.dtype),
        scratch_shapes=[pltpu.VMEM((HEAD_PAIRS, 2, dec_seq, LANES), F32),
                        pltpu.VMEM((HEAD_PAIRS, 2, dec_seq, LANES), F32),
                        pltpu.VMEM((dec_seq, D_A), F32)],
        input_output_aliases={8: 0},
        compiler_params=_cparams("parallel", "arbitrary"),
        name="fox_sample",
    )(q, cache_k, cache_v, rsuf, ks, vs, dcol, drow_s, a_buf)


def _suffix_sum_kernel(x_ref, u_ref, o_ref, carry_ref):
    j = pl.program_id(0)

    @pl.when(j == 0)
    def _():
        carry_ref[...] = jnp.zeros_like(carry_ref)

    x = x_ref[...]
    hi, mid, lo = _split3(x)
    u = u_ref[...]
    loc = (jnp.dot(hi, u, preferred_element_type=F32) + jnp.dot(mid, u, preferred_element_type=F32)
           + jnp.dot(lo, u, preferred_element_type=F32))
    o_ref[...] = (loc + carry_ref[:, 0:1]) * LOG2E
    carry_ref[...] = carry_ref[...] + jnp.sum(x, axis=1, keepdims=True)


def _suffix_sums(x):
    rows, n = x.shape
    tb = min(512, n)
    nb = n // tb
    u = (lax.broadcasted_iota(jnp.int32, (tb, tb), 0) > lax.broadcasted_iota(jnp.int32, (tb, tb), 1)).astype(BF16)
    return pl.pallas_call(
        _suffix_sum_kernel,
        grid=(nb,),
        in_specs=[pl.BlockSpec((rows, tb), lambda j: (0, nb - 1 - j)),
                  pl.BlockSpec((tb, tb), lambda j: (0, 0))],
        out_specs=pl.BlockSpec((rows, tb), lambda j: (0, nb - 1 - j)),
        out_shape=jax.ShapeDtypeStruct((rows, n), F32),
        scratch_shapes=[pltpu.VMEM((rows, LANES), F32)],
        compiler_params=_cparams("arbitrary"),
        name="cache_suffix_sums",
    )(x, u)


def _out_proj_kernel(alpha, a_ref, bn_ref, x_ref, gna_ref, wo_ref, g1_ref, b1_ref, wrh_ref, wrl_ref, br_ref, ls_ref,
                     x1_ref, x1b_ref, route_ref, counts_ref, carry_ref):
    i = pl.program_id(0)
    tm = x_ref.shape[0]

    @pl.when(i == 0)
    def _():
        carry_ref[...] = jnp.zeros_like(carry_ref)

    an = _rms_norm(a_ref[...].astype(F32), gna_ref[...]).astype(BF16)
    mix = (jnp.dot(an, wo_ref[0:D_A, :], preferred_element_type=F32)
           + jnp.dot(bn_ref[...], wo_ref[D_A:, :], preferred_element_type=F32))
    x1 = _layer_norm(alpha * x_ref[...] + mix, g1_ref[...], b1_ref[...])
    x1_ref[...] = x1
    x1h = x1.astype(BF16)
    x1b_ref[...] = _pack_halves(x1)

    x1l = (x1 - x1h.astype(F32)).astype(BF16)
    logits = (jnp.dot(x1h, wrh_ref[...], preferred_element_type=F32)
              + jnp.dot(x1l, wrh_ref[...], preferred_element_type=F32)
              + jnp.dot(x1h, wrl_ref[...], preferred_element_type=F32)) + br_ref[...]
    lane = lax.broadcasted_iota(jnp.int32, (tm, LANES), 1)
    is_group = lane < N_GROUPS
    gl = jnp.where(is_group, logits, NEG_BIG)
    gmax = jnp.max(gl, axis=1, keepdims=True)
    g_idx = jnp.min(jnp.where(gl == gmax, lane, LANES), axis=1, keepdims=True)
    g_prob = 1.0 / jnp.sum(jnp.exp(gl - gmax), axis=1, keepdims=True)
    in_group = jnp.logical_and(lane >= N_GROUPS, lane < N_GROUPS + N_EXPERTS)
    in_group = jnp.logical_and(in_group, lax.shift_right_arithmetic(lane - N_GROUPS, 3) == g_idx)
    el = jnp.where(in_group, logits, NEG_BIG)
    e1 = jnp.max(el, axis=1, keepdims=True)
    i1 = jnp.min(jnp.where(el == e1, lane, LANES), axis=1, keepdims=True)
    el2 = jnp.where(lane == i1, NEG_BIG, el)
    e2 = jnp.max(el2, axis=1, keepdims=True)
    i2 = jnp.min(jnp.where(el2 == e2, lane, LANES), axis=1, keepdims=True)
    t2 = jnp.exp(e2 - e1)
    w1 = 1.0 / (1.0 + t2)
    gate1 = g_prob * w1
    gate2 = g_prob * (t2 * w1)

    oh1 = lane == i1
    oh2 = lane == i2
    both = (oh1.astype(F32) + oh2.astype(F32)).astype(BF16)
    before = jnp.dot(ls_ref[...], both, preferred_element_type=F32) + carry_ref[...]
    rank1 = jnp.sum(jnp.where(oh1, before, 0.0), axis=1, keepdims=True)
    rank2 = jnp.sum(jnp.where(oh2, before, 0.0), axis=1, keepdims=True)
    carry_ref[...] = carry_ref[...] + jnp.sum(both.astype(F32), axis=0, keepdims=True)
    counts_ref[...] = carry_ref[...]

    route = jnp.where(lane == 0, (i1 - N_GROUPS).astype(F32), 0.0)
    route = jnp.where(lane == 1, (i2 - N_GROUPS).astype(F32), route)
    route = jnp.where(lane == 2, rank1, route)
    route = jnp.where(lane == 3, rank2, route)
    route = jnp.where(lane == 4, gate1, route)
    route = jnp.where(lane == 5, gate2, route)
    route_ref[...] = route


def _out_proj(alpha, a, bn, x, gna, wo, g1, b1, wrh, wrl, br, lstrict):
    t_rows = x.shape[0]
    tm = TOKEN_TILE
    full = lambda arr: pl.BlockSpec(arr.shape, lambda i: (0,) * arr.ndim)
    row = lambda w: pl.BlockSpec((tm, w), lambda i: (i, 0))
    return pl.pallas_call(
        functools.partial(_out_proj_kernel, alpha),
        grid=(t_rows // tm,),
        in_specs=[row(D_A), row(D_B), row(D_MODEL), full(gna), full(wo), full(g1), full(b1), full(wrh), full(wrl),
                  full(br), full(lstrict)],
        out_specs=(row(D_MODEL), row(HALF_MODEL), row(LANES), pl.BlockSpec((1, LANES), lambda i: (0, 0))),
        out_shape=(jax.ShapeDtypeStruct((t_rows, D_MODEL), F32),
                   jax.ShapeDtypeStruct((t_rows, HALF_MODEL), jnp.int32),
                   jax.ShapeDtypeStruct((t_rows, LANES), F32),
                   jax.ShapeDtypeStruct((1, LANES), F32)),
        scratch_shapes=[pltpu.VMEM((1, LANES), F32)],
        compiler_params=_cparams("arbitrary"),
        name="out_proj_router",
    )(a, bn, x, gna, wo, g1, b1, wrh, wrl, br, lstrict)


def _pack_halves(x):
    lo = lax.bitcast_convert_type(x[:, :HALF_MODEL].astype(BF16).astype(F32), jnp.uint32)
    hi = lax.bitcast_convert_type(x[:, HALF_MODEL:].astype(BF16).astype(F32), jnp.uint32)
    return lax.bitcast_convert_type((hi & jnp.uint32(0xFFFF0000)) | (lo >> 16), jnp.int32)


def _unpack_halves(w):
    u = lax.bitcast_convert_type(w, jnp.uint32)
    return (lax.bitcast_convert_type(u << 16, F32), lax.bitcast_convert_type(u & jnp.uint32(0xFFFF0000), F32))


def _expert_kernel(be_ref, nu_ref, x_ref, w1_ref, w3_ref, w2_ref, o_ref):
    i = pl.program_id(0)

    @pl.when(i < nu_ref[0])
    def _():
        x_lo, x_hi = (v.astype(BF16) for v in _unpack_halves(x_ref[...]))
        h1 = (jnp.dot(x_lo, w1_ref[0, 0, :HALF_MODEL, :], preferred_element_type=F32)
              + jnp.dot(x_hi, w1_ref[0, 0, HALF_MODEL:, :], preferred_element_type=F32))
        h3 = (jnp.dot(x_lo, w3_ref[0, 0, :HALF_MODEL, :], preferred_element_type=F32)
              + jnp.dot(x_hi, w3_ref[0, 0, HALF_MODEL:, :], preferred_element_type=F32))
        h = (h1 * (1.0 / (1.0 + jnp.exp(-h1))) * h3).astype(BF16)
        o_ref[...] = _pack_halves(jnp.dot(h, w2_ref[0, 0], preferred_element_type=F32))


def _experts(layer, block_expert, n_used, xs, w1, w3, w2):
    n_slots = xs.shape[0]
    bm = MOE_ROWS
    nblk = n_slots // bm
    blk = lambda i, be, nu: (jnp.minimum(i, nu[0] - 1), 0)
    wsel = lambda i, be, nu: (layer, be[jnp.minimum(i, nu[0] - 1)], 0, 0)
    grid_spec = pltpu.PrefetchScalarGridSpec(
        num_scalar_prefetch=2,
        grid=(nblk,),
        in_specs=[pl.BlockSpec((bm, HALF_MODEL), blk),
                  pl.BlockSpec((1, 1, D_MODEL, D_EXPERT), wsel),
                  pl.BlockSpec((1, 1, D_MODEL, D_EXPERT), wsel),
                  pl.BlockSpec((1, 1, D_EXPERT, D_MODEL), wsel)],
        out_specs=pl.BlockSpec((bm, HALF_MODEL), blk),
    )
    return pl.pallas_call(
        _expert_kernel,
        grid_spec=grid_spec,
        out_shape=jax.ShapeDtypeStruct((n_slots, HALF_MODEL), jnp.int32),
        compiler_params=_cparams("arbitrary"),
        name="experts",
    )(block_expert, n_used, xs, w1, w3, w2)


def _combine_kernel(alpha, x1_ref, y0_ref, y1_ref, route_ref, g2_ref, b2_ref, o_ref):
    route = route_ref[...]
    g0, g1 = route[:, 4:5], route[:, 5:6]
    y0_lo, y0_hi = _unpack_halves(y0_ref[...])
    y1_lo, y1_hi = _unpack_halves(y1_ref[...])
    z = jnp.concatenate([alpha * x1_ref[:, :HALF_MODEL] + (g0 * y0_lo + g1 * y1_lo),
                         alpha * x1_ref[:, HALF_MODEL:] + (g0 * y0_hi + g1 * y1_hi)], axis=1)
    o_ref[...] = _layer_norm(z, g2_ref[...], b2_ref[...])


def _combine(alpha, x1, y0, y1, route, g2, b2):
    t_rows = x1.shape[0]
    tm = TOKEN_TILE
    full = lambda arr: pl.BlockSpec(arr.shape, lambda i: (0,) * arr.ndim)
    row = lambda w: pl.BlockSpec((tm, w), lambda i: (i, 0))
    return pl.pallas_call(
        functools.partial(_combine_kernel, alpha),
        grid=(t_rows // tm,),
        in_specs=[row(D_MODEL), row(HALF_MODEL), row(HALF_MODEL), row(LANES), full(g2), full(b2)],
        out_specs=row(D_MODEL),
        out_shape=jax.ShapeDtypeStruct((t_rows, D_MODEL), F32),
        compiler_params=_cparams("parallel"),
        name="moe_combine",
    )(x1, y0, y1, route, g2, b2)


def _gather_rows(table, idx):
    n_rows, width = idx.shape[0], table.shape[1]
    workers = SC_CORES * SC_SUBCORES
    step = workers * SC_GATHER_ROWS
    n_pad = -(-n_rows // step) * step
    if n_pad != n_rows:
        idx = jnp.pad(idx, (0, n_pad - n_rows))
    per_worker = n_pad // workers
    n_chunks = per_worker // SC_GATHER_ROWS
    mesh = plsc.VectorSubcoreMesh(core_axis_name="c", subcore_axis_name="s")

    @functools.partial(
        pl.kernel, mesh=mesh,
        out_type=jax.ShapeDtypeStruct((n_pad, width), table.dtype),
        scratch_types=[pltpu.VMEM((SC_GATHER_ROWS,), jnp.int32),
                       pltpu.VMEM((SC_GATHER_ROWS, width), table.dtype),
                       pltpu.SemaphoreType.DMA],
        name="sc_gather_rows",
    )
    def gather(table_hbm, idx_hbm, out_hbm, idx_v, rows_v, sem):
        base = (lax.axis_index("s") * SC_CORES + lax.axis_index("c")) * per_worker

        @pl.loop(0, n_chunks)
        def _(c):
            off = base + c * SC_GATHER_ROWS
            pltpu.sync_copy(idx_hbm.at[pl.ds(off, SC_GATHER_ROWS)], idx_v)
            pltpu.async_copy(table_hbm.at[idx_v], rows_v, sem).wait()
            pltpu.sync_copy(rows_v, out_hbm.at[pl.ds(off, SC_GATHER_ROWS)])

    out = gather(table, idx)
    return out if n_pad == n_rows else out[:n_rows]


def _attention_tile(seq):
    for t in (512, 256, 128):
        if seq % t == 0:
            return t
    raise ValueError("sequence length must be a multiple of 128")


def kernel(x_prompt, x_sample, cache_k, cache_v, cache_logf, w_in, b_f, ln_v_g, ln_v_b, w_s, b_s,
           g_norm_a, g_norm_b, w_out, ln1_g, ln1_b, w_gr, b_gr, w_er, b_er, w1, w3, w2, ln2_g, ln2_b):
    batch, seq, _ = x_prompt.shape
    dec_batch, dec_seq, _ = x_sample.shape
    depth = w_in.shape[0]
    past = cache_k.shape[2]
    n_prompt = batch * seq
    n_sample = dec_batch * dec_seq
    t_rows = n_prompt + n_sample
    alpha = float((2 * depth) ** 0.25)
    tm = TOKEN_TILE
    assert seq % tm == 0 and n_sample % tm == 0 and tm % dec_seq == 0 and dec_seq == GMLP_CHUNK // 2
    assert past % 128 == 0

    x = jnp.concatenate([x_prompt.reshape(n_prompt, D_MODEL), x_sample.reshape(n_sample, D_MODEL)], axis=0)

    sp = (D_A, 2 * D_A, 3 * D_A, 3 * D_A + H_A, 3 * D_A + H_A + D_B)
    wq, wk, wv, wf, wu, wgv = (w_in[..., a:b] for a, b in zip((0,) + sp, sp + (w_in.shape[-1],)))
    wf_pad = jnp.pad(wf, ((0, 0), (0, 0), (0, LANES - H_A)))
    w_cat = jnp.concatenate([wq * (LOG2E * DH_A ** -0.5), wk, wv, wu, wgv, wf_pad], axis=-1).astype(BF16)
    bf_pad = jnp.pad(b_f, ((0, 0), (0, LANES - H_A)))[:, None, :]
    half = GMLP_CHUNK // 2
    wmix = jnp.stack([w_s, jnp.tile(w_s[:, :, :half, :half], (1, 1, 2, 2))], axis=1)
    bs_t = jnp.swapaxes(b_s, 1, 2)
    bs_var = jnp.stack([bs_t, jnp.tile(bs_t[:, :half], (1, 2, 1))], axis=1)
    bmix = jnp.repeat(bs_var, C_B, axis=-1)
    wo_b = w_out.astype(BF16)
    wr = jnp.pad(jnp.concatenate([w_gr, w_er], axis=-1), ((0, 0), (0, 0), (0, LANES - N_GROUPS - N_EXPERTS)))
    wrh = wr.astype(BF16)
    wrl = (wr - wrh.astype(F32)).astype(BF16)
    br = jnp.pad(jnp.concatenate([b_gr, b_er], axis=-1), ((0, 0), (0, LANES - N_GROUPS - N_EXPERTS)))[:, None, :]
    w1b, w3b, w2b = w1.astype(BF16), w3.astype(BF16), w2.astype(BF16)
    row2 = lambda a: a[:, None, :]

    ri = lax.broadcasted_iota(jnp.int32, (tm, tm), 0)
    ci = lax.broadcasted_iota(jnp.int32, (tm, tm), 1)
    tri = jnp.stack([ri >= ci, jnp.logical_and(ri >= ci, ri // dec_seq == ci // dec_seq)]).astype(BF16)
    lstrict = (ri > ci).astype(BF16)
    prow = lax.broadcasted_iota(jnp.int32, (LANES, H_A * LANES), 0)
    pcol = lax.broadcasted_iota(jnp.int32, (LANES, H_A * LANES), 1)
    phead = pcol // LANES
    poff = pcol % LANES - jnp.where(phead % 2 == 0, DH_A, 0)
    is_one = jnp.logical_and(prow == 3 * H_A, jnp.logical_and(poff >= 0, poff < 3))
    is_part = jnp.logical_and(jnp.logical_and(poff >= 3, poff < 6), prow == (poff - 3) * H_A + phead)
    place = jnp.logical_or(is_one, is_part).astype(BF16)

    clf = jnp.transpose(cache_logf, (0, 1, 3, 2)).reshape(depth * dec_batch * H_A, past)
    rsuf = _suffix_sums(clf).reshape(depth, dec_batch, H_A, past)
    ck = cache_k.reshape(depth, dec_batch, past, D_A)
    cv = cache_v.reshape(depth, dec_batch, past, D_A)

    kp = jnp.zeros((depth, n_prompt, D_A), F32)
    vp = jnp.zeros((depth, n_prompt, D_A), F32)
    ks = jnp.zeros((depth, n_sample, D_A), F32)
    vs = jnp.zeros((depth, n_sample, D_A), F32)
    logfs, gvns = [], []

    bm = MOE_ROWS
    n_assign = 2 * t_rows
    nblk = -(-(n_assign + N_EXPERTS * (bm - 1)) // bm)
    n_slots = nblk * bm
    q_tile = PROMPT_Q_TILE if seq % PROMPT_Q_TILE == 0 else _attention_tile(seq)
    k_tile = PROMPT_K_TILE if q_tile % PROMPT_K_TILE == 0 else q_tile
    c_tile = SAMPLE_K_TILE if past % SAMPLE_K_TILE == 0 else _attention_tile(past)
    slot_ids = jnp.arange(n_slots, dtype=jnp.int32)
    assign_ids = jnp.arange(n_assign, dtype=jnp.int32)

    for l in range(depth):
        q, kp, vp, ks, vs, kaug, vt2, logf, dT, dcol, bn, gvn = _in_proj(
            l, x, w_cat[l], bf_pad[l], row2(ln_v_g)[l], row2(ln_v_b)[l], wmix[l], bmix[l], row2(g_norm_b)[l], tri,
            place, kp, vp, ks, vs, n_prompt, seq)
        logfs.append(logf)
        gvns.append(gvn)

        dpair = jnp.pad(dT.reshape(HEAD_PAIRS, 2, t_rows), ((0, 0), (0, SUBLANES - 2), (0, 0)))
        a = _fox_prompt(q, kaug, vt2, dpair, batch, seq, q_tile, k_tile)
        drow_s = jnp.transpose(dT[:, n_prompt:].reshape(H_A, dec_batch, dec_seq), (1, 0, 2))
        a = _fox_sample(l, q, ck, cv, rsuf, ks, vs, dcol, drow_s, a, n_prompt, c_tile)

        x1, x1b, route, counts = _out_proj(alpha, a, bn, x, row2(g_norm_a)[l], wo_b[l], row2(ln1_g)[l],
                                           row2(ln1_b)[l], wrh[l], wrl[l], br[l], lstrict)

        cnt = counts[0, N_GROUPS:N_GROUPS + N_EXPERTS].astype(jnp.int32)
        padded = (cnt + bm - 1) // bm * bm
        pad_ends = jnp.cumsum(padded)
        pad_starts = pad_ends - padded
        starts = jnp.cumsum(cnt) - cnt
        eid = route[:, 0:2].astype(jnp.int32)
        rank = route[:, 2:4].astype(jnp.int32)
        slot = pad_starts[eid] + rank
        block_expert = jnp.minimum(
            jnp.searchsorted(pad_ends, jnp.arange(nblk, dtype=jnp.int32) * bm, side='right'), N_EXPERTS - 1
        ).astype(jnp.int32)
        n_used = (pad_ends[-1:] // bm).astype(jnp.int32)
        order = jnp.sort(eid.reshape(n_assign) * n_assign + assign_ids) % n_assign
        slot_e = block_expert[slot_ids // bm]
        slot_r = slot_ids - pad_starts[slot_e]
        src = order[jnp.clip(starts[slot_e] + slot_r, 0, n_assign - 1)] // 2
        slot_tok = jnp.where(slot_r < cnt[slot_e], src, 0)

        xs = _gather_rows(x1b, slot_tok)
        yb = _experts(l, block_expert, n_used, xs, w1b, w3b, w2b)
        y0 = _gather_rows(yb, slot[:, 0])
        y1 = _gather_rows(yb, slot[:, 1])
        x = _combine(alpha, x1, y0, y1, route, row2(ln2_g)[l], row2(ln2_b)[l])

    y_prompt = x[:n_prompt].reshape(batch, seq, D_MODEL)
    y_sample = x[n_prompt:].reshape(dec_batch, dec_seq, D_MODEL)
    logf_all = jnp.stack(logfs)
    return (y_prompt, y_sample,
            kp.reshape(depth, batch, seq, H_A, DH_A), vp.reshape(depth, batch, seq, H_A, DH_A),
            logf_all[:, :n_prompt].reshape(depth, batch, seq, H_A),
            ks.reshape(depth, dec_batch, dec_seq, H_A, DH_A), vs.reshape(depth, dec_batch, dec_seq, H_A, DH_A),
            logf_all[:, n_prompt:].reshape(depth, dec_batch, dec_seq, H_A),
            jnp.stack(gvns).reshape(depth, dec_batch, dec_seq, D_B))
```

```python
import functools

import jax
import jax.numpy as jnp
from jax import lax
from jax.experimental import pallas as pl
from jax.experimental.pallas import tpu as pltpu
from jax.experimental.pallas import tpu_sc as plsc

F32 = jnp.float32
BF16 = jnp.bfloat16

D_MODEL = 1024
D_A = 512
H_A = 8
DH_A = 64
D_B = 512
G_B = 8
C_B = 64
GMLP_CHUNK = 128
N_GROUPS = 4
EXPERTS_PER_GROUP = 8
N_EXPERTS = N_GROUPS * EXPERTS_PER_GROUP
D_EXPERT = 512
LN_EPS = 1e-5
HEAD_PAIRS = H_A // 2
HALF_MODEL = D_MODEL // 2

LANES = 128
SUBLANES = 8
VMEM_LIMIT_BYTES = 56 * 1024 * 1024
SC_CORES = 2
SC_SUBCORES = 16
SC_GATHER_ROWS = 64

TOKEN_TILE = 512
MOE_ROWS = 512
PROMPT_Q_TILE = 1024
PROMPT_K_TILE = 1024
SAMPLE_K_TILE = 2048
NEG_BIG = -1e30
LOG2E = 1.4426950408889634


def _cparams(*sem):
    return pltpu.CompilerParams(dimension_semantics=sem, vmem_limit_bytes=VMEM_LIMIT_BYTES)


def _split3(x):
    hi = x.astype(BF16)
    r1 = x - hi.astype(F32)
    mid = r1.astype(BF16)
    lo = (r1 - mid.astype(F32)).astype(BF16)
    return hi, mid, lo


def _gelu_tanh(x):
    return 0.5 * x * (1.0 + jnp.tanh(0.7978845608028654 * (x + 0.044715 * (x * x * x))))


def _log_sigmoid(z):
    return jnp.minimum(z, 0.0) - jnp.log(1.0 + jnp.exp(-jnp.abs(z)))


def _layer_norm(x, g, b):
    mu = jnp.mean(x, axis=-1, keepdims=True)
    xc = x - mu
    var = jnp.mean(xc * xc, axis=-1, keepdims=True)
    return xc * lax.rsqrt(var + LN_EPS) * g + b


def _rms_norm(x, g):
    return x * lax.rsqrt(jnp.mean(x * x, axis=-1, keepdims=True) + LN_EPS) * g


def _in_proj_kernel(n_prompt_tiles, tiles_per_seq,
                    x_ref, w_ref, bf_ref, lnvg_ref, lnvb_ref, wmix_ref, bmix_ref, gnb_ref, tri_ref, place_ref,
                    kp_in, vp_in, ks_in, vs_in,
                    q_ref, kp_ref, vp_ref, ks_ref, vs_ref, kaug_ref, vt2_ref, logf_ref, dT_ref, dcol_ref, bn_ref,
                    gvn_ref, carry_ref):
    del kp_in, vp_in, ks_in, vs_in
    i = pl.program_id(0)
    tm = x_ref.shape[0]
    is_sample = i >= n_prompt_tiles
    var = is_sample.astype(jnp.int32)

    p = jnp.dot(x_ref[...].astype(BF16), w_ref[...], preferred_element_type=F32)
    q = p[:, 0:D_A]
    k = p[:, D_A:2 * D_A]
    v = p[:, 2 * D_A:3 * D_A]
    u = p[:, 3 * D_A:3 * D_A + D_B]
    gv = p[:, 3 * D_A + D_B:3 * D_A + 2 * D_B]
    fl = p[:, 3 * D_A + 2 * D_B:]

    q_ref[...] = q.astype(BF16)
    for j in range(HEAD_PAIRS):
        vt2_ref[j] = v[:, j * LANES:(j + 1) * LANES].T.astype(BF16)

    @pl.when(jnp.logical_not(is_sample))
    def _():
        kp_ref[0] = k
        vp_ref[0] = v

    @pl.when(is_sample)
    def _():
        ks_ref[0] = k
        vs_ref[0] = v

    logf = _log_sigmoid(fl + bf_ref[...])
    logf_ref[...] = logf[:, 0:H_A]
    hi, mid, lo = _split3(logf)
    parts = jnp.concatenate([hi, mid, lo], axis=1)
    cs = jnp.dot(tri_ref[var], parts, preferred_element_type=F32)
    cs = cs[:, 0:LANES] + cs[:, LANES:2 * LANES] + cs[:, 2 * LANES:3 * LANES]

    @pl.when(jnp.logical_or(is_sample, i % tiles_per_seq == 0))
    def _():
        carry_ref[...] = jnp.zeros_like(carry_ref)

    d = cs + carry_ref[...]
    carry_ref[...] = d[tm - 1:tm, :]
    d2 = d * LOG2E
    dcol_ref[...] = d2[:, 0:H_A]
    dT_ref[...] = d2.T[0:H_A, :]

    nh, nm, nl = _split3(-d2)
    lane_t = lax.broadcasted_iota(jnp.int32, (tm, LANES), 1)
    dparts = jnp.where(lane_t < H_A, nh,
                       jnp.where(lane_t < 2 * H_A, pltpu.roll(nm, H_A, 1),
                                 jnp.where(lane_t < 3 * H_A, pltpu.roll(nl, 2 * H_A, 1),
                                           jnp.where(lane_t == 3 * H_A, 1.0, 0.0).astype(BF16))))
    aug = jnp.dot(dparts, place_ref[...], preferred_element_type=F32)
    low_t = lane_t < DH_A
    for h in range(H_A):
        kh = k[:, (h // 2) * LANES:(h // 2 + 1) * LANES]
        own = low_t if h % 2 == 0 else jnp.logical_not(low_t)
        kaug_ref[h] = jnp.where(own, kh, aug[:, h * LANES:(h + 1) * LANES]).astype(BF16)

    ug = _gelu_tanh(u)
    vn = _layer_norm(_gelu_tanh(gv), lnvg_ref[...], lnvb_ref[...])

    @pl.when(is_sample)
    def _():
        gvn_ref[...] = vn

    r_io = lax.broadcasted_iota(jnp.int32, (GMLP_CHUNK, GMLP_CHUNK), 0)
    c_io = lax.broadcasted_iota(jnp.int32, (GMLP_CHUNK, GMLP_CHUNK), 1)
    causal = r_io >= c_io
    half = GMLP_CHUNK // 2
    same_half = (r_io >= half) == (c_io >= half)
    keep = jnp.logical_and(causal, jnp.logical_or(jnp.logical_not(is_sample), same_half))
    lane = lax.broadcasted_iota(jnp.int32, (GMLP_CHUNK, LANES), 1)
    low_lanes = lane < C_B
    vnb = vn.astype(BF16)
    bias = bmix_ref[var]
    mixed_rows = []
    for r in range(tm // GMLP_CHUNK):
        rows = slice(r * GMLP_CHUNK, (r + 1) * GMLP_CHUNK)
        cols_out = []
        for j in range(G_B // 2):
            vj = vnb[rows, j * LANES:(j + 1) * LANES]
            m0 = jnp.where(keep, wmix_ref[var, 2 * j], 0.0).astype(BF16)
            m1 = jnp.where(keep, wmix_ref[var, 2 * j + 1], 0.0).astype(BF16)
            y0 = jnp.dot(m0, vj, preferred_element_type=F32)
            y1 = jnp.dot(m1, vj, preferred_element_type=F32)
            cols_out.append(jnp.where(low_lanes, y0, y1))
        mixed_rows.append(jnp.concatenate(cols_out, axis=1) + bias)
    mixed = jnp.concatenate(mixed_rows, axis=0)
    b_out = ug * mixed
    bn_ref[...] = _rms_norm(b_out, gnb_ref[...]).astype(BF16)


def _in_proj(layer, x, w, bf, lnvg, lnvb, wmix, bmix, gnb, tri, place, kp, vp, ks, vs, n_prompt, seq):
    t_rows = x.shape[0]
    tm = TOKEN_TILE
    n_tiles = t_rows // tm
    npt = n_prompt // tm
    n_sample = t_rows - n_prompt
    tiles_per_seq = seq // tm
    full = lambda a: pl.BlockSpec(a.shape, lambda i: (0,) * a.ndim)
    any_spec = pl.BlockSpec(memory_space=pl.ANY)
    p_idx = lambda i: (layer, jnp.minimum(i, npt - 1), 0)
    s_idx = lambda i: (layer, jnp.maximum(i - npt, 0), 0)
    out_shapes = (
        jax.ShapeDtypeStruct((t_rows, D_A), BF16),
        jax.ShapeDtypeStruct(kp.shape, F32),
        jax.ShapeDtypeStruct(vp.shape, F32),
        jax.ShapeDtypeStruct(ks.shape, F32),
        jax.ShapeDtypeStruct(vs.shape, F32),
        jax.ShapeDtypeStruct((H_A, t_rows, LANES), BF16),
        jax.ShapeDtypeStruct((HEAD_PAIRS, LANES, t_rows), BF16),
        jax.ShapeDtypeStruct((t_rows, H_A), F32),
        jax.ShapeDtypeStruct((H_A, t_rows), F32),
        jax.ShapeDtypeStruct((t_rows, H_A), F32),
        jax.ShapeDtypeStruct((t_rows, D_B), BF16),
        jax.ShapeDtypeStruct((n_sample, D_B), F32),
    )
    out_specs = (
        pl.BlockSpec((tm, D_A), lambda i: (i, 0)),
        pl.BlockSpec((1, tm, D_A), p_idx),
        pl.BlockSpec((1, tm, D_A), p_idx),
        pl.BlockSpec((1, tm, D_A), s_idx),
        pl.BlockSpec((1, tm, D_A), s_idx),
        pl.BlockSpec((H_A, tm, LANES), lambda i: (0, i, 0)),
        pl.BlockSpec((HEAD_PAIRS, LANES, tm), lambda i: (0, 0, i)),
        pl.BlockSpec((tm, H_A), lambda i: (i, 0)),
        pl.BlockSpec((H_A, tm), lambda i: (0, i)),
        pl.BlockSpec((tm, H_A), lambda i: (i, 0)),
        pl.BlockSpec((tm, D_B), lambda i: (i, 0)),
        pl.BlockSpec((tm, D_B), lambda i: (jnp.maximum(i - npt, 0), 0)),
    )
    in_specs = [pl.BlockSpec((tm, D_MODEL), lambda i: (i, 0)),
                full(w), full(bf), full(lnvg), full(lnvb), full(wmix), full(bmix), full(gnb), full(tri), full(place),
                any_spec, any_spec, any_spec, any_spec]
    return pl.pallas_call(
        functools.partial(_in_proj_kernel, npt, tiles_per_seq),
        grid=(n_tiles,),
        in_specs=in_specs,
        out_specs=out_specs,
        out_shape=out_shapes,
        scratch_shapes=[pltpu.VMEM((1, LANES), F32)],
        input_output_aliases={10: 1, 11: 2, 12: 3, 13: 4},
        compiler_params=_cparams("arbitrary"),
        name="in_proj",
    )(x, w, bf, lnvg, lnvb, wmix, bmix, gnb, tri, place, kp, vp, ks, vs)


def _prompt_tile(qa_ref, ka, vt_prev, key_off, masked, m_ref, l_ref, acc_ref, p_ref, alpha_ref):
    tk, tq = ka.shape[1], qa_ref.shape[1]
    nt = (((1,), (1,)), ((), ()))
    if masked:
        rel = lax.broadcasted_iota(jnp.int32, (tk, tq), 0) - lax.broadcasted_iota(jnp.int32, (tk, tq), 1)
        keep = rel <= key_off
    _prompt_pv(vt_prev, p_ref, alpha_ref, acc_ref)
    for h in range(2):
        s = lax.dot_general(ka[h], qa_ref[h], nt, preferred_element_type=F32)
        if masked:
            s = jnp.where(keep, s, NEG_BIG)
        m_prev = m_ref[h]
        m_new = jnp.maximum(m_prev, jnp.max(s, axis=0, keepdims=True))
        alpha = jnp.exp2(m_prev - m_new)
        p = jnp.exp2(s - m_new[0:1, :])
        l_ref[h] = alpha * l_ref[h] + jnp.sum(p, axis=0, keepdims=True)
        m_ref[h] = m_new
        p_ref[h] = p.astype(BF16)
        alpha_ref[h] = alpha


def _prompt_pv(vt, p_ref, alpha_ref, acc_ref):
    tq = acc_ref.shape[1]
    top = lax.broadcasted_iota(jnp.int32, (LANES, tq), 0) < DH_A
    pv0 = jnp.dot(vt, p_ref[0], preferred_element_type=F32)
    pv1 = jnp.dot(vt, p_ref[1], preferred_element_type=F32)
    acc_ref[...] = (jnp.where(top, alpha_ref[0, 0:1, :], alpha_ref[1, 0:1, :]) * acc_ref[...]
                    + jnp.where(top, pv0, pv1))


def _fox_prompt_kernel(kv_per_q, qi_tab, ki_tab, q_ref, ka_ref, vt_ref, dq_ref, a_in, o_ref,
                       qa_ref, m_ref, l_ref, acc_ref, p_ref, alpha_ref, vtp_ref):
    del a_in
    t = pl.program_id(2)
    qi = qi_tab[t]
    ki = ki_tab[t]
    tq = q_ref.shape[0]
    tk = ka_ref.shape[1]

    @pl.when(ki == 0)
    def _():
        m_ref[...] = jnp.full_like(m_ref, -jnp.inf)
        l_ref[...] = jnp.zeros_like(l_ref)
        acc_ref[...] = jnp.zeros_like(acc_ref)
        p_ref[...] = jnp.zeros_like(p_ref)
        alpha_ref[...] = jnp.zeros_like(alpha_ref)
        vtp_ref[...] = jnp.zeros_like(vtp_ref)
        dqc = jnp.concatenate([dq_ref[0], jnp.zeros((LANES - SUBLANES, tq), F32)], axis=0).T
        parts = [p.astype(F32) for p in _split3(dqc)]
        lane = lax.broadcasted_iota(jnp.int32, (tq, LANES), 1)
        q2 = q_ref[...].astype(F32)
        for h in range(2):
            base = DH_A if h == 0 else 0
            own = (lane < DH_A) if h == 0 else (lane >= DH_A)
            extra = jnp.where(jnp.logical_and(lane >= base + 3, lane < base + 6), 1.0, 0.0)
            for c in range(3):
                extra = jnp.where(lane == base + c, parts[c][:, h:h + 1], extra)
            qa_ref[h] = jnp.where(own, q2, extra).astype(BF16)

    key_off = qi * tq - ki * tk
    unmasked = (ki + 1) * tk <= qi * tq

    @pl.when(unmasked)
    def _():
        _prompt_tile(qa_ref, ka_ref[...], vtp_ref[...], key_off, False, m_ref, l_ref, acc_ref, p_ref, alpha_ref)
        vtp_ref[...] = vt_ref[0]

    @pl.when(jnp.logical_not(unmasked))
    def _():
        _prompt_tile(qa_ref, ka_ref[...], vtp_ref[...], key_off, True, m_ref, l_ref, acc_ref, p_ref, alpha_ref)
        vtp_ref[...] = vt_ref[0]

    @pl.when(ki == (qi + 1) * kv_per_q - 1)
    def _():
        _prompt_pv(vt_ref[0], p_ref, alpha_ref, acc_ref)
        top = lax.broadcasted_iota(jnp.int32, (LANES, tq), 0) < DH_A
        out_t = acc_ref[...] / jnp.where(top, l_ref[0, 0:1, :], l_ref[1, 0:1, :])
        o_ref[...] = out_t.T.astype(o_ref.dtype)


def _fox_prompt(q, kaug, vt2, dpair, batch, seq, tq, tk):
    t_rows = q.shape[0]
    nq, nk = seq // tq, seq // tk
    kv_per_q = tq // tk
    pairs = [(a, b) for a in range(nq) for b in range((a + 1) * kv_per_q)]
    qi_tab = jnp.asarray([a for a, _ in pairs], jnp.int32)
    ki_tab = jnp.asarray([b for _, b in pairs], jnp.int32)
    grid_spec = pltpu.PrefetchScalarGridSpec(
        num_scalar_prefetch=2,
        grid=(batch, HEAD_PAIRS, len(pairs)),
        in_specs=[
            pl.BlockSpec((tq, LANES), lambda b, hp, t, qt, kt: (b * nq + qt[t], hp)),
            pl.BlockSpec((2, tk, LANES), lambda b, hp, t, qt, kt: (hp, b * nk + kt[t], 0)),
            pl.BlockSpec((1, LANES, tk), lambda b, hp, t, qt, kt: (hp, 0, b * nk + kt[t])),
            pl.BlockSpec((1, SUBLANES, tq), lambda b, hp, t, qt, kt: (hp, 0, b * nq + qt[t])),
            pl.BlockSpec(memory_space=pl.ANY),
        ],
        out_specs=pl.BlockSpec((tq, LANES), lambda b, hp, t, qt, kt: (b * nq + qt[t], hp)),
        scratch_shapes=[pltpu.VMEM((2, tq, LANES), BF16),
                        pltpu.VMEM((2, SUBLANES, tq), F32), pltpu.VMEM((2, SUBLANES, tq), F32),
                        pltpu.VMEM((LANES, tq), F32),
                        pltpu.VMEM((2, tk, tq), BF16), pltpu.VMEM((2, SUBLANES, tq), F32),
                        pltpu.VMEM((LANES, tk), BF16)],
    )
    return pl.pallas_call(
        functools.partial(_fox_prompt_kernel, kv_per_q),
        grid_spec=grid_spec,
        out_shape=jax.ShapeDtypeStruct((t_rows, D_A), BF16),
        input_output_aliases={6: 0},
        compiler_params=_cparams("parallel", "parallel", "arbitrary"),
        name="fox_prompt",
    )(qi_tab, ki_tab, q, kaug, vt2, dpair, jnp.zeros((t_rows, D_A), BF16))


def _sample_tile(q2, k2, v2, bias_a, bias_b, mask, m_ref, l_ref, acc_ref):
    tq = q2.shape[0]
    low = lax.broadcasted_iota(jnp.int32, (tq, LANES), 1) < DH_A
    zero = jnp.zeros_like(q2)
    nt = (((1,), (1,)), ((), ()))
    pvs, alphas = [], []
    for h, (qh, bias) in enumerate(((jnp.where(low, q2, zero), bias_a), (jnp.where(low, zero, q2), bias_b))):
        s = lax.dot_general(qh, k2, nt, preferred_element_type=F32) + bias
        if mask is not None:
            s = jnp.where(mask, s, NEG_BIG)
        m_prev = m_ref[h]
        m_new = jnp.maximum(m_prev, jnp.max(s, axis=1, keepdims=True))
        alpha = jnp.exp2(m_prev - m_new)
        p = jnp.exp2(s - m_new[:, 0:1])
        l_ref[h] = alpha * l_ref[h] + jnp.sum(p, axis=1, keepdims=True)
        m_ref[h] = m_new
        pvs.append(jnp.dot(p.astype(BF16), v2, preferred_element_type=F32))
        alphas.append(alpha)
    acc_ref[...] = jnp.where(low, alphas[0], alphas[1]) * acc_ref[...] + jnp.where(low, pvs[0], pvs[1])


def _fox_sample_kernel(n_cache_tiles, q_ref, ck_ref, cv_ref, r_ref, nk_ref, nv_ref, dqc_ref, dqr_ref,
                       a_in, o_ref, m_ref, l_ref, acc_ref):
    del a_in
    s_idx = pl.program_id(1)
    tq = q_ref.shape[0]

    @pl.when(s_idx == 0)
    def _():
        m_ref[...] = jnp.full_like(m_ref, -jnp.inf)
        l_ref[...] = jnp.zeros_like(l_ref)
        acc_ref[...] = jnp.zeros_like(acc_ref)

    dq = dqc_ref[...]

    def pair_refs(p):
        cols = pl.ds(p * LANES, LANES)
        return m_ref.at[p], l_ref.at[p], acc_ref.at[:, cols]

    @pl.when(s_idx < n_cache_tiles)
    def _():
        r = r_ref[0, 0]
        for p in range(HEAD_PAIRS):
            cols = slice(p * LANES, (p + 1) * LANES)
            bias_a = dq[:, 2 * p:2 * p + 1] + r[2 * p:2 * p + 1, :]
            bias_b = dq[:, 2 * p + 1:2 * p + 2] + r[2 * p + 1:2 * p + 2, :]
            _sample_tile(q_ref[:, cols], ck_ref[0, 0, :, cols].astype(BF16), cv_ref[0, 0, :, cols].astype(BF16),
                         bias_a, bias_b, None, *pair_refs(p))

    @pl.when(s_idx == n_cache_tiles)
    def _():
        dk = dqr_ref[0]
        causal = lax.broadcasted_iota(jnp.int32, (tq, tq), 0) >= lax.broadcasted_iota(jnp.int32, (tq, tq), 1)
        for p in range(HEAD_PAIRS):
            cols = slice(p * LANES, (p + 1) * LANES)
            bias_a = dq[:, 2 * p:2 * p + 1] - dk[2 * p:2 * p + 1, :]
            bias_b = dq[:, 2 * p + 1:2 * p + 2] - dk[2 * p + 1:2 * p + 2, :]
            _sample_tile(q_ref[:, cols], nk_ref[0, :, cols].astype(BF16), nv_ref[0, :, cols].astype(BF16),
                         bias_a, bias_b, causal, *pair_refs(p))
        low = lax.broadcasted_iota(jnp.int32, (tq, LANES), 1) < DH_A
        for p in range(HEAD_PAIRS):
            cols = slice(p * LANES, (p + 1) * LANES)
            o_ref[:, cols] = (acc_ref[:, cols] / jnp.where(low, l_ref[p, 0], l_ref[p, 1])).astype(o_ref.dtype)


def _fox_sample(layer, q, cache_k, cache_v, rsuf, ks, vs, dcol, drow_s, a_buf, n_prompt, tile):
    depth, dec_batch, past, _ = cache_k.shape
    dec_seq = ks.shape[1] // dec_batch
    nct = past // tile
    q0 = n_prompt // dec_seq
    ci = lambda b, s: (layer, b, jnp.minimum(s, nct - 1), 0)
    return pl.pallas_call(
        functools.partial(_fox_sample_kernel, nct),
        grid=(dec_batch, nct + 1),
        in_specs=[
            pl.BlockSpec((dec_seq, D_A), lambda b, s: (q0 + b, 0)),
            pl.BlockSpec((1, 1, tile, D_A), ci),
            pl.BlockSpec((1, 1, tile, D_A), ci),
            pl.BlockSpec((1, 1, H_A, tile), lambda b, s: (layer, b, 0, jnp.minimum(s, nct - 1))),
            pl.BlockSpec((1, dec_seq, D_A), lambda b, s: (layer, b, 0)),
            pl.BlockSpec((1, dec_seq, D_A), lambda b, s: (layer, b, 0)),
            pl.BlockSpec((dec_seq, H_A), lambda b, s: (q0 + b, 0)),
            pl.BlockSpec((1, H_A, dec_seq), lambda b, s: (b, 0, 0)),
            pl.BlockSpec(memory_space=pl.ANY),
        ],
        out_specs=pl.BlockSpec((dec_seq, D_A), lambda b, s: (q0 + b, 0)),
        out_shape=jax.ShapeDtypeStruct(a_buf.shape, a_buf.dtype),
        scratch_shapes=[pltpu.VMEM((HEAD_PAIRS, 2, dec_seq, LANES), F32),
                        pltpu.VMEM((HEAD_PAIRS, 2, dec_seq, LANES), F32),
                        pltpu.VMEM((dec_seq, D_A), F32)],
        input_output_aliases={8: 0},
        compiler_params=_cparams("parallel", "arbitrary"),
        name="fox_sample",
    )(q, cache_k, cache_v, rsuf, ks, vs, dcol, drow_s, a_buf)


def _suffix_sum_kernel(x_ref, u_ref, o_ref, carry_ref):
    j = pl.program_id(0)

    @pl.when(j == 0)
    def _():
        carry_ref[...] = jnp.zeros_like(carry_ref)

    x = x_ref[...]
    hi, mid, lo = _split3(x)
    u = u_ref[...]
    loc = (jnp.dot(hi, u, preferred_element_type=F32) + jnp.dot(mid, u, preferred_element_type=F32)
           + jnp.dot(lo, u, preferred_element_type=F32))
    o_ref[...] = (loc + carry_ref[:, 0:1]) * LOG2E
    carry_ref[...] = carry_ref[...] + jnp.sum(x, axis=1, keepdims=True)


def _suffix_sums(x):
    rows, n = x.shape
    tb = min(512, n)
    nb = n // tb
    u = (lax.broadcasted_iota(jnp.int32, (tb, tb), 0) > lax.broadcasted_iota(jnp.int32, (tb, tb), 1)).astype(BF16)
    return pl.pallas_call(
        _suffix_sum_kernel,
        grid=(nb,),
        in_specs=[pl.BlockSpec((rows, tb), lambda j: (0, nb - 1 - j)),
                  pl.BlockSpec((tb, tb), lambda j: (0, 0))],
        out_specs=pl.BlockSpec((rows, tb), lambda j: (0, nb - 1 - j)),
        out_shape=jax.ShapeDtypeStruct((rows, n), F32),
        scratch_shapes=[pltpu.VMEM((rows, LANES), F32)],
        compiler_params=_cparams("arbitrary"),
        name="cache_suffix_sums",
    )(x, u)


def _out_proj_kernel(alpha, a_ref, bn_ref, x_ref, gna_ref, wo_ref, g1_ref, b1_ref, wrh_ref, wrl_ref, br_ref, ls_ref,
                     x1_ref, x1b_ref, route_ref, counts_ref, carry_ref):
    i = pl.program_id(0)
    tm = x_ref.shape[0]

    @pl.when(i == 0)
    def _():
        carry_ref[...] = jnp.zeros_like(carry_ref)

    an = _rms_norm(a_ref[...].astype(F32), gna_ref[...]).astype(BF16)
    mix = (jnp.dot(an, wo_ref[0:D_A, :], preferred_element_type=F32)
           + jnp.dot(bn_ref[...], wo_ref[D_A:, :], preferred_element_type=F32))
    x1 = _layer_norm(alpha * x_ref[...] + mix, g1_ref[...], b1_ref[...])
    x1_ref[...] = x1
    x1h = x1.astype(BF16)
    x1b_ref[...] = _pack_halves(x1)

    x1l = (x1 - x1h.astype(F32)).astype(BF16)
    logits = (jnp.dot(x1h, wrh_ref[...], preferred_element_type=F32)
              + jnp.dot(x1l, wrh_ref[...], preferred_element_type=F32)
              + jnp.dot(x1h, wrl_ref[...], preferred_element_type=F32)) + br_ref[...]
    lane = lax.broadcasted_iota(jnp.int32, (tm, LANES), 1)
    is_group = lane < N_GROUPS
    gl = jnp.where(is_group, logits, NEG_BIG)
    gmax = jnp.max(gl, axis=1, keepdims=True)
    g_idx = jnp.min(jnp.where(gl == gmax, lane, LANES), axis=1, keepdims=True)
    g_prob = 1.0 / jnp.sum(jnp.exp(gl - gmax), axis=1, keepdims=True)
    in_group = jnp.logical_and(lane >= N_GROUPS, lane < N_GROUPS + N_EXPERTS)
    in_group = jnp.logical_and(in_group, lax.shift_right_arithmetic(lane - N_GROUPS, 3) == g_idx)
    el = jnp.where(in_group, logits, NEG_BIG)
    e1 = jnp.max(el, axis=1, keepdims=True)
    i1 = jnp.min(jnp.where(el == e1, lane, LANES), axis=1, keepdims=True)
    el2 = jnp.where(lane == i1, NEG_BIG, el)
    e2 = jnp.max(el2, axis=1, keepdims=True)
    i2 = jnp.min(jnp.where(el2 == e2, lane, LANES), axis=1, keepdims=True)
    t2 = jnp.exp(e2 - e1)
    w1 = 1.0 / (1.0 + t2)
    gate1 = g_prob * w1
    gate2 = g_prob * (t2 * w1)

    oh1 = lane == i1
    oh2 = lane == i2
    both = (oh1.astype(F32) + oh2.astype(F32)).astype(BF16)
    before = jnp.dot(ls_ref[...], both, preferred_element_type=F32) + carry_ref[...]
    rank1 = jnp.sum(jnp.where(oh1, before, 0.0), axis=1, keepdims=True)
    rank2 = jnp.sum(jnp.where(oh2, before, 0.0), axis=1, keepdims=True)
    carry_ref[...] = carry_ref[...] + jnp.sum(both.astype(F32), axis=0, keepdims=True)
    counts_ref[...] = carry_ref[...]

    route = jnp.where(lane == 0, (i1 - N_GROUPS).astype(F32), 0.0)
    route = jnp.where(lane == 1, (i2 - N_GROUPS).astype(F32), route)
    route = jnp.where(lane == 2, rank1, route)
    route = jnp.where(lane == 3, rank2, route)
    route = jnp.where(lane == 4, gate1, route)
    route = jnp.where(lane == 5, gate2, route)
    route_ref[...] = route


def _out_proj(alpha, a, bn, x, gna, wo, g1, b1, wrh, wrl, br, lstrict):
    t_rows = x.shape[0]
    tm = TOKEN_TILE
    full = lambda arr: pl.BlockSpec(arr.shape, lambda i: (0,) * arr.ndim)
    row = lambda w: pl.BlockSpec((tm, w), lambda i: (i, 0))
    return pl.pallas_call(
        functools.partial(_out_proj_kernel, alpha),
        grid=(t_rows // tm,),
        in_specs=[row(D_A), row(D_B), row(D_MODEL), full(gna), full(wo), full(g1), full(b1), full(wrh), full(wrl),
                  full(br), full(lstrict)],
        out_specs=(row(D_MODEL), row(HALF_MODEL), row(LANES), pl.BlockSpec((1, LANES), lambda i: (0, 0))),
        out_shape=(jax.ShapeDtypeStruct((t_rows, D_MODEL), F32),
                   jax.ShapeDtypeStruct((t_rows, HALF_MODEL), jnp.int32),
                   jax.ShapeDtypeStruct((t_rows, LANES), F32),
                   jax.ShapeDtypeStruct((1, LANES), F32)),
        scratch_shapes=[pltpu.VMEM((1, LANES), F32)],
        compiler_params=_cparams("arbitrary"),
        name="out_proj_router",
    )(a, bn, x, gna, wo, g1, b1, wrh, wrl, br, lstrict)


def _pack_halves(x):
    lo = lax.bitcast_convert_type(x[:, :HALF_MODEL].astype(BF16).astype(F32), jnp.uint32)
    hi = lax.bitcast_convert_type(x[:, HALF_MODEL:].astype(BF16).astype(F32), jnp.uint32)
    return lax.bitcast_convert_type((hi & jnp.uint32(0xFFFF0000)) | (lo >> 16), jnp.int32)


def _unpack_halves(w):
    u = lax.bitcast_convert_type(w, jnp.uint32)
    return (lax.bitcast_convert_type(u << 16, F32), lax.bitcast_convert_type(u & jnp.uint32(0xFFFF0000), F32))


def _expert_kernel(ib_ref, ie_ref, ni_ref, st_ref, en_ref, x_ref, w1_ref, w3_ref, w2_ref, o_ref,
                   w1b_ref, w3b_ref, w2b_ref):
    i = pl.program_id(0)
    bm = x_ref.shape[0]

    @pl.when(i < ni_ref[0])
    def _():
        e = ie_ref[i]
        b = ib_ref[i]
        prev = jnp.maximum(i - 1, 0)
        first_item = i == 0

        @pl.when(jnp.logical_or(first_item, ie_ref[prev] != e))
        def _():
            w1b_ref[...] = w1_ref[0, 0].astype(BF16)
            w3b_ref[...] = w3_ref[0, 0].astype(BF16)
            w2b_ref[...] = w2_ref[0, 0].astype(BF16)

        x_lo, x_hi = (v.astype(BF16) for v in _unpack_halves(x_ref[...]))
        h1 = (jnp.dot(x_lo, w1b_ref[:HALF_MODEL, :], preferred_element_type=F32)
              + jnp.dot(x_hi, w1b_ref[HALF_MODEL:, :], preferred_element_type=F32))
        h3 = (jnp.dot(x_lo, w3b_ref[:HALF_MODEL, :], preferred_element_type=F32)
              + jnp.dot(x_hi, w3b_ref[HALF_MODEL:, :], preferred_element_type=F32))
        h = (h1 * (1.0 / (1.0 + jnp.exp(-h1))) * h3).astype(BF16)
        y = _pack_halves(jnp.dot(h, w2b_ref[...], preferred_element_type=F32))
        first_visit = jnp.logical_or(first_item, ib_ref[prev] != b)

        @pl.when(first_visit)
        def _():
            o_ref[...] = y

        @pl.when(jnp.logical_not(first_visit))
        def _():
            row = b * bm + lax.broadcasted_iota(jnp.int32, (bm, 1), 0)
            mine = jnp.logical_and(row >= st_ref[e], row < en_ref[e])
            o_ref[...] = jnp.where(mine, y, o_ref[...])


def _experts(layer, item_block, item_expert, n_items, starts, ends, xs, w1, w3, w2):
    n_rows = xs.shape[0]
    bm = MOE_ROWS
    n_max = item_block.shape[0]
    item = lambda i, ni: jnp.minimum(i, ni[0] - 1)
    blk = lambda i, ib, ie, ni, st, en: (ib[item(i, ni)], 0)
    wsel = lambda i, ib, ie, ni, st, en: (layer, ie[item(i, ni)], 0, 0)
    grid_spec = pltpu.PrefetchScalarGridSpec(
        num_scalar_prefetch=5,
        grid=(n_max,),
        in_specs=[pl.BlockSpec((bm, HALF_MODEL), blk),
                  pl.BlockSpec((1, 1, D_MODEL, D_EXPERT), wsel),
                  pl.BlockSpec((1, 1, D_MODEL, D_EXPERT), wsel),
                  pl.BlockSpec((1, 1, D_EXPERT, D_MODEL), wsel)],
        out_specs=pl.BlockSpec((bm, HALF_MODEL), blk),
        scratch_shapes=[pltpu.VMEM((D_MODEL, D_EXPERT), BF16), pltpu.VMEM((D_MODEL, D_EXPERT), BF16),
                        pltpu.VMEM((D_EXPERT, D_MODEL), BF16)],
    )
    return pl.pallas_call(
        _expert_kernel,
        grid_spec=grid_spec,
        out_shape=jax.ShapeDtypeStruct((n_rows, HALF_MODEL), jnp.int32),
        compiler_params=_cparams("arbitrary"),
        name="experts",
    )(item_block, item_expert, n_items, starts, ends, xs, w1, w3, w2)


def _combine_kernel(alpha, x1_ref, y0_ref, y1_ref, route_ref, g2_ref, b2_ref, o_ref):
    route = route_ref[...]
    g0, g1 = route[:, 4:5], route[:, 5:6]
    y0_lo, y0_hi = _unpack_halves(y0_ref[...])
    y1_lo, y1_hi = _unpack_halves(y1_ref[...])
    z = jnp.concatenate([alpha * x1_ref[:, :HALF_MODEL] + (g0 * y0_lo + g1 * y1_lo),
                         alpha * x1_ref[:, HALF_MODEL:] + (g0 * y0_hi + g1 * y1_hi)], axis=1)
    o_ref[...] = _layer_norm(z, g2_ref[...], b2_ref[...])


def _combine(alpha, x1, y0, y1, route, g2, b2):
    t_rows = x1.shape[0]
    tm = TOKEN_TILE
    full = lambda arr: pl.BlockSpec(arr.shape, lambda i: (0,) * arr.ndim)
    row = lambda w: pl.BlockSpec((tm, w), lambda i: (i, 0))
    return pl.pallas_call(
        functools.partial(_combine_kernel, alpha),
        grid=(t_rows // tm,),
        in_specs=[row(D_MODEL), row(HALF_MODEL), row(HALF_MODEL), row(LANES), full(g2), full(b2)],
        out_specs=row(D_MODEL),
        out_shape=jax.ShapeDtypeStruct((t_rows, D_MODEL), F32),
        compiler_params=_cparams("parallel"),
        name="moe_combine",
    )(x1, y0, y1, route, g2, b2)


def _gather_rows(table, idx):
    n_rows, width = idx.shape[0], table.shape[1]
    workers = SC_CORES * SC_SUBCORES
    step = workers * SC_GATHER_ROWS
    n_pad = -(-n_rows // step) * step
    if n_pad != n_rows:
        idx = jnp.pad(idx, (0, n_pad - n_rows))
    per_worker = n_pad // workers
    n_chunks = per_worker // SC_GATHER_ROWS
    mesh = plsc.VectorSubcoreMesh(core_axis_name="c", subcore_axis_name="s")

    @functools.partial(
        pl.kernel, mesh=mesh,
        out_type=jax.ShapeDtypeStruct((n_pad, width), table.dtype),
        scratch_types=[pltpu.VMEM((SC_GATHER_ROWS,), jnp.int32),
                       pltpu.VMEM((SC_GATHER_ROWS, width), table.dtype),
                       pltpu.SemaphoreType.DMA],
        name="sc_gather_rows",
    )
    def gather(table_hbm, idx_hbm, out_hbm, idx_v, rows_v, sem):
        base = (lax.axis_index("s") * SC_CORES + lax.axis_index("c")) * per_worker

        @pl.loop(0, n_chunks)
        def _(c):
            off = base + c * SC_GATHER_ROWS
            pltpu.sync_copy(idx_hbm.at[pl.ds(off, SC_GATHER_ROWS)], idx_v)
            pltpu.async_copy(table_hbm.at[idx_v], rows_v, sem).wait()
            pltpu.sync_copy(rows_v, out_hbm.at[pl.ds(off, SC_GATHER_ROWS)])

    out = gather(table, idx)
    return out if n_pad == n_rows else out[:n_rows]


def _attention_tile(seq):
    for t in (512, 256, 128):
        if seq % t == 0:
            return t
    raise ValueError("sequence length must be a multiple of 128")


def kernel(x_prompt, x_sample, cache_k, cache_v, cache_logf, w_in, b_f, ln_v_g, ln_v_b, w_s, b_s,
           g_norm_a, g_norm_b, w_out, ln1_g, ln1_b, w_gr, b_gr, w_er, b_er, w1, w3, w2, ln2_g, ln2_b):
    batch, seq, _ = x_prompt.shape
    dec_batch, dec_seq, _ = x_sample.shape
    depth = w_in.shape[0]
    past = cache_k.shape[2]
    n_prompt = batch * seq
    n_sample = dec_batch * dec_seq
    t_rows = n_prompt + n_sample
    alpha = float((2 * depth) ** 0.25)
    tm = TOKEN_TILE
    assert seq % tm == 0 and n_sample % tm == 0 and tm % dec_seq == 0 and dec_seq == GMLP_CHUNK // 2
    assert past % 128 == 0

    x = jnp.concatenate([x_prompt.reshape(n_prompt, D_MODEL), x_sample.reshape(n_sample, D_MODEL)], axis=0)

    sp = (D_A, 2 * D_A, 3 * D_A, 3 * D_A + H_A, 3 * D_A + H_A + D_B)
    wq, wk, wv, wf, wu, wgv = (w_in[..., a:b] for a, b in zip((0,) + sp, sp + (w_in.shape[-1],)))
    wf_pad = jnp.pad(wf, ((0, 0), (0, 0), (0, LANES - H_A)))
    w_cat = jnp.concatenate([wq * (LOG2E * DH_A ** -0.5), wk, wv, wu, wgv, wf_pad], axis=-1).astype(BF16)
    bf_pad = jnp.pad(b_f, ((0, 0), (0, LANES - H_A)))[:, None, :]
    half = GMLP_CHUNK // 2
    wmix = jnp.stack([w_s, jnp.tile(w_s[:, :, :half, :half], (1, 1, 2, 2))], axis=1)
    bs_t = jnp.swapaxes(b_s, 1, 2)
    bs_var = jnp.stack([bs_t, jnp.tile(bs_t[:, :half], (1, 2, 1))], axis=1)
    bmix = jnp.repeat(bs_var, C_B, axis=-1)
    wo_b = w_out.astype(BF16)
    wr = jnp.pad(jnp.concatenate([w_gr, w_er], axis=-1), ((0, 0), (0, 0), (0, LANES - N_GROUPS - N_EXPERTS)))
    wrh = wr.astype(BF16)
    wrl = (wr - wrh.astype(F32)).astype(BF16)
    br = jnp.pad(jnp.concatenate([b_gr, b_er], axis=-1), ((0, 0), (0, LANES - N_GROUPS - N_EXPERTS)))[:, None, :]
    row2 = lambda a: a[:, None, :]

    ri = lax.broadcasted_iota(jnp.int32, (tm, tm), 0)
    ci = lax.broadcasted_iota(jnp.int32, (tm, tm), 1)
    tri = jnp.stack([ri >= ci, jnp.logical_and(ri >= ci, ri // dec_seq == ci // dec_seq)]).astype(BF16)
    lstrict = (ri > ci).astype(BF16)
    prow = lax.broadcasted_iota(jnp.int32, (LANES, H_A * LANES), 0)
    pcol = lax.broadcasted_iota(jnp.int32, (LANES, H_A * LANES), 1)
    phead = pcol // LANES
    poff = pcol % LANES - jnp.where(phead % 2 == 0, DH_A, 0)
    is_one = jnp.logical_and(prow == 3 * H_A, jnp.logical_and(poff >= 0, poff < 3))
    is_part = jnp.logical_and(jnp.logical_and(poff >= 3, poff < 6), prow == (poff - 3) * H_A + phead)
    place = jnp.logical_or(is_one, is_part).astype(BF16)

    clf = jnp.transpose(cache_logf, (0, 1, 3, 2)).reshape(depth * dec_batch * H_A, past)
    rsuf = _suffix_sums(clf).reshape(depth, dec_batch, H_A, past)
    ck = cache_k.reshape(depth, dec_batch, past, D_A)
    cv = cache_v.reshape(depth, dec_batch, past, D_A)

    kp = jnp.zeros((depth, n_prompt, D_A), F32)
    vp = jnp.zeros((depth, n_prompt, D_A), F32)
    ks = jnp.zeros((depth, n_sample, D_A), F32)
    vs = jnp.zeros((depth, n_sample, D_A), F32)
    logfs, gvns = [], []

    bm = MOE_ROWS
    n_assign = 2 * t_rows
    nblk = n_assign // bm
    expert_ids = jnp.arange(N_EXPERTS, dtype=jnp.int32)
    item_ids = jnp.arange(nblk + N_EXPERTS - 1, dtype=jnp.int32)
    q_tile = PROMPT_Q_TILE if seq % PROMPT_Q_TILE == 0 else _attention_tile(seq)
    k_tile = PROMPT_K_TILE if q_tile % PROMPT_K_TILE == 0 else q_tile
    c_tile = SAMPLE_K_TILE if past % SAMPLE_K_TILE == 0 else _attention_tile(past)
    assign_ids = jnp.arange(n_assign, dtype=jnp.int32)

    for l in range(depth):
        q, kp, vp, ks, vs, kaug, vt2, logf, dT, dcol, bn, gvn = _in_proj(
            l, x, w_cat[l], bf_pad[l], row2(ln_v_g)[l], row2(ln_v_b)[l], wmix[l], bmix[l], row2(g_norm_b)[l], tri,
            place, kp, vp, ks, vs, n_prompt, seq)
        logfs.append(logf)
        gvns.append(gvn)

        dpair = jnp.pad(dT.reshape(HEAD_PAIRS, 2, t_rows), ((0, 0), (0, SUBLANES - 2), (0, 0)))
        a = _fox_prompt(q, kaug, vt2, dpair, batch, seq, q_tile, k_tile)
        drow_s = jnp.transpose(dT[:, n_prompt:].reshape(H_A, dec_batch, dec_seq), (1, 0, 2))
        a = _fox_sample(l, q, ck, cv, rsuf, ks, vs, dcol, drow_s, a, n_prompt, c_tile)

        x1, x1b, route, counts = _out_proj(alpha, a, bn, x, row2(g_norm_a)[l], wo_b[l], row2(ln1_g)[l],
                                           row2(ln1_b)[l], wrh[l], wrl[l], br[l], lstrict)

        cnt = counts[0, N_GROUPS:N_GROUPS + N_EXPERTS].astype(jnp.int32)
        ends = jnp.cumsum(cnt)
        starts = ends - cnt
        eid = route[:, 0:2].astype(jnp.int32)
        rank = route[:, 2:4].astype(jnp.int32)
        onehot = eid[:, :, None] == expert_ids[None, None, :]
        pos = jnp.sum(jnp.where(onehot, starts[None, None, :], 0), axis=-1) + rank
        order = jnp.sort(eid.reshape(n_assign) * n_assign + assign_ids) % n_assign
        first_blk = starts // bm
        n_it = jnp.where(cnt > 0, (ends - 1) // bm - first_blk + 1, 0)
        it_end = jnp.cumsum(n_it)
        item_expert = jnp.minimum(jnp.sum(item_ids[:, None] >= it_end[None, :], axis=1), N_EXPERTS - 1).astype(jnp.int32)
        it_first = jnp.sum(jnp.where(item_expert[:, None] == expert_ids[None, :], (it_end - n_it)[None, :], 0), axis=1)
        it_blk0 = jnp.sum(jnp.where(item_expert[:, None] == expert_ids[None, :], first_blk[None, :], 0), axis=1)
        item_block = jnp.clip(it_blk0 + item_ids - it_first, 0, nblk - 1).astype(jnp.int32)

        xs = _gather_rows(x1b, order // 2)
        yb = _experts(l, item_block, item_expert, it_end[-1:].astype(jnp.int32), starts, ends, xs, w1, w3, w2)
        y0 = _gather_rows(yb, pos[:, 0])
        y1 = _gather_rows(yb, pos[:, 1])
        x = _combine(alpha, x1, y0, y1, route, row2(ln2_g)[l], row2(ln2_b)[l])

    y_prompt = x[:n_prompt].reshape(batch, seq, D_MODEL)
    y_sample = x[n_prompt:].reshape(dec_batch, dec_seq, D_MODEL)
    logf_all = jnp.stack(logfs)
    return (y_prompt, y_sample,
            kp.reshape(depth, batch, seq, H_A, DH_A), vp.reshape(depth, batch, seq, H_A, DH_A),
            logf_all[:, :n_prompt].reshape(depth, batch, seq, H_A),
            ks.reshape(depth, dec_batch, dec_seq, H_A, DH_A), vs.reshape(depth, dec_batch, dec_seq, H_A, DH_A),
            logf_all[:, n_prompt:].reshape(depth, dec_batch, dec_seq, H_A),
            jnp.stack(gvns).reshape(depth, dec_batch, dec_seq, D_B))
```

```python
import functools

import jax
import jax.numpy as jnp
from jax import lax
from jax.experimental import pallas as pl
from jax.experimental.pallas import tpu as pltpu
from jax.experimental.pallas import tpu_sc as plsc

F32 = jnp.float32
BF16 = jnp.bfloat16

D_MODEL = 1024
D_A = 512
H_A = 8
DH_A = 64
D_B = 512
G_B = 8
C_B = 64
GMLP_CHUNK = 128
N_GROUPS = 4
EXPERTS_PER_GROUP = 8
N_EXPERTS = N_GROUPS * EXPERTS_PER_GROUP
D_EXPERT = 512
LN_EPS = 1e-5
HEAD_PAIRS = H_A // 2
HALF_MODEL = D_MODEL // 2

LANES = 128
SUBLANES = 8
VMEM_LIMIT_BYTES = 56 * 1024 * 1024
SC_CORES = 2
SC_SUBCORES = 16
SC_GATHER_ROWS = 64

TOKEN_TILE = 512
MOE_ROWS = 512
PROMPT_Q_TILE = 1024
PROMPT_K_TILE = 1024
SAMPLE_K_TILE = 2048
PROMPT_HEADS = 4
NEG_BIG = -1e30
LOG2E = 1.4426950408889634


def _cparams(*sem):
    return pltpu.CompilerParams(dimension_semantics=sem, vmem_limit_bytes=VMEM_LIMIT_BYTES)


def _split3(x):
    hi = x.astype(BF16)
    r1 = x - hi.astype(F32)
    mid = r1.astype(BF16)
    lo = (r1 - mid.astype(F32)).astype(BF16)
    return hi, mid, lo


def _gelu_tanh(x):
    return 0.5 * x * (1.0 + jnp.tanh(0.7978845608028654 * (x + 0.044715 * (x * x * x))))


def _log_sigmoid(z):
    return jnp.minimum(z, 0.0) - jnp.log(1.0 + jnp.exp(-jnp.abs(z)))


def _layer_norm(x, g, b):
    mu = jnp.mean(x, axis=-1, keepdims=True)
    xc = x - mu
    var = jnp.mean(xc * xc, axis=-1, keepdims=True)
    return xc * lax.rsqrt(var + LN_EPS) * g + b


def _rms_norm(x, g):
    return x * lax.rsqrt(jnp.mean(x * x, axis=-1, keepdims=True) + LN_EPS) * g


def _in_proj_kernel(n_prompt_tiles, tiles_per_seq,
                    x_ref, w_ref, bf_ref, lnvg_ref, lnvb_ref, wmix_ref, bmix_ref, gnb_ref, tri_ref, place_ref,
                    kp_in, vp_in, ks_in, vs_in,
                    q_ref, kp_ref, vp_ref, ks_ref, vs_ref, kaug_ref, vta_ref, logf_ref, dT_ref, dcol_ref, bn_ref,
                    gvn_ref, carry_ref):
    del kp_in, vp_in, ks_in, vs_in
    i = pl.program_id(0)
    tm = x_ref.shape[0]
    is_sample = i >= n_prompt_tiles
    var = is_sample.astype(jnp.int32)

    p = jnp.dot(x_ref[...].astype(BF16), w_ref[...], preferred_element_type=F32)
    q = p[:, 0:D_A]
    k = p[:, D_A:2 * D_A]
    v = p[:, 2 * D_A:3 * D_A]
    u = p[:, 3 * D_A:3 * D_A + D_B]
    gv = p[:, 3 * D_A + D_B:3 * D_A + 2 * D_B]
    fl = p[:, 3 * D_A + 2 * D_B:]

    q_ref[...] = q.astype(BF16)
    row_v = lax.broadcasted_iota(jnp.int32, (LANES, tm), 0)
    tail = jnp.where(row_v == DH_A, 1.0, 0.0)
    for j in range(HEAD_PAIRS):
        vt_pair = v[:, j * LANES:(j + 1) * LANES].T
        vta_ref[2 * j] = jnp.where(row_v < DH_A, vt_pair, tail).astype(BF16)
        vta_ref[2 * j + 1] = jnp.where(row_v < DH_A, pltpu.roll(vt_pair, DH_A, 0), tail).astype(BF16)

    @pl.when(jnp.logical_not(is_sample))
    def _():
        kp_ref[0] = k
        vp_ref[0] = v

    @pl.when(is_sample)
    def _():
        ks_ref[0] = k
        vs_ref[0] = v

    logf = _log_sigmoid(fl + bf_ref[...])
    logf_ref[...] = logf[:, 0:H_A]
    hi, mid, lo = _split3(logf)
    parts = jnp.concatenate([hi, mid, lo], axis=1)
    cs = jnp.dot(tri_ref[var], parts, preferred_element_type=F32)
    cs = cs[:, 0:LANES] + cs[:, LANES:2 * LANES] + cs[:, 2 * LANES:3 * LANES]

    @pl.when(jnp.logical_or(is_sample, i % tiles_per_seq == 0))
    def _():
        carry_ref[...] = jnp.zeros_like(carry_ref)

    d = cs + carry_ref[...]
    carry_ref[...] = d[tm - 1:tm, :]
    d2 = d * LOG2E
    dcol_ref[...] = d2[:, 0:H_A]
    dT_ref[...] = d2.T[0:H_A, :]

    nh, nm, nl = _split3(-d2)
    lane_t = lax.broadcasted_iota(jnp.int32, (tm, LANES), 1)
    dparts = jnp.where(lane_t < H_A, nh,
                       jnp.where(lane_t < 2 * H_A, pltpu.roll(nm, H_A, 1),
                                 jnp.where(lane_t < 3 * H_A, pltpu.roll(nl, 2 * H_A, 1),
                                           jnp.where(lane_t == 3 * H_A, 1.0, 0.0).astype(BF16))))
    aug = jnp.dot(dparts, place_ref[...], preferred_element_type=F32)
    low_t = lane_t < DH_A
    for h in range(H_A):
        kh = k[:, (h // 2) * LANES:(h // 2 + 1) * LANES]
        own = low_t if h % 2 == 0 else jnp.logical_not(low_t)
        kaug_ref[h] = jnp.where(own, kh, aug[:, h * LANES:(h + 1) * LANES]).astype(BF16)

    ug = _gelu_tanh(u)
    vn = _layer_norm(_gelu_tanh(gv), lnvg_ref[...], lnvb_ref[...])

    @pl.when(is_sample)
    def _():
        gvn_ref[...] = vn

    r_io = lax.broadcasted_iota(jnp.int32, (GMLP_CHUNK, GMLP_CHUNK), 0)
    c_io = lax.broadcasted_iota(jnp.int32, (GMLP_CHUNK, GMLP_CHUNK), 1)
    causal = r_io >= c_io
    half = GMLP_CHUNK // 2
    same_half = (r_io >= half) == (c_io >= half)
    keep = jnp.logical_and(causal, jnp.logical_or(jnp.logical_not(is_sample), same_half))
    lane = lax.broadcasted_iota(jnp.int32, (GMLP_CHUNK, LANES), 1)
    low_lanes = lane < C_B
    vnb = vn.astype(BF16)
    bias = bmix_ref[var]
    mixed_rows = []
    for r in range(tm // GMLP_CHUNK):
        rows = slice(r * GMLP_CHUNK, (r + 1) * GMLP_CHUNK)
        cols_out = []
        for j in range(G_B // 2):
            vj = vnb[rows, j * LANES:(j + 1) * LANES]
            m0 = jnp.where(keep, wmix_ref[var, 2 * j], 0.0).astype(BF16)
            m1 = jnp.where(keep, wmix_ref[var, 2 * j + 1], 0.0).astype(BF16)
            y0 = jnp.dot(m0, vj, preferred_element_type=F32)
            y1 = jnp.dot(m1, vj, preferred_element_type=F32)
            cols_out.append(jnp.where(low_lanes, y0, y1))
        mixed_rows.append(jnp.concatenate(cols_out, axis=1) + bias)
    mixed = jnp.concatenate(mixed_rows, axis=0)
    b_out = ug * mixed
    bn_ref[...] = _rms_norm(b_out, gnb_ref[...]).astype(BF16)


def _in_proj(layer, x, w, bf, lnvg, lnvb, wmix, bmix, gnb, tri, place, kp, vp, ks, vs, n_prompt, seq):
    t_rows = x.shape[0]
    tm = TOKEN_TILE
    n_tiles = t_rows // tm
    npt = n_prompt // tm
    n_sample = t_rows - n_prompt
    tiles_per_seq = seq // tm
    full = lambda a: pl.BlockSpec(a.shape, lambda i: (0,) * a.ndim)
    any_spec = pl.BlockSpec(memory_space=pl.ANY)
    p_idx = lambda i: (layer, jnp.minimum(i, npt - 1), 0)
    s_idx = lambda i: (layer, jnp.maximum(i - npt, 0), 0)
    out_shapes = (
        jax.ShapeDtypeStruct((t_rows, D_A), BF16),
        jax.ShapeDtypeStruct(kp.shape, F32),
        jax.ShapeDtypeStruct(vp.shape, F32),
        jax.ShapeDtypeStruct(ks.shape, F32),
        jax.ShapeDtypeStruct(vs.shape, F32),
        jax.ShapeDtypeStruct((H_A, t_rows, LANES), BF16),
        jax.ShapeDtypeStruct((H_A, LANES, t_rows), BF16),
        jax.ShapeDtypeStruct((t_rows, H_A), F32),
        jax.ShapeDtypeStruct((H_A, t_rows), F32),
        jax.ShapeDtypeStruct((t_rows, H_A), F32),
        jax.ShapeDtypeStruct((t_rows, D_B), BF16),
        jax.ShapeDtypeStruct((n_sample, D_B), F32),
    )
    out_specs = (
        pl.BlockSpec((tm, D_A), lambda i: (i, 0)),
        pl.BlockSpec((1, tm, D_A), p_idx),
        pl.BlockSpec((1, tm, D_A), p_idx),
        pl.BlockSpec((1, tm, D_A), s_idx),
        pl.BlockSpec((1, tm, D_A), s_idx),
        pl.BlockSpec((H_A, tm, LANES), lambda i: (0, i, 0)),
        pl.BlockSpec((H_A, LANES, tm), lambda i: (0, 0, i)),
        pl.BlockSpec((tm, H_A), lambda i: (i, 0)),
        pl.BlockSpec((H_A, tm), lambda i: (0, i)),
        pl.BlockSpec((tm, H_A), lambda i: (i, 0)),
        pl.BlockSpec((tm, D_B), lambda i: (i, 0)),
        pl.BlockSpec((tm, D_B), lambda i: (jnp.maximum(i - npt, 0), 0)),
    )
    in_specs = [pl.BlockSpec((tm, D_MODEL), lambda i: (i, 0)),
                full(w), full(bf), full(lnvg), full(lnvb), full(wmix), full(bmix), full(gnb), full(tri), full(place),
                any_spec, any_spec, any_spec, any_spec]
    return pl.pallas_call(
        functools.partial(_in_proj_kernel, npt, tiles_per_seq),
        grid=(n_tiles,),
        in_specs=in_specs,
        out_specs=out_specs,
        out_shape=out_shapes,
        scratch_shapes=[pltpu.VMEM((1, LANES), F32)],
        input_output_aliases={10: 1, 11: 2, 12: 3, 13: 4},
        compiler_params=_cparams("arbitrary"),
        name="in_proj",
    )(x, w, bf, lnvg, lnvb, wmix, bmix, gnb, tri, place, kp, vp, ks, vs)


def _prompt_tile(qa_ref, ka, vt_prev, key_off, masked, m_ref, acc_ref, p_ref, alpha_ref):
    tk, tq = ka.shape[1], qa_ref.shape[1]
    nt = (((1,), (1,)), ((), ()))
    if masked:
        rel = lax.broadcasted_iota(jnp.int32, (tk, tq), 0) - lax.broadcasted_iota(jnp.int32, (tk, tq), 1)
        keep = rel <= key_off
    _prompt_pv(vt_prev, p_ref, alpha_ref, acc_ref)
    for h in range(PROMPT_HEADS):
        s = lax.dot_general(ka[h], qa_ref[h], nt, preferred_element_type=F32)
        if masked:
            s = jnp.where(keep, s, NEG_BIG)
        m_prev = m_ref[h]
        m_new = jnp.maximum(m_prev, jnp.max(s, axis=0, keepdims=True))
        m_ref[h] = m_new
        p_ref[h] = jnp.exp2(s - m_new[0:1, :]).astype(BF16)
        alpha_ref[h] = jnp.exp2(m_prev - m_new)


def _prompt_pv(vt, p_ref, alpha_ref, acc_ref):
    for h in range(PROMPT_HEADS):
        acc_ref[h] = (alpha_ref[h, 0:1, :] * acc_ref[h]
                      + jnp.dot(vt[h], p_ref[h], preferred_element_type=F32))


def _fox_prompt_kernel(kv_per_q, qi_tab, ki_tab, q_ref, ka_ref, vt_ref, dq_ref, a_in, o_ref,
                       qa_ref, m_ref, acc_ref, p_ref, alpha_ref, vtp_ref):
    del a_in
    t = pl.program_id(2)
    qi = qi_tab[t]
    ki = ki_tab[t]
    tq = q_ref.shape[0]
    tk = ka_ref.shape[1]

    @pl.when(ki == 0)
    def _():
        m_ref[...] = jnp.full_like(m_ref, -jnp.inf)
        acc_ref[...] = jnp.zeros_like(acc_ref)
        p_ref[...] = jnp.zeros_like(p_ref)
        alpha_ref[...] = jnp.zeros_like(alpha_ref)
        vtp_ref[...] = jnp.zeros_like(vtp_ref)
        lane = lax.broadcasted_iota(jnp.int32, (tq, LANES), 1)
        for pp in range(PROMPT_HEADS // 2):
            dqc = jnp.concatenate([dq_ref[pp], jnp.zeros((LANES - SUBLANES, tq), F32)], axis=0).T
            parts = [p.astype(F32) for p in _split3(dqc)]
            q2 = q_ref[:, pp * LANES:(pp + 1) * LANES].astype(F32)
            for h in range(2):
                base = DH_A if h == 0 else 0
                own = (lane < DH_A) if h == 0 else (lane >= DH_A)
                extra = jnp.where(jnp.logical_and(lane >= base + 3, lane < base + 6), 1.0, 0.0)
                for c in range(3):
                    extra = jnp.where(lane == base + c, parts[c][:, h:h + 1], extra)
                qa_ref[2 * pp + h] = jnp.where(own, q2, extra).astype(BF16)

    key_off = qi * tq - ki * tk
    unmasked = (ki + 1) * tk <= qi * tq

    @pl.when(unmasked)
    def _():
        _prompt_tile(qa_ref, ka_ref[...], vtp_ref[...], key_off, False, m_ref, acc_ref, p_ref, alpha_ref)
        vtp_ref[...] = vt_ref[...]

    @pl.when(jnp.logical_not(unmasked))
    def _():
        _prompt_tile(qa_ref, ka_ref[...], vtp_ref[...], key_off, True, m_ref, acc_ref, p_ref, alpha_ref)
        vtp_ref[...] = vt_ref[...]

    @pl.when(ki == (qi + 1) * kv_per_q - 1)
    def _():
        _prompt_pv(vt_ref[...], p_ref, alpha_ref, acc_ref)
        for pp in range(PROMPT_HEADS // 2):
            out_t = jnp.concatenate([acc_ref[2 * pp + h, 0:DH_A, :] / acc_ref[2 * pp + h, DH_A:DH_A + 1, :]
                                     for h in range(2)], axis=0)
            o_ref[:, pp * LANES:(pp + 1) * LANES] = out_t.T.astype(o_ref.dtype)


def _fox_prompt(q, kaug, vta, dpair, batch, seq, tq, tk):
    t_rows = q.shape[0]
    nq, nk = seq // tq, seq // tk
    kv_per_q = tq // tk
    pairs = [(a, b) for a in range(nq) for b in range((a + 1) * kv_per_q)]
    qi_tab = jnp.asarray([a for a, _ in pairs], jnp.int32)
    ki_tab = jnp.asarray([b for _, b in pairs], jnp.int32)
    nh = PROMPT_HEADS
    wq = nh * DH_A
    grid_spec = pltpu.PrefetchScalarGridSpec(
        num_scalar_prefetch=2,
        grid=(batch, H_A // nh, len(pairs)),
        in_specs=[
            pl.BlockSpec((tq, wq), lambda b, g, t, qt, kt: (b * nq + qt[t], g)),
            pl.BlockSpec((nh, tk, LANES), lambda b, g, t, qt, kt: (g, b * nk + kt[t], 0)),
            pl.BlockSpec((nh, LANES, tk), lambda b, g, t, qt, kt: (g, 0, b * nk + kt[t])),
            pl.BlockSpec((nh // 2, SUBLANES, tq), lambda b, g, t, qt, kt: (g, 0, b * nq + qt[t])),
            pl.BlockSpec(memory_space=pl.ANY),
        ],
        out_specs=pl.BlockSpec((tq, wq), lambda b, g, t, qt, kt: (b * nq + qt[t], g)),
        scratch_shapes=[pltpu.VMEM((nh, tq, LANES), BF16),
                        pltpu.VMEM((nh, SUBLANES, tq), F32),
                        pltpu.VMEM((nh, LANES, tq), F32),
                        pltpu.VMEM((nh, tk, tq), BF16), pltpu.VMEM((nh, SUBLANES, tq), F32),
                        pltpu.VMEM((nh, LANES, tk), BF16)],
    )
    return pl.pallas_call(
        functools.partial(_fox_prompt_kernel, kv_per_q),
        grid_spec=grid_spec,
        out_shape=jax.ShapeDtypeStruct((t_rows, D_A), BF16),
        input_output_aliases={6: 0},
        compiler_params=_cparams("parallel", "parallel", "arbitrary"),
        name="fox_prompt",
    )(qi_tab, ki_tab, q, kaug, vta, dpair, jnp.zeros((t_rows, D_A), BF16))


def _sample_tile(q2, k2, v2, bias_a, bias_b, mask, m_ref, l_ref, acc_ref):
    tq = q2.shape[0]
    low = lax.broadcasted_iota(jnp.int32, (tq, LANES), 1) < DH_A
    zero = jnp.zeros_like(q2)
    nt = (((1,), (1,)), ((), ()))
    pvs, alphas = [], []
    for h, (qh, bias) in enumerate(((jnp.where(low, q2, zero), bias_a), (jnp.where(low, zero, q2), bias_b))):
        s = lax.dot_general(qh, k2, nt, preferred_element_type=F32) + bias
        if mask is not None:
            s = jnp.where(mask, s, NEG_BIG)
        m_prev = m_ref[h]
        m_new = jnp.maximum(m_prev, jnp.max(s, axis=1, keepdims=True))
        alpha = jnp.exp2(m_prev - m_new)
        p = jnp.exp2(s - m_new[:, 0:1])
        l_ref[h] = alpha * l_ref[h] + jnp.sum(p, axis=1, keepdims=True)
        m_ref[h] = m_new
        pvs.append(jnp.dot(p.astype(BF16), v2, preferred_element_type=F32))
        alphas.append(alpha)
    acc_ref[...] = jnp.where(low, alphas[0], alphas[1]) * acc_ref[...] + jnp.where(low, pvs[0], pvs[1])


def _fox_sample_kernel(n_cache_tiles, q_ref, ck_ref, cv_ref, r_ref, nk_ref, nv_ref, dqc_ref, dqr_ref,
                       a_in, o_ref, m_ref, l_ref, acc_ref):
    del a_in
    s_idx = pl.program_id(1)
    tq = q_ref.shape[0]

    @pl.when(s_idx == 0)
    def _():
        m_ref[...] = jnp.full_like(m_ref, -jnp.inf)
        l_ref[...] = jnp.zeros_like(l_ref)
        acc_ref[...] = jnp.zeros_like(acc_ref)

    dq = dqc_ref[...]

    def pair_refs(p):
        cols = pl.ds(p * LANES, LANES)
        return m_ref.at[p], l_ref.at[p], acc_ref.at[:, cols]

    @pl.when(s_idx < n_cache_tiles)
    def _():
        r = r_ref[0, 0]
        for p in range(HEAD_PAIRS):
            cols = slice(p * LANES, (p + 1) * LANES)
            bias_a = dq[:, 2 * p:2 * p + 1] + r[2 * p:2 * p + 1, :]
            bias_b = dq[:, 2 * p + 1:2 * p + 2] + r[2 * p + 1:2 * p + 2, :]
            _sample_tile(q_ref[:, cols], ck_ref[0, 0, :, cols].astype(BF16), cv_ref[0, 0, :, cols].astype(BF16),
                         bias_a, bias_b, None, *pair_refs(p))

    @pl.when(s_idx == n_cache_tiles)
    def _():
        dk = dqr_ref[0]
        causal = lax.broadcasted_iota(jnp.int32, (tq, tq), 0) >= lax.broadcasted_iota(jnp.int32, (tq, tq), 1)
        for p in range(HEAD_PAIRS):
            cols = slice(p * LANES, (p + 1) * LANES)
            bias_a = dq[:, 2 * p:2 * p + 1] - dk[2 * p:2 * p + 1, :]
            bias_b = dq[:, 2 * p + 1:2 * p + 2] - dk[2 * p + 1:2 * p + 2, :]
            _sample_tile(q_ref[:, cols], nk_ref[0, :, cols].astype(BF16), nv_ref[0, :, cols].astype(BF16),
                         bias_a, bias_b, causal, *pair_refs(p))
        low = lax.broadcasted_iota(jnp.int32, (tq, LANES), 1) < DH_A
        for p in range(HEAD_PAIRS):
            cols = slice(p * LANES, (p + 1) * LANES)
            o_ref[:, cols] = (acc_ref[:, cols] / jnp.where(low, l_ref[p, 0], l_ref[p, 1])).astype(o_ref.dtype)


def _fox_sample(layer, q, cache_k, cache_v, rsuf, ks, vs, dcol, drow_s, a_buf, n_prompt, tile):
    depth, dec_batch, past, _ = cache_k.shape
    dec_seq = ks.shape[1] // dec_batch
    nct = past // tile
    q0 = n_prompt // dec_seq
    ci = lambda b, s: (layer, b, jnp.minimum(s, nct - 1), 0)
    return pl.pallas_call(
        functools.partial(_fox_sample_kernel, nct),
        grid=(dec_batch, nct + 1),
        in_specs=[
            pl.BlockSpec((dec_seq, D_A), lambda b, s: (q0 + b, 0)),
            pl.BlockSpec((1, 1, tile, D_A), ci),
            pl.BlockSpec((1, 1, tile, D_A), ci),
            pl.BlockSpec((1, 1, H_A, tile), lambda b, s: (layer, b, 0, jnp.minimum(s, nct - 1))),
            pl.BlockSpec((1, dec_seq, D_A), lambda b, s: (layer, b, 0)),
            pl.BlockSpec((1, dec_seq, D_A), lambda b, s: (layer, b, 0)),
            pl.BlockSpec((dec_seq, H_A), lambda b, s: (q0 + b, 0)),
            pl.BlockSpec((1, H_A, dec_seq), lambda b, s: (b, 0, 0)),
            pl.BlockSpec(memory_space=pl.ANY),
        ],
        out_specs=pl.BlockSpec((dec_seq, D_A), lambda b, s: (q0 + b, 0)),
        out_shape=jax.ShapeDtypeStruct(a_buf.shape, a_buf.dtype),
        scratch_shapes=[pltpu.VMEM((HEAD_PAIRS, 2, dec_seq, LANES), F32),
                        pltpu.VMEM((HEAD_PAIRS, 2, dec_seq, LANES), F32),
                        pltpu.VMEM((dec_seq, D_A), F32)],
        input_output_aliases={8: 0},
        compiler_params=_cparams("parallel", "arbitrary"),
        name="fox_sample",
    )(q, cache_k, cache_v, rsuf, ks, vs, dcol, drow_s, a_buf)


def _suffix_sum_kernel(x_ref, u_ref, o_ref, carry_ref):
    j = pl.program_id(0)

    @pl.when(j == 0)
    def _():
        carry_ref[...] = jnp.zeros_like(carry_ref)

    x = x_ref[...]
    hi, mid, lo = _split3(x)
    u = u_ref[...]
    loc = (jnp.dot(hi, u, preferred_element_type=F32) + jnp.dot(mid, u, preferred_element_type=F32)
           + jnp.dot(lo, u, preferred_element_type=F32))
    o_ref[...] = (loc + carry_ref[:, 0:1]) * LOG2E
    carry_ref[...] = carry_ref[...] + jnp.sum(x, axis=1, keepdims=True)


def _suffix_sums(x):
    rows, n = x.shape
    tb = min(512, n)
    nb = n // tb
    u = (lax.broadcasted_iota(jnp.int32, (tb, tb), 0) > lax.broadcasted_iota(jnp.int32, (tb, tb), 1)).astype(BF16)
    return pl.pallas_call(
        _suffix_sum_kernel,
        grid=(nb,),
        in_specs=[pl.BlockSpec((rows, tb), lambda j: (0, nb - 1 - j)),
                  pl.BlockSpec((tb, tb), lambda j: (0, 0))],
        out_specs=pl.BlockSpec((rows, tb), lambda j: (0, nb - 1 - j)),
        out_shape=jax.ShapeDtypeStruct((rows, n), F32),
        scratch_shapes=[pltpu.VMEM((rows, LANES), F32)],
        compiler_params=_cparams("arbitrary"),
        name="cache_suffix_sums",
    )(x, u)


def _out_proj_kernel(alpha, a_ref, bn_ref, x_ref, gna_ref, wo_ref, g1_ref, b1_ref, wrh_ref, wrl_ref, br_ref, ls_ref,
                     x1_ref, x1b_ref, route_ref, counts_ref, carry_ref):
    i = pl.program_id(0)
    tm = x_ref.shape[0]

    @pl.when(i == 0)
    def _():
        carry_ref[...] = jnp.zeros_like(carry_ref)

    an = _rms_norm(a_ref[...].astype(F32), gna_ref[...]).astype(BF16)
    mix = (jnp.dot(an, wo_ref[0:D_A, :], preferred_element_type=F32)
           + jnp.dot(bn_ref[...], wo_ref[D_A:, :], preferred_element_type=F32))
    x1 = _layer_norm(alpha * x_ref[...] + mix, g1_ref[...], b1_ref[...])
    x1_ref[...] = x1
    x1h = x1.astype(BF16)
    x1b_ref[...] = _pack_halves(x1)

    x1l = (x1 - x1h.astype(F32)).astype(BF16)
    logits = (jnp.dot(x1h, wrh_ref[...], preferred_element_type=F32)
              + jnp.dot(x1l, wrh_ref[...], preferred_element_type=F32)
              + jnp.dot(x1h, wrl_ref[...], preferred_element_type=F32)) + br_ref[...]
    lane = lax.broadcasted_iota(jnp.int32, (tm, LANES), 1)
    is_group = lane < N_GROUPS
    gl = jnp.where(is_group, logits, NEG_BIG)
    gmax = jnp.max(gl, axis=1, keepdims=True)
    g_idx = jnp.min(jnp.where(gl == gmax, lane, LANES), axis=1, keepdims=True)
    g_prob = 1.0 / jnp.sum(jnp.exp(gl - gmax), axis=1, keepdims=True)
    in_group = jnp.logical_and(lane >= N_GROUPS, lane < N_GROUPS + N_EXPERTS)
    in_group = jnp.logical_and(in_group, lax.shift_right_arithmetic(lane - N_GROUPS, 3) == g_idx)
    el = jnp.where(in_group, logits, NEG_BIG)
    e1 = jnp.max(el, axis=1, keepdims=True)
    i1 = jnp.min(jnp.where(el == e1, lane, LANES), axis=1, keepdims=True)
    el2 = jnp.where(lane == i1, NEG_BIG, el)
    e2 = jnp.max(el2, axis=1, keepdims=True)
    i2 = jnp.min(jnp.where(el2 == e2, lane, LANES), axis=1, keepdims=True)
    t2 = jnp.exp(e2 - e1)
    w1 = 1.0 / (1.0 + t2)
    gate1 = g_prob * w1
    gate2 = g_prob * (t2 * w1)

    oh1 = lane == i1
    oh2 = lane == i2
    both = (oh1.astype(F32) + oh2.astype(F32)).astype(BF16)
    before = jnp.dot(ls_ref[...], both, preferred_element_type=F32) + carry_ref[...]
    rank1 = jnp.sum(jnp.where(oh1, before, 0.0), axis=1, keepdims=True)
    rank2 = jnp.sum(jnp.where(oh2, before, 0.0), axis=1, keepdims=True)
    carry_ref[...] = carry_ref[...] + jnp.sum(both.astype(F32), axis=0, keepdims=True)
    counts_ref[...] = carry_ref[...]

    route = jnp.where(lane == 0, (i1 - N_GROUPS).astype(F32), 0.0)
    route = jnp.where(lane == 1, (i2 - N_GROUPS).astype(F32), route)
    route = jnp.where(lane == 2, rank1, route)
    route = jnp.where(lane == 3, rank2, route)
    route = jnp.where(lane == 4, gate1, route)
    route = jnp.where(lane == 5, gate2, route)
    route_ref[...] = route


def _out_proj(alpha, a, bn, x, gna, wo, g1, b1, wrh, wrl, br, lstrict):
    t_rows = x.shape[0]
    tm = TOKEN_TILE
    full = lambda arr: pl.BlockSpec(arr.shape, lambda i: (0,) * arr.ndim)
    row = lambda w: pl.BlockSpec((tm, w), lambda i: (i, 0))
    return pl.pallas_call(
        functools.partial(_out_proj_kernel, alpha),
        grid=(t_rows // tm,),
        in_specs=[row(D_A), row(D_B), row(D_MODEL), full(gna), full(wo), full(g1), full(b1), full(wrh), full(wrl),
                  full(br), full(lstrict)],
        out_specs=(row(D_MODEL), row(HALF_MODEL), row(LANES), pl.BlockSpec((1, LANES), lambda i: (0, 0))),
        out_shape=(jax.ShapeDtypeStruct((t_rows, D_MODEL), F32),
                   jax.ShapeDtypeStruct((t_rows, HALF_MODEL), jnp.int32),
                   jax.ShapeDtypeStruct((t_rows, LANES), F32),
                   jax.ShapeDtypeStruct((1, LANES), F32)),
        scratch_shapes=[pltpu.VMEM((1, LANES), F32)],
        compiler_params=_cparams("arbitrary"),
        name="out_proj_router",
    )(a, bn, x, gna, wo, g1, b1, wrh, wrl, br, lstrict)


def _pack_halves(x):
    lo = lax.bitcast_convert_type(x[:, :HALF_MODEL].astype(BF16).astype(F32), jnp.uint32)
    hi = lax.bitcast_convert_type(x[:, HALF_MODEL:].astype(BF16).astype(F32), jnp.uint32)
    return lax.bitcast_convert_type((hi & jnp.uint32(0xFFFF0000)) | (lo >> 16), jnp.int32)


def _unpack_halves(w):
    u = lax.bitcast_convert_type(w, jnp.uint32)
    return (lax.bitcast_convert_type(u << 16, F32), lax.bitcast_convert_type(u & jnp.uint32(0xFFFF0000), F32))


def _expert_kernel(ib_ref, ie_ref, ni_ref, st_ref, en_ref, x_ref, w1_ref, w3_ref, w2_ref, o_ref,
                   w1b_ref, w3b_ref, w2b_ref):
    i = pl.program_id(0)
    bm = x_ref.shape[0]

    @pl.when(i < ni_ref[0])
    def _():
        e = ie_ref[i]
        b = ib_ref[i]
        prev = jnp.maximum(i - 1, 0)
        first_item = i == 0

        @pl.when(jnp.logical_or(first_item, ie_ref[prev] != e))
        def _():
            w1b_ref[...] = w1_ref[0, 0].astype(BF16)
            w3b_ref[...] = w3_ref[0, 0].astype(BF16)
            w2b_ref[...] = w2_ref[0, 0].astype(BF16)

        x_lo, x_hi = (v.astype(BF16) for v in _unpack_halves(x_ref[...]))
        h1 = (jnp.dot(x_lo, w1b_ref[:HALF_MODEL, :], preferred_element_type=F32)
              + jnp.dot(x_hi, w1b_ref[HALF_MODEL:, :], preferred_element_type=F32))
        h3 = (jnp.dot(x_lo, w3b_ref[:HALF_MODEL, :], preferred_element_type=F32)
              + jnp.dot(x_hi, w3b_ref[HALF_MODEL:, :], preferred_element_type=F32))
        h = (h1 * (1.0 / (1.0 + jnp.exp(-h1))) * h3).astype(BF16)
        y = _pack_halves(jnp.dot(h, w2b_ref[...], preferred_element_type=F32))
        first_visit = jnp.logical_or(first_item, ib_ref[prev] != b)

        @pl.when(first_visit)
        def _():
            o_ref[...] = y

        @pl.when(jnp.logical_not(first_visit))
        def _():
            row = b * bm + lax.broadcasted_iota(jnp.int32, (bm, 1), 0)
            mine = jnp.logical_and(row >= st_ref[e], row < en_ref[e])
            o_ref[...] = jnp.where(mine, y, o_ref[...])


def _experts(layer, item_block, item_expert, n_items, starts, ends, xs, w1, w3, w2):
    n_rows = xs.shape[0]
    bm = MOE_ROWS
    n_max = item_block.shape[0]
    item = lambda i, ni: jnp.minimum(i, ni[0] - 1)
    blk = lambda i, ib, ie, ni, st, en: (ib[item(i, ni)], 0)
    wsel = lambda i, ib, ie, ni, st, en: (layer, ie[item(i, ni)], 0, 0)
    grid_spec = pltpu.PrefetchScalarGridSpec(
        num_scalar_prefetch=5,
        grid=(n_max,),
        in_specs=[pl.BlockSpec((bm, HALF_MODEL), blk),
                  pl.BlockSpec((1, 1, D_MODEL, D_EXPERT), wsel),
                  pl.BlockSpec((1, 1, D_MODEL, D_EXPERT), wsel),
                  pl.BlockSpec((1, 1, D_EXPERT, D_MODEL), wsel)],
        out_specs=pl.BlockSpec((bm, HALF_MODEL), blk),
        scratch_shapes=[pltpu.VMEM((D_MODEL, D_EXPERT), BF16), pltpu.VMEM((D_MODEL, D_EXPERT), BF16),
                        pltpu.VMEM((D_EXPERT, D_MODEL), BF16)],
    )
    return pl.pallas_call(
        _expert_kernel,
        grid_spec=grid_spec,
        out_shape=jax.ShapeDtypeStruct((n_rows, HALF_MODEL), jnp.int32),
        compiler_params=_cparams("arbitrary"),
        name="experts",
    )(item_block, item_expert, n_items, starts, ends, xs, w1, w3, w2)


def _combine_kernel(alpha, x1_ref, y0_ref, y1_ref, route_ref, g2_ref, b2_ref, o_ref):
    route = route_ref[...]
    g0, g1 = route[:, 4:5], route[:, 5:6]
    y0_lo, y0_hi = _unpack_halves(y0_ref[...])
    y1_lo, y1_hi = _unpack_halves(y1_ref[...])
    z = jnp.concatenate([alpha * x1_ref[:, :HALF_MODEL] + (g0 * y0_lo + g1 * y1_lo),
                         alpha * x1_ref[:, HALF_MODEL:] + (g0 * y0_hi + g1 * y1_hi)], axis=1)
    o_ref[...] = _layer_norm(z, g2_ref[...], b2_ref[...])


def _combine(alpha, x1, y0, y1, route, g2, b2):
    t_rows = x1.shape[0]
    tm = TOKEN_TILE
    full = lambda arr: pl.BlockSpec(arr.shape, lambda i: (0,) * arr.ndim)
    row = lambda w: pl.BlockSpec((tm, w), lambda i: (i, 0))
    return pl.pallas_call(
        functools.partial(_combine_kernel, alpha),
        grid=(t_rows // tm,),
        in_specs=[row(D_MODEL), row(HALF_MODEL), row(HALF_MODEL), row(LANES), full(g2), full(b2)],
        out_specs=row(D_MODEL),
        out_shape=jax.ShapeDtypeStruct((t_rows, D_MODEL), F32),
        compiler_params=_cparams("parallel"),
        name="moe_combine",
    )(x1, y0, y1, route, g2, b2)


def _gather_rows(table, idx):
    n_rows, width = idx.shape[0], table.shape[1]
    workers = SC_CORES * SC_SUBCORES
    step = workers * SC_GATHER_ROWS
    n_pad = -(-n_rows // step) * step
    if n_pad != n_rows:
        idx = jnp.pad(idx, (0, n_pad - n_rows))
    per_worker = n_pad // workers
    n_chunks = per_worker // SC_GATHER_ROWS
    mesh = plsc.VectorSubcoreMesh(core_axis_name="c", subcore_axis_name="s")

    @functools.partial(
        pl.kernel, mesh=mesh,
        out_type=jax.ShapeDtypeStruct((n_pad, width), table.dtype),
        scratch_types=[pltpu.VMEM((SC_GATHER_ROWS,), jnp.int32),
                       pltpu.VMEM((SC_GATHER_ROWS, width), table.dtype),
                       pltpu.SemaphoreType.DMA],
        name="sc_gather_rows",
    )
    def gather(table_hbm, idx_hbm, out_hbm, idx_v, rows_v, sem):
        base = (lax.axis_index("s") * SC_CORES + lax.axis_index("c")) * per_worker

        @pl.loop(0, n_chunks)
        def _(c):
            off = base + c * SC_GATHER_ROWS
            pltpu.sync_copy(idx_hbm.at[pl.ds(off, SC_GATHER_ROWS)], idx_v)
            pltpu.async_copy(table_hbm.at[idx_v], rows_v, sem).wait()
            pltpu.sync_copy(rows_v, out_hbm.at[pl.ds(off, SC_GATHER_ROWS)])

    out = gather(table, idx)
    return out if n_pad == n_rows else out[:n_rows]


def _attention_tile(seq):
    for t in (512, 256, 128):
        if seq % t == 0:
            return t
    raise ValueError("sequence length must be a multiple of 128")


def kernel(x_prompt, x_sample, cache_k, cache_v, cache_logf, w_in, b_f, ln_v_g, ln_v_b, w_s, b_s,
           g_norm_a, g_norm_b, w_out, ln1_g, ln1_b, w_gr, b_gr, w_er, b_er, w1, w3, w2, ln2_g, ln2_b):
    batch, seq, _ = x_prompt.shape
    dec_batch, dec_seq, _ = x_sample.shape
    depth = w_in.shape[0]
    past = cache_k.shape[2]
    n_prompt = batch * seq
    n_sample = dec_batch * dec_seq
    t_rows = n_prompt + n_sample
    alpha = float((2 * depth) ** 0.25)
    tm = TOKEN_TILE
    assert seq % tm == 0 and n_sample % tm == 0 and tm % dec_seq == 0 and dec_seq == GMLP_CHUNK // 2
    assert past % 128 == 0

    x = jnp.concatenate([x_prompt.reshape(n_prompt, D_MODEL), x_sample.reshape(n_sample, D_MODEL)], axis=0)

    sp = (D_A, 2 * D_A, 3 * D_A, 3 * D_A + H_A, 3 * D_A + H_A + D_B)
    wq, wk, wv, wf, wu, wgv = (w_in[..., a:b] for a, b in zip((0,) + sp, sp + (w_in.shape[-1],)))
    wf_pad = jnp.pad(wf, ((0, 0), (0, 0), (0, LANES - H_A)))
    w_cat = jnp.concatenate([wq * (LOG2E * DH_A ** -0.5), wk, wv, wu, wgv, wf_pad], axis=-1).astype(BF16)
    bf_pad = jnp.pad(b_f, ((0, 0), (0, LANES - H_A)))[:, None, :]
    half = GMLP_CHUNK // 2
    wmix = jnp.stack([w_s, jnp.tile(w_s[:, :, :half, :half], (1, 1, 2, 2))], axis=1)
    bs_t = jnp.swapaxes(b_s, 1, 2)
    bs_var = jnp.stack([bs_t, jnp.tile(bs_t[:, :half], (1, 2, 1))], axis=1)
    bmix = jnp.repeat(bs_var, C_B, axis=-1)
    wo_b = w_out.astype(BF16)
    wr = jnp.pad(jnp.concatenate([w_gr, w_er], axis=-1), ((0, 0), (0, 0), (0, LANES - N_GROUPS - N_EXPERTS)))
    wrh = wr.astype(BF16)
    wrl = (wr - wrh.astype(F32)).astype(BF16)
    br = jnp.pad(jnp.concatenate([b_gr, b_er], axis=-1), ((0, 0), (0, LANES - N_GROUPS - N_EXPERTS)))[:, None, :]
    row2 = lambda a: a[:, None, :]

    ri = lax.broadcasted_iota(jnp.int32, (tm, tm), 0)
    ci = lax.broadcasted_iota(jnp.int32, (tm, tm), 1)
    tri = jnp.stack([ri >= ci, jnp.logical_and(ri >= ci, ri // dec_seq == ci // dec_seq)]).astype(BF16)
    lstrict = (ri > ci).astype(BF16)
    prow = lax.broadcasted_iota(jnp.int32, (LANES, H_A * LANES), 0)
    pcol = lax.broadcasted_iota(jnp.int32, (LANES, H_A * LANES), 1)
    phead = pcol // LANES
    poff = pcol % LANES - jnp.where(phead % 2 == 0, DH_A, 0)
    is_one = jnp.logical_and(prow == 3 * H_A, jnp.logical_and(poff >= 0, poff < 3))
    is_part = jnp.logical_and(jnp.logical_and(poff >= 3, poff < 6), prow == (poff - 3) * H_A + phead)
    place = jnp.logical_or(is_one, is_part).astype(BF16)

    clf = jnp.transpose(cache_logf, (0, 1, 3, 2)).reshape(depth * dec_batch * H_A, past)
    rsuf = _suffix_sums(clf).reshape(depth, dec_batch, H_A, past)
    ck = cache_k.reshape(depth, dec_batch, past, D_A)
    cv = cache_v.reshape(depth, dec_batch, past, D_A)

    kp = jnp.zeros((depth, n_prompt, D_A), F32)
    vp = jnp.zeros((depth, n_prompt, D_A), F32)
    ks = jnp.zeros((depth, n_sample, D_A), F32)
    vs = jnp.zeros((depth, n_sample, D_A), F32)
    logfs, gvns = [], []

    bm = MOE_ROWS
    n_assign = 2 * t_rows
    nblk = n_assign // bm
    expert_ids = jnp.arange(N_EXPERTS, dtype=jnp.int32)
    item_ids = jnp.arange(nblk + N_EXPERTS - 1, dtype=jnp.int32)
    q_tile = PROMPT_Q_TILE if seq % PROMPT_Q_TILE == 0 else _attention_tile(seq)
    k_tile = PROMPT_K_TILE if q_tile % PROMPT_K_TILE == 0 else q_tile
    c_tile = SAMPLE_K_TILE if past % SAMPLE_K_TILE == 0 else _attention_tile(past)
    assign_ids = jnp.arange(n_assign, dtype=jnp.int32)

    for l in range(depth):
        q, kp, vp, ks, vs, kaug, vta, logf, dT, dcol, bn, gvn = _in_proj(
            l, x, w_cat[l], bf_pad[l], row2(ln_v_g)[l], row2(ln_v_b)[l], wmix[l], bmix[l], row2(g_norm_b)[l], tri,
            place, kp, vp, ks, vs, n_prompt, seq)
        logfs.append(logf)
        gvns.append(gvn)

        dpair = jnp.pad(dT.reshape(HEAD_PAIRS, 2, t_rows), ((0, 0), (0, SUBLANES - 2), (0, 0)))
        a = _fox_prompt(q, kaug, vta, dpair, batch, seq, q_tile, k_tile)
        drow_s = jnp.transpose(dT[:, n_prompt:].reshape(H_A, dec_batch, dec_seq), (1, 0, 2))
        a = _fox_sample(l, q, ck, cv, rsuf, ks, vs, dcol, drow_s, a, n_prompt, c_tile)

        x1, x1b, route, counts = _out_proj(alpha, a, bn, x, row2(g_norm_a)[l], wo_b[l], row2(ln1_g)[l],
                                           row2(ln1_b)[l], wrh[l], wrl[l], br[l], lstrict)

        cnt = counts[0, N_GROUPS:N_GROUPS + N_EXPERTS].astype(jnp.int32)
        ends = jnp.cumsum(cnt)
        starts = ends - cnt
        eid = route[:, 0:2].astype(jnp.int32)
        rank = route[:, 2:4].astype(jnp.int32)
        onehot = eid[:, :, None] == expert_ids[None, None, :]
        pos = jnp.sum(jnp.where(onehot, starts[None, None, :], 0), axis=-1) + rank
        order = jnp.sort(eid.reshape(n_assign) * n_assign + assign_ids) % n_assign
        first_blk = starts // bm
        n_it = jnp.where(cnt > 0, (ends - 1) // bm - first_blk + 1, 0)
        it_end = jnp.cumsum(n_it)
        item_expert = jnp.minimum(jnp.sum(item_ids[:, None] >= it_end[None, :], axis=1), N_EXPERTS - 1).astype(jnp.int32)
        it_first = jnp.sum(jnp.where(item_expert[:, None] == expert_ids[None, :], (it_end - n_it)[None, :], 0), axis=1)
        it_blk0 = jnp.sum(jnp.where(item_expert[:, None] == expert_ids[None, :], first_blk[None, :], 0), axis=1)
        item_block = jnp.clip(it_blk0 + item_ids - it_first, 0, nblk - 1).astype(jnp.int32)

        xs = _gather_rows(x1b, order // 2)
        yb = _experts(l, item_block, item_expert, it_end[-1:].astype(jnp.int32), starts, ends, xs, w1, w3, w2)
        y0 = _gather_rows(yb, pos[:, 0])
        y1 = _gather_rows(yb, pos[:, 1])
        x = _combine(alpha, x1, y0, y1, route, row2(ln2_g)[l], row2(ln2_b)[l])

    y_prompt = x[:n_prompt].reshape(batch, seq, D_MODEL)
    y_sample = x[n_prompt:].reshape(dec_batch, dec_seq, D_MODEL)
    logf_all = jnp.stack(logfs)
    return (y_prompt, y_sample,
            kp.reshape(depth, batch, seq, H_A, DH_A), vp.reshape(depth, batch, seq, H_A, DH_A),
            logf_all[:, :n_prompt].reshape(depth, batch, seq, H_A),
            ks.reshape(depth, dec_batch, dec_seq, H_A, DH_A), vs.reshape(depth, dec_batch, dec_seq, H_A, DH_A),
            logf_all[:, n_prompt:].reshape(depth, dec_batch, dec_seq, H_A),
            jnp.stack(gvns).reshape(depth, dec_batch, dec_seq, D_B))
```

```python
import functools

import jax
import jax.numpy as jnp
from jax import lax
from jax.experimental import pallas as pl
from jax.experimental.pallas import tpu as pltpu
from jax.experimental.pallas import tpu_sc as plsc

F32 = jnp.float32
BF16 = jnp.bfloat16

D_MODEL = 1024
D_A = 512
H_A = 8
DH_A = 64
D_B = 512
G_B = 8
C_B = 64
GMLP_CHUNK = 128
N_GROUPS = 4
EXPERTS_PER_GROUP = 8
N_EXPERTS = N_GROUPS * EXPERTS_PER_GROUP
D_EXPERT = 512
LN_EPS = 1e-5
HEAD_PAIRS = H_A // 2
HALF_MODEL = D_MODEL // 2

LANES = 128
SUBLANES = 8
VMEM_LIMIT_BYTES = 56 * 1024 * 1024
SC_CORES = 2
SC_SUBCORES = 16
SC_GATHER_ROWS = 64

TOKEN_TILE = 512
MOE_ROWS = 512
PROMPT_Q_TILE = 1024
PROMPT_K_TILE = 1024
SAMPLE_K_TILE = 2048
PROMPT_HEADS = 4
NEG_BIG = -1e30
LOG2E = 1.4426950408889634


def _cparams(*sem):
    return pltpu.CompilerParams(dimension_semantics=sem, vmem_limit_bytes=VMEM_LIMIT_BYTES)


def _split3(x):
    hi = x.astype(BF16)
    r1 = x - hi.astype(F32)
    mid = r1.astype(BF16)
    lo = (r1 - mid.astype(F32)).astype(BF16)
    return hi, mid, lo


def _gelu_tanh(x):
    return 0.5 * x * (1.0 + jnp.tanh(0.7978845608028654 * (x + 0.044715 * (x * x * x))))


def _log_sigmoid(z):
    return jnp.minimum(z, 0.0) - jnp.log(1.0 + jnp.exp(-jnp.abs(z)))


def _layer_norm(x, g, b):
    mu = jnp.mean(x, axis=-1, keepdims=True)
    xc = x - mu
    var = jnp.mean(xc * xc, axis=-1, keepdims=True)
    return xc * lax.rsqrt(var + LN_EPS) * g + b


def _rms_norm(x, g):
    return x * lax.rsqrt(jnp.mean(x * x, axis=-1, keepdims=True) + LN_EPS) * g


def _in_proj_kernel(n_prompt_tiles, tiles_per_seq,
                    x_ref, w_ref, bf_ref, lnvg_ref, lnvb_ref, wmix_ref, bmix_ref, gnb_ref, tri_ref, place_ref,
                    kp_in, vp_in, ks_in, vs_in,
                    q_ref, kp_ref, vp_ref, ks_ref, vs_ref, kaug_ref, vta_ref, logf_ref, dT_ref, dcol_ref, bn_ref,
                    gvn_ref, carry_ref):
    del kp_in, vp_in, ks_in, vs_in
    i = pl.program_id(0)
    tm = x_ref.shape[0]
    is_sample = i >= n_prompt_tiles
    var = is_sample.astype(jnp.int32)

    p = jnp.dot(x_ref[...].astype(BF16), w_ref[...], preferred_element_type=F32)
    q = p[:, 0:D_A]
    k = p[:, D_A:2 * D_A]
    v = p[:, 2 * D_A:3 * D_A]
    u = p[:, 3 * D_A:3 * D_A + D_B]
    gv = p[:, 3 * D_A + D_B:3 * D_A + 2 * D_B]
    fl = p[:, 3 * D_A + 2 * D_B:]

    q_ref[...] = q.astype(BF16)
    row_v = lax.broadcasted_iota(jnp.int32, (LANES, tm), 0)
    tail = jnp.where(row_v == DH_A, 1.0, 0.0)
    for j in range(HEAD_PAIRS):
        vt_pair = v[:, j * LANES:(j + 1) * LANES].T
        vta_ref[2 * j] = jnp.where(row_v < DH_A, vt_pair, tail).astype(BF16)
        vta_ref[2 * j + 1] = jnp.where(row_v < DH_A, pltpu.roll(vt_pair, DH_A, 0), tail).astype(BF16)

    @pl.when(jnp.logical_not(is_sample))
    def _():
        kp_ref[0] = k
        vp_ref[0] = v

    @pl.when(is_sample)
    def _():
        ks_ref[0] = k
        vs_ref[0] = v

    logf = _log_sigmoid(fl + bf_ref[...])
    logf_ref[...] = logf[:, 0:H_A]
    hi, mid, lo = _split3(logf)
    parts = jnp.concatenate([hi, mid, lo], axis=1)
    cs = jnp.dot(tri_ref[var], parts, preferred_element_type=F32)
    cs = cs[:, 0:LANES] + cs[:, LANES:2 * LANES] + cs[:, 2 * LANES:3 * LANES]

    @pl.when(jnp.logical_or(is_sample, i % tiles_per_seq == 0))
    def _():
        carry_ref[...] = jnp.zeros_like(carry_ref)

    d = cs + carry_ref[...]
    carry_ref[...] = d[tm - 1:tm, :]
    d2 = d * LOG2E
    dcol_ref[...] = d2[:, 0:H_A]
    dT_ref[...] = d2.T[0:H_A, :]

    nh, nm, nl = _split3(-d2)
    lane_t = lax.broadcasted_iota(jnp.int32, (tm, LANES), 1)
    dparts = jnp.where(lane_t < H_A, nh,
                       jnp.where(lane_t < 2 * H_A, pltpu.roll(nm, H_A, 1),
                                 jnp.where(lane_t < 3 * H_A, pltpu.roll(nl, 2 * H_A, 1),
                                           jnp.where(lane_t == 3 * H_A, 1.0, 0.0).astype(BF16))))
    aug = jnp.dot(dparts, place_ref[...], preferred_element_type=F32)
    low_t = lane_t < DH_A
    for h in range(H_A):
        kh = k[:, (h // 2) * LANES:(h // 2 + 1) * LANES]
        own = low_t if h % 2 == 0 else jnp.logical_not(low_t)
        kaug_ref[h] = jnp.where(own, kh, aug[:, h * LANES:(h + 1) * LANES]).astype(BF16)

    ug = _gelu_tanh(u)
    vn = _layer_norm(_gelu_tanh(gv), lnvg_ref[...], lnvb_ref[...])

    @pl.when(is_sample)
    def _():
        gvn_ref[...] = vn

    r_io = lax.broadcasted_iota(jnp.int32, (GMLP_CHUNK, GMLP_CHUNK), 0)
    c_io = lax.broadcasted_iota(jnp.int32, (GMLP_CHUNK, GMLP_CHUNK), 1)
    causal = r_io >= c_io
    half = GMLP_CHUNK // 2
    same_half = (r_io >= half) == (c_io >= half)
    keep = jnp.logical_and(causal, jnp.logical_or(jnp.logical_not(is_sample), same_half))
    lane = lax.broadcasted_iota(jnp.int32, (GMLP_CHUNK, LANES), 1)
    low_lanes = lane < C_B
    vnb = vn.astype(BF16)
    bias = bmix_ref[var]
    mixed_rows = []
    for r in range(tm // GMLP_CHUNK):
        rows = slice(r * GMLP_CHUNK, (r + 1) * GMLP_CHUNK)
        cols_out = []
        for j in range(G_B // 2):
            vj = vnb[rows, j * LANES:(j + 1) * LANES]
            m0 = jnp.where(keep, wmix_ref[var, 2 * j], 0.0).astype(BF16)
            m1 = jnp.where(keep, wmix_ref[var, 2 * j + 1], 0.0).astype(BF16)
            y0 = jnp.dot(m0, vj, preferred_element_type=F32)
            y1 = jnp.dot(m1, vj, preferred_element_type=F32)
            cols_out.append(jnp.where(low_lanes, y0, y1))
        mixed_rows.append(jnp.concatenate(cols_out, axis=1) + bias)
    mixed = jnp.concatenate(mixed_rows, axis=0)
    b_out = ug * mixed
    bn_ref[...] = _rms_norm(b_out, gnb_ref[...]).astype(BF16)


def _in_proj(layer, x, w, bf, lnvg, lnvb, wmix, bmix, gnb, tri, place, kp, vp, ks, vs, n_prompt, seq):
    t_rows = x.shape[0]
    tm = TOKEN_TILE
    n_tiles = t_rows // tm
    npt = n_prompt // tm
    n_sample = t_rows - n_prompt
    tiles_per_seq = seq // tm
    full = lambda a: pl.BlockSpec(a.shape, lambda i: (0,) * a.ndim)
    any_spec = pl.BlockSpec(memory_space=pl.ANY)
    p_idx = lambda i: (layer, jnp.minimum(i, npt - 1), 0)
    s_idx = lambda i: (layer, jnp.maximum(i - npt, 0), 0)
    out_shapes = (
        jax.ShapeDtypeStruct((t_rows, D_A), BF16),
        jax.ShapeDtypeStruct(kp.shape, F32),
        jax.ShapeDtypeStruct(vp.shape, F32),
        jax.ShapeDtypeStruct(ks.shape, F32),
        jax.ShapeDtypeStruct(vs.shape, F32),
        jax.ShapeDtypeStruct((H_A, t_rows, LANES), BF16),
        jax.ShapeDtypeStruct((H_A, LANES, t_rows), BF16),
        jax.ShapeDtypeStruct((t_rows, H_A), F32),
        jax.ShapeDtypeStruct((H_A, t_rows), F32),
        jax.ShapeDtypeStruct((t_rows, H_A), F32),
        jax.ShapeDtypeStruct((t_rows, D_B), BF16),
        jax.ShapeDtypeStruct((n_sample, D_B), F32),
    )
    out_specs = (
        pl.BlockSpec((tm, D_A), lambda i: (i, 0)),
        pl.BlockSpec((1, tm, D_A), p_idx),
        pl.BlockSpec((1, tm, D_A), p_idx),
        pl.BlockSpec((1, tm, D_A), s_idx),
        pl.BlockSpec((1, tm, D_A), s_idx),
        pl.BlockSpec((H_A, tm, LANES), lambda i: (0, i, 0)),
        pl.BlockSpec((H_A, LANES, tm), lambda i: (0, 0, i)),
        pl.BlockSpec((tm, H_A), lambda i: (i, 0)),
        pl.BlockSpec((H_A, tm), lambda i: (0, i)),
        pl.BlockSpec((tm, H_A), lambda i: (i, 0)),
        pl.BlockSpec((tm, D_B), lambda i: (i, 0)),
        pl.BlockSpec((tm, D_B), lambda i: (jnp.maximum(i - npt, 0), 0)),
    )
    in_specs = [pl.BlockSpec((tm, D_MODEL), lambda i: (i, 0)),
                full(w), full(bf), full(lnvg), full(lnvb), full(wmix), full(bmix), full(gnb), full(tri), full(place),
                any_spec, any_spec, any_spec, any_spec]
    return pl.pallas_call(
        functools.partial(_in_proj_kernel, npt, tiles_per_seq),
        grid=(n_tiles,),
        in_specs=in_specs,
        out_specs=out_specs,
        out_shape=out_shapes,
        scratch_shapes=[pltpu.VMEM((1, LANES), F32)],
        input_output_aliases={10: 1, 11: 2, 12: 3, 13: 4},
        compiler_params=_cparams("arbitrary"),
        name="in_proj",
    )(x, w, bf, lnvg, lnvb, wmix, bmix, gnb, tri, place, kp, vp, ks, vs)


def _prompt_tile(qa_ref, ka, vt_prev, key_off, mode, m_ref, acc_ref, p_ref, alpha_ref):
    tk, tq = ka.shape[1], qa_ref.shape[1]
    nt = (((1,), (1,)), ((), ()))
    if mode == "masked":
        rel = lax.broadcasted_iota(jnp.int32, (tk, tq), 0) - lax.broadcasted_iota(jnp.int32, (tk, tq), 1)
        keep = rel <= key_off
    _prompt_pv(vt_prev, p_ref, alpha_ref, acc_ref)
    for h in range(PROMPT_HEADS):
        s = lax.dot_general(ka[h], qa_ref[h], nt, preferred_element_type=F32)
        if mode == "frozen":
            p_ref[h] = jnp.exp2(s - m_ref[h, 0:1, :]).astype(BF16)
            alpha_ref[h] = jnp.ones_like(alpha_ref[h])
            continue
        if mode == "masked":
            s = jnp.where(keep, s, NEG_BIG)
        m_prev = m_ref[h]
        m_new = jnp.maximum(m_prev, jnp.max(s, axis=0, keepdims=True))
        m_ref[h] = m_new
        p_ref[h] = jnp.exp2(s - m_new[0:1, :]).astype(BF16)
        alpha_ref[h] = jnp.exp2(m_prev - m_new)


def _prompt_pv(vt, p_ref, alpha_ref, acc_ref):
    for h in range(PROMPT_HEADS):
        acc_ref[h] = (alpha_ref[h, 0:1, :] * acc_ref[h]
                      + jnp.dot(vt[h], p_ref[h], preferred_element_type=F32))


def _fox_prompt_kernel(frozen, qi_tab, ki_tab, first_tab, last_tab, q_ref, ka_ref, vt_ref, dq_ref, a_in,
                       o_ref, bad_ref, qa_ref, m_ref, acc_ref, p_ref, alpha_ref, vtp_ref):
    del a_in
    t = pl.program_id(2)
    qi = qi_tab[t]
    ki = ki_tab[t]
    tq = q_ref.shape[0]
    tk = ka_ref.shape[1]

    @pl.when(jnp.logical_and(t == 0, jnp.logical_and(pl.program_id(0) == 0, pl.program_id(1) == 0)))
    def _():
        p_ref[...] = jnp.zeros_like(p_ref)

    @pl.when(first_tab[t] == 1)
    def _():
        m_ref[...] = jnp.full_like(m_ref, -jnp.inf)
        acc_ref[...] = jnp.zeros_like(acc_ref)
        alpha_ref[...] = jnp.zeros_like(alpha_ref)
        vtp_ref[...] = jnp.zeros_like(vtp_ref)
        lane = lax.broadcasted_iota(jnp.int32, (tq, LANES), 1)
        for pp in range(PROMPT_HEADS // 2):
            dqc = jnp.concatenate([dq_ref[pp], jnp.zeros((LANES - SUBLANES, tq), F32)], axis=0).T
            parts = [p.astype(F32) for p in _split3(dqc)]
            q2 = q_ref[:, pp * LANES:(pp + 1) * LANES].astype(F32)
            for h in range(2):
                base = DH_A if h == 0 else 0
                own = (lane < DH_A) if h == 0 else (lane >= DH_A)
                extra = jnp.where(jnp.logical_and(lane >= base + 3, lane < base + 6), 1.0, 0.0)
                for c in range(3):
                    extra = jnp.where(lane == base + c, parts[c][:, h:h + 1], extra)
                qa_ref[2 * pp + h] = jnp.where(own, q2, extra).astype(BF16)

    key_off = qi * tq - ki * tk
    unmasked = (ki + 1) * tk <= qi * tq

    @pl.when(unmasked)
    def _():
        _prompt_tile(qa_ref, ka_ref[...], vtp_ref[...], key_off, "frozen" if frozen else "full",
                     m_ref, acc_ref, p_ref, alpha_ref)
        vtp_ref[...] = vt_ref[...]

    @pl.when(jnp.logical_not(unmasked))
    def _():
        _prompt_tile(qa_ref, ka_ref[...], vtp_ref[...], key_off, "masked", m_ref, acc_ref, p_ref, alpha_ref)
        vtp_ref[...] = vt_ref[...]

    @pl.when(last_tab[t] == 1)
    def _():
        _prompt_pv(vt_ref[...], p_ref, alpha_ref, acc_ref)
        bad = jnp.zeros((1, tq), F32)
        for h in range(PROMPT_HEADS):
            acc = acc_ref[h]
            bad = jnp.maximum(bad, jnp.max(jnp.where(acc - acc == 0.0, 0.0, 1.0), axis=0, keepdims=True))
        bad_ref[0] = jnp.broadcast_to(jnp.max(bad, axis=1, keepdims=True), (SUBLANES, LANES))
        for pp in range(PROMPT_HEADS // 2):
            out_t = jnp.concatenate([acc_ref[2 * pp + h, 0:DH_A, :] / acc_ref[2 * pp + h, DH_A:DH_A + 1, :]
                                     for h in range(2)], axis=0)
            o_ref[:, pp * LANES:(pp + 1) * LANES] = out_t.T.astype(o_ref.dtype)


def _fox_prompt_call(frozen, q, kaug, vta, dpair, batch, seq, tq, tk):
    t_rows = q.shape[0]
    nq, nk = seq // tq, seq // tk
    kv_per_q = tq // tk
    pairs = []
    for a in range(nq):
        tiles = list(range((a + 1) * kv_per_q))
        pairs += [(a, b) for b in (reversed(tiles) if frozen else tiles)]
    n_steps = len(pairs)
    qi_tab = jnp.asarray([a for a, _ in pairs], jnp.int32)
    ki_tab = jnp.asarray([b for _, b in pairs], jnp.int32)
    first_tab = jnp.asarray([int(i == 0 or pairs[i - 1][0] != pairs[i][0]) for i in range(n_steps)], jnp.int32)
    last_tab = jnp.asarray([int(i == n_steps - 1 or pairs[i + 1][0] != pairs[i][0]) for i in range(n_steps)],
                           jnp.int32)
    nh = PROMPT_HEADS
    groups = H_A // nh
    wq = nh * DH_A
    grid_spec = pltpu.PrefetchScalarGridSpec(
        num_scalar_prefetch=4,
        grid=(batch, groups, n_steps),
        in_specs=[
            pl.BlockSpec((tq, wq), lambda b, g, t, qt, kt, ft, lt: (b * nq + qt[t], g)),
            pl.BlockSpec((nh, tk, LANES), lambda b, g, t, qt, kt, ft, lt: (g, b * nk + kt[t], 0)),
            pl.BlockSpec((nh, LANES, tk), lambda b, g, t, qt, kt, ft, lt: (g, 0, b * nk + kt[t])),
            pl.BlockSpec((nh // 2, SUBLANES, tq), lambda b, g, t, qt, kt, ft, lt: (g, 0, b * nq + qt[t])),
            pl.BlockSpec(memory_space=pl.ANY),
        ],
        out_specs=(pl.BlockSpec((tq, wq), lambda b, g, t, qt, kt, ft, lt: (b * nq + qt[t], g)),
                   pl.BlockSpec((1, SUBLANES, LANES),
                                lambda b, g, t, qt, kt, ft, lt: ((b * groups + g) * nq + qt[t], 0, 0))),
        scratch_shapes=[pltpu.VMEM((nh, tq, LANES), BF16),
                        pltpu.VMEM((nh, SUBLANES, tq), F32),
                        pltpu.VMEM((nh, LANES, tq), F32),
                        pltpu.VMEM((nh, tk, tq), BF16), pltpu.VMEM((nh, SUBLANES, tq), F32),
                        pltpu.VMEM((nh, LANES, tk), BF16)],
    )
    return pl.pallas_call(
        functools.partial(_fox_prompt_kernel, frozen),
        grid_spec=grid_spec,
        out_shape=(jax.ShapeDtypeStruct((t_rows, D_A), BF16),
                   jax.ShapeDtypeStruct((batch * groups * nq, SUBLANES, LANES), F32)),
        input_output_aliases={8: 0},
        compiler_params=_cparams("arbitrary", "arbitrary", "arbitrary"),
        name="fox_prompt_frozen" if frozen else "fox_prompt",
    )(qi_tab, ki_tab, first_tab, last_tab, q, kaug, vta, dpair, jnp.zeros((t_rows, D_A), BF16))


def _fox_prompt(q, kaug, vta, dpair, batch, seq, tq, tk):
    args = (q, kaug, vta, dpair, batch, seq, tq, tk)
    a_fast, bad = _fox_prompt_call(True, *args)
    return lax.cond(jnp.max(bad) > 0.0, lambda: _fox_prompt_call(False, *args)[0], lambda: a_fast)


def _sample_tile(q2, k2, v2, bias_a, bias_b, mask, m_ref, l_ref, acc_ref):
    tq = q2.shape[0]
    low = lax.broadcasted_iota(jnp.int32, (tq, LANES), 1) < DH_A
    zero = jnp.zeros_like(q2)
    nt = (((1,), (1,)), ((), ()))
    pvs, alphas = [], []
    for h, (qh, bias) in enumerate(((jnp.where(low, q2, zero), bias_a), (jnp.where(low, zero, q2), bias_b))):
        s = lax.dot_general(qh, k2, nt, preferred_element_type=F32) + bias
        if mask is not None:
            s = jnp.where(mask, s, NEG_BIG)
        m_prev = m_ref[h]
        m_new = jnp.maximum(m_prev, jnp.max(s, axis=1, keepdims=True))
        alpha = jnp.exp2(m_prev - m_new)
        p = jnp.exp2(s - m_new[:, 0:1])
        l_ref[h] = alpha * l_ref[h] + jnp.sum(p, axis=1, keepdims=True)
        m_ref[h] = m_new
        pvs.append(jnp.dot(p.astype(BF16), v2, preferred_element_type=F32))
        alphas.append(alpha)
    acc_ref[...] = jnp.where(low, alphas[0], alphas[1]) * acc_ref[...] + jnp.where(low, pvs[0], pvs[1])


def _fox_sample_kernel(n_cache_tiles, q_ref, ck_ref, cv_ref, r_ref, nk_ref, nv_ref, dqc_ref, dqr_ref,
                       a_in, o_ref, m_ref, l_ref, acc_ref):
    del a_in
    s_idx = pl.program_id(1)
    tq = q_ref.shape[0]

    @pl.when(s_idx == 0)
    def _():
        m_ref[...] = jnp.full_like(m_ref, -jnp.inf)
        l_ref[...] = jnp.zeros_like(l_ref)
        acc_ref[...] = jnp.zeros_like(acc_ref)

    dq = dqc_ref[...]

    def pair_refs(p):
        cols = pl.ds(p * LANES, LANES)
        return m_ref.at[p], l_ref.at[p], acc_ref.at[:, cols]

    @pl.when(s_idx < n_cache_tiles)
    def _():
        r = r_ref[0, 0]
        for p in range(HEAD_PAIRS):
            cols = slice(p * LANES, (p + 1) * LANES)
            bias_a = dq[:, 2 * p:2 * p + 1] + r[2 * p:2 * p + 1, :]
            bias_b = dq[:, 2 * p + 1:2 * p + 2] + r[2 * p + 1:2 * p + 2, :]
            _sample_tile(q_ref[:, cols], ck_ref[0, 0, :, cols].astype(BF16), cv_ref[0, 0, :, cols].astype(BF16),
                         bias_a, bias_b, None, *pair_refs(p))

    @pl.when(s_idx == n_cache_tiles)
    def _():
        dk = dqr_ref[0]
        causal = lax.broadcasted_iota(jnp.int32, (tq, tq), 0) >= lax.broadcasted_iota(jnp.int32, (tq, tq), 1)
        for p in range(HEAD_PAIRS):
            cols = slice(p * LANES, (p + 1) * LANES)
            bias_a = dq[:, 2 * p:2 * p + 1] - dk[2 * p:2 * p + 1, :]
            bias_b = dq[:, 2 * p + 1:2 * p + 2] - dk[2 * p + 1:2 * p + 2, :]
            _sample_tile(q_ref[:, cols], nk_ref[0, :, cols].astype(BF16), nv_ref[0, :, cols].astype(BF16),
                         bias_a, bias_b, causal, *pair_refs(p))
        low = lax.broadcasted_iota(jnp.int32, (tq, LANES), 1) < DH_A
        for p in range(HEAD_PAIRS):
            cols = slice(p * LANES, (p + 1) * LANES)
            o_ref[:, cols] = (acc_ref[:, cols] / jnp.where(low, l_ref[p, 0], l_ref[p, 1])).astype(o_ref.dtype)


def _fox_sample(layer, q, cache_k, cache_v, rsuf, ks, vs, dcol, drow_s, a_buf, n_prompt, tile):
    depth, dec_batch, past, _ = cache_k.shape
    dec_seq = ks.shape[1] // dec_batch
    nct = past // tile
    q0 = n_prompt // dec_seq
    ci = lambda b, s: (layer, b, jnp.minimum(s, nct - 1), 0)
    return pl.pallas_call(
        functools.partial(_fox_sample_kernel, nct),
        grid=(dec_batch, nct + 1),
        in_specs=[
            pl.BlockSpec((dec_seq, D_A), lambda b, s: (q0 + b, 0)),
            pl.BlockSpec((1, 1, tile, D_A), ci),
            pl.BlockSpec((1, 1, tile, D_A), ci),
            pl.BlockSpec((1, 1, H_A, tile), lambda b, s: (layer, b, 0, jnp.minimum(s, nct - 1))),
            pl.BlockSpec((1, dec_seq, D_A), lambda b, s: (layer, b, 0)),
            pl.BlockSpec((1, dec_seq, D_A), lambda b, s: (layer, b, 0)),
            pl.BlockSpec((dec_seq, H_A), lambda b, s: (q0 + b, 0)),
            pl.BlockSpec((1, H_A, dec_seq), lambda b, s: (b, 0, 0)),
            pl.BlockSpec(memory_space=pl.ANY),
        ],
        out_specs=pl.BlockSpec((dec_seq, D_A), lambda b, s: (q0 + b, 0)),
        out_shape=jax.ShapeDtypeStruct(a_buf.shape, a_buf.dtype),
        scratch_shapes=[pltpu.VMEM((HEAD_PAIRS, 2, dec_seq, LANES), F32),
                        pltpu.VMEM((HEAD_PAIRS, 2, dec_seq, LANES), F32),
                        pltpu.VMEM((dec_seq, D_A), F32)],
        input_output_aliases={8: 0},
        compiler_params=_cparams("parallel", "arbitrary"),
        name="fox_sample",
    )(q, cache_k, cache_v, rsuf, ks, vs, dcol, drow_s, a_buf)


def _suffix_sum_kernel(x_ref, u_ref, o_ref, carry_ref):
    j = pl.program_id(0)

    @pl.when(j == 0)
    def _():
        carry_ref[...] = jnp.zeros_like(carry_ref)

    x = x_ref[...]
    hi, mid, lo = _split3(x)
    u = u_ref[...]
    loc = (jnp.dot(hi, u, preferred_element_type=F32) + jnp.dot(mid, u, preferred_element_type=F32)
           + jnp.dot(lo, u, preferred_element_type=F32))
    o_ref[...] = (loc + carry_ref[:, 0:1]) * LOG2E
    carry_ref[...] = carry_ref[...] + jnp.sum(x, axis=1, keepdims=True)


def _suffix_sums(x):
    rows, n = x.shape
    tb = min(512, n)
    nb = n // tb
    u = (lax.broadcasted_iota(jnp.int32, (tb, tb), 0) > lax.broadcasted_iota(jnp.int32, (tb, tb), 1)).astype(BF16)
    return pl.pallas_call(
        _suffix_sum_kernel,
        grid=(nb,),
        in_specs=[pl.BlockSpec((rows, tb), lambda j: (0, nb - 1 - j)),
                  pl.BlockSpec((tb, tb), lambda j: (0, 0))],
        out_specs=pl.BlockSpec((rows, tb), lambda j: (0, nb - 1 - j)),
        out_shape=jax.ShapeDtypeStruct((rows, n), F32),
        scratch_shapes=[pltpu.VMEM((rows, LANES), F32)],
        compiler_params=_cparams("arbitrary"),
        name="cache_suffix_sums",
    )(x, u)


def _out_proj_kernel(alpha, a_ref, bn_ref, x_ref, gna_ref, wo_ref, g1_ref, b1_ref, wrh_ref, wrl_ref, br_ref, ls_ref,
                     x1_ref, x1b_ref, route_ref, counts_ref, carry_ref):
    i = pl.program_id(0)
    tm = x_ref.shape[0]

    @pl.when(i == 0)
    def _():
        carry_ref[...] = jnp.zeros_like(carry_ref)

    an = _rms_norm(a_ref[...].astype(F32), gna_ref[...]).astype(BF16)
    mix = (jnp.dot(an, wo_ref[0:D_A, :], preferred_element_type=F32)
           + jnp.dot(bn_ref[...], wo_ref[D_A:, :], preferred_element_type=F32))
    x1 = _layer_norm(alpha * x_ref[...] + mix, g1_ref[...], b1_ref[...])
    x1_ref[...] = x1
    x1h = x1.astype(BF16)
    x1b_ref[...] = _pack_halves(x1)

    x1l = (x1 - x1h.astype(F32)).astype(BF16)
    logits = (jnp.dot(x1h, wrh_ref[...], preferred_element_type=F32)
              + jnp.dot(x1l, wrh_ref[...], preferred_element_type=F32)
              + jnp.dot(x1h, wrl_ref[...], preferred_element_type=F32)) + br_ref[...]
    lane = lax.broadcasted_iota(jnp.int32, (tm, LANES), 1)
    is_group = lane < N_GROUPS
    gl = jnp.where(is_group, logits, NEG_BIG)
    gmax = jnp.max(gl, axis=1, keepdims=True)
    g_idx = jnp.min(jnp.where(gl == gmax, lane, LANES), axis=1, keepdims=True)
    g_prob = 1.0 / jnp.sum(jnp.exp(gl - gmax), axis=1, keepdims=True)
    in_group = jnp.logical_and(lane >= N_GROUPS, lane < N_GROUPS + N_EXPERTS)
    in_group = jnp.logical_and(in_group, lax.shift_right_arithmetic(lane - N_GROUPS, 3) == g_idx)
    el = jnp.where(in_group, logits, NEG_BIG)
    e1 = jnp.max(el, axis=1, keepdims=True)
    i1 = jnp.min(jnp.where(el == e1, lane, LANES), axis=1, keepdims=True)
    el2 = jnp.where(lane == i1, NEG_BIG, el)
    e2 = jnp.max(el2, axis=1, keepdims=True)
    i2 = jnp.min(jnp.where(el2 == e2, lane, LANES), axis=1, keepdims=True)
    t2 = jnp.exp(e2 - e1)
    w1 = 1.0 / (1.0 + t2)
    gate1 = g_prob * w1
    gate2 = g_prob * (t2 * w1)

    oh1 = lane == i1
    oh2 = lane == i2
    both = (oh1.astype(F32) + oh2.astype(F32)).astype(BF16)
    before = jnp.dot(ls_ref[...], both, preferred_element_type=F32) + carry_ref[...]
    rank1 = jnp.sum(jnp.where(oh1, before, 0.0), axis=1, keepdims=True)
    rank2 = jnp.sum(jnp.where(oh2, before, 0.0), axis=1, keepdims=True)
    carry_ref[...] = carry_ref[...] + jnp.sum(both.astype(F32), axis=0, keepdims=True)
    counts_ref[...] = carry_ref[...]

    route = jnp.where(lane == 0, (i1 - N_GROUPS).astype(F32), 0.0)
    route = jnp.where(lane == 1, (i2 - N_GROUPS).astype(F32), route)
    route = jnp.where(lane == 2, rank1, route)
    route = jnp.where(lane == 3, rank2, route)
    route = jnp.where(lane == 4, gate1, route)
    route = jnp.where(lane == 5, gate2, route)
    route_ref[...] = route


def _out_proj(alpha, a, bn, x, gna, wo, g1, b1, wrh, wrl, br, lstrict):
    t_rows = x.shape[0]
    tm = TOKEN_TILE
    full = lambda arr: pl.BlockSpec(arr.shape, lambda i: (0,) * arr.ndim)
    row = lambda w: pl.BlockSpec((tm, w), lambda i: (i, 0))
    return pl.pallas_call(
        functools.partial(_out_proj_kernel, alpha),
        grid=(t_rows // tm,),
        in_specs=[row(D_A), row(D_B), row(D_MODEL), full(gna), full(wo), full(g1), full(b1), full(wrh), full(wrl),
                  full(br), full(lstrict)],
        out_specs=(row(D_MODEL), row(HALF_MODEL), row(LANES), pl.BlockSpec((1, LANES), lambda i: (0, 0))),
        out_shape=(jax.ShapeDtypeStruct((t_rows, D_MODEL), F32),
                   jax.ShapeDtypeStruct((t_rows, HALF_MODEL), jnp.int32),
                   jax.ShapeDtypeStruct((t_rows, LANES), F32),
                   jax.ShapeDtypeStruct((1, LANES), F32)),
        scratch_shapes=[pltpu.VMEM((1, LANES), F32)],
        compiler_params=_cparams("arbitrary"),
        name="out_proj_router",
    )(a, bn, x, gna, wo, g1, b1, wrh, wrl, br, lstrict)


def _pack_halves(x):
    lo = lax.bitcast_convert_type(x[:, :HALF_MODEL].astype(BF16).astype(F32), jnp.uint32)
    hi = lax.bitcast_convert_type(x[:, HALF_MODEL:].astype(BF16).astype(F32), jnp.uint32)
    return lax.bitcast_convert_type((hi & jnp.uint32(0xFFFF0000)) | (lo >> 16), jnp.int32)


def _unpack_halves(w):
    u = lax.bitcast_convert_type(w, jnp.uint32)
    return (lax.bitcast_convert_type(u << 16, F32), lax.bitcast_convert_type(u & jnp.uint32(0xFFFF0000), F32))


def _expert_kernel(ib_ref, ie_ref, ni_ref, st_ref, en_ref, x_ref, w1_ref, w3_ref, w2_ref, o_ref,
                   w1b_ref, w3b_ref, w2b_ref):
    i = pl.program_id(0)
    bm = x_ref.shape[0]

    @pl.when(i < ni_ref[0])
    def _():
        e = ie_ref[i]
        b = ib_ref[i]
        prev = jnp.maximum(i - 1, 0)
        first_item = i == 0

        @pl.when(jnp.logical_or(first_item, ie_ref[prev] != e))
        def _():
            w1b_ref[...] = w1_ref[0, 0].astype(BF16)
            w3b_ref[...] = w3_ref[0, 0].astype(BF16)
            w2b_ref[...] = w2_ref[0, 0].astype(BF16)

        x_lo, x_hi = (v.astype(BF16) for v in _unpack_halves(x_ref[...]))
        h1 = (jnp.dot(x_lo, w1b_ref[:HALF_MODEL, :], preferred_element_type=F32)
              + jnp.dot(x_hi, w1b_ref[HALF_MODEL:, :], preferred_element_type=F32))
        h3 = (jnp.dot(x_lo, w3b_ref[:HALF_MODEL, :], preferred_element_type=F32)
              + jnp.dot(x_hi, w3b_ref[HALF_MODEL:, :], preferred_element_type=F32))
        h = (h1 * (1.0 / (1.0 + jnp.exp(-h1))) * h3).astype(BF16)
        y = _pack_halves(jnp.dot(h, w2b_ref[...], preferred_element_type=F32))
        first_visit = jnp.logical_or(first_item, ib_ref[prev] != b)

        @pl.when(first_visit)
        def _():
            o_ref[...] = y

        @pl.when(jnp.logical_not(first_visit))
        def _():
            row = b * bm + lax.broadcasted_iota(jnp.int32, (bm, 1), 0)
            mine = jnp.logical_and(row >= st_ref[e], row < en_ref[e])
            o_ref[...] = jnp.where(mine, y, o_ref[...])


def _experts(layer, item_block, item_expert, n_items, starts, ends, xs, w1, w3, w2):
    n_rows = xs.shape[0]
    bm = MOE_ROWS
    n_max = item_block.shape[0]
    item = lambda i, ni: jnp.minimum(i, ni[0] - 1)
    blk = lambda i, ib, ie, ni, st, en: (ib[item(i, ni)], 0)
    wsel = lambda i, ib, ie, ni, st, en: (layer, ie[item(i, ni)], 0, 0)
    grid_spec = pltpu.PrefetchScalarGridSpec(
        num_scalar_prefetch=5,
        grid=(n_max,),
        in_specs=[pl.BlockSpec((bm, HALF_MODEL), blk),
                  pl.BlockSpec((1, 1, D_MODEL, D_EXPERT), wsel),
                  pl.BlockSpec((1, 1, D_MODEL, D_EXPERT), wsel),
                  pl.BlockSpec((1, 1, D_EXPERT, D_MODEL), wsel)],
        out_specs=pl.BlockSpec((bm, HALF_MODEL), blk),
        scratch_shapes=[pltpu.VMEM((D_MODEL, D_EXPERT), BF16), pltpu.VMEM((D_MODEL, D_EXPERT), BF16),
                        pltpu.VMEM((D_EXPERT, D_MODEL), BF16)],
    )
    return pl.pallas_call(
        _expert_kernel,
        grid_spec=grid_spec,
        out_shape=jax.ShapeDtypeStruct((n_rows, HALF_MODEL), jnp.int32),
        compiler_params=_cparams("arbitrary"),
        name="experts",
    )(item_block, item_expert, n_items, starts, ends, xs, w1, w3, w2)


def _combine_kernel(alpha, x1_ref, y0_ref, y1_ref, route_ref, g2_ref, b2_ref, o_ref):
    route = route_ref[...]
    g0, g1 = route[:, 4:5], route[:, 5:6]
    y0_lo, y0_hi = _unpack_halves(y0_ref[...])
    y1_lo, y1_hi = _unpack_halves(y1_ref[...])
    z = jnp.concatenate([alpha * x1_ref[:, :HALF_MODEL] + (g0 * y0_lo + g1 * y1_lo),
                         alpha * x1_ref[:, HALF_MODEL:] + (g0 * y0_hi + g1 * y1_hi)], axis=1)
    o_ref[...] = _layer_norm(z, g2_ref[...], b2_ref[...])


def _combine(alpha, x1, y0, y1, route, g2, b2):
    t_rows = x1.shape[0]
    tm = TOKEN_TILE
    full = lambda arr: pl.BlockSpec(arr.shape, lambda i: (0,) * arr.ndim)
    row = lambda w: pl.BlockSpec((tm, w), lambda i: (i, 0))
    return pl.pallas_call(
        functools.partial(_combine_kernel, alpha),
        grid=(t_rows // tm,),
        in_specs=[row(D_MODEL), row(HALF_MODEL), row(HALF_MODEL), row(LANES), full(g2), full(b2)],
        out_specs=row(D_MODEL),
        out_shape=jax.ShapeDtypeStruct((t_rows, D_MODEL), F32),
        compiler_params=_cparams("parallel"),
        name="moe_combine",
    )(x1, y0, y1, route, g2, b2)


def _gather_rows(table, idx):
    n_rows, width = idx.shape[0], table.shape[1]
    workers = SC_CORES * SC_SUBCORES
    step = workers * SC_GATHER_ROWS
    n_pad = -(-n_rows // step) * step
    if n_pad != n_rows:
        idx = jnp.pad(idx, (0, n_pad - n_rows))
    per_worker = n_pad // workers
    n_chunks = per_worker // SC_GATHER_ROWS
    mesh = plsc.VectorSubcoreMesh(core_axis_name="c", subcore_axis_name="s")

    @functools.partial(
        pl.kernel, mesh=mesh,
        out_type=jax.ShapeDtypeStruct((n_pad, width), table.dtype),
        scratch_types=[pltpu.VMEM((SC_GATHER_ROWS,), jnp.int32),
                       pltpu.VMEM((SC_GATHER_ROWS, width), table.dtype),
                       pltpu.SemaphoreType.DMA],
        name="sc_gather_rows",
    )
    def gather(table_hbm, idx_hbm, out_hbm, idx_v, rows_v, sem):
        base = (lax.axis_index("s") * SC_CORES + lax.axis_index("c")) * per_worker

        @pl.loop(0, n_chunks)
        def _(c):
            off = base + c * SC_GATHER_ROWS
            pltpu.sync_copy(idx_hbm.at[pl.ds(off, SC_GATHER_ROWS)], idx_v)
            pltpu.async_copy(table_hbm.at[idx_v], rows_v, sem).wait()
            pltpu.sync_copy(rows_v, out_hbm.at[pl.ds(off, SC_GATHER_ROWS)])

    out = gather(table, idx)
    return out if n_pad == n_rows else out[:n_rows]


def _attention_tile(seq):
    for t in (512, 256, 128):
        if seq % t == 0:
            return t
    raise ValueError("sequence length must be a multiple of 128")


def kernel(x_prompt, x_sample, cache_k, cache_v, cache_logf, w_in, b_f, ln_v_g, ln_v_b, w_s, b_s,
           g_norm_a, g_norm_b, w_out, ln1_g, ln1_b, w_gr, b_gr, w_er, b_er, w1, w3, w2, ln2_g, ln2_b):
    batch, seq, _ = x_prompt.shape
    dec_batch, dec_seq, _ = x_sample.shape
    depth = w_in.shape[0]
    past = cache_k.shape[2]
    n_prompt = batch * seq
    n_sample = dec_batch * dec_seq
    t_rows = n_prompt + n_sample
    alpha = float((2 * depth) ** 0.25)
    tm = TOKEN_TILE
    assert seq % tm == 0 and n_sample % tm == 0 and tm % dec_seq == 0 and dec_seq == GMLP_CHUNK // 2
    assert past % 128 == 0

    x = jnp.concatenate([x_prompt.reshape(n_prompt, D_MODEL), x_sample.reshape(n_sample, D_MODEL)], axis=0)

    sp = (D_A, 2 * D_A, 3 * D_A, 3 * D_A + H_A, 3 * D_A + H_A + D_B)
    wq, wk, wv, wf, wu, wgv = (w_in[..., a:b] for a, b in zip((0,) + sp, sp + (w_in.shape[-1],)))
    wf_pad = jnp.pad(wf, ((0, 0), (0, 0), (0, LANES - H_A)))
    w_cat = jnp.concatenate([wq * (LOG2E * DH_A ** -0.5), wk, wv, wu, wgv, wf_pad], axis=-1).astype(BF16)
    bf_pad = jnp.pad(b_f, ((0, 0), (0, LANES - H_A)))[:, None, :]
    half = GMLP_CHUNK // 2
    wmix = jnp.stack([w_s, jnp.tile(w_s[:, :, :half, :half], (1, 1, 2, 2))], axis=1)
    bs_t = jnp.swapaxes(b_s, 1, 2)
    bs_var = jnp.stack([bs_t, jnp.tile(bs_t[:, :half], (1, 2, 1))], axis=1)
    bmix = jnp.repeat(bs_var, C_B, axis=-1)
    wo_b = w_out.astype(BF16)
    wr = jnp.pad(jnp.concatenate([w_gr, w_er], axis=-1), ((0, 0), (0, 0), (0, LANES - N_GROUPS - N_EXPERTS)))
    wrh = wr.astype(BF16)
    wrl = (wr - wrh.astype(F32)).astype(BF16)
    br = jnp.pad(jnp.concatenate([b_gr, b_er], axis=-1), ((0, 0), (0, LANES - N_GROUPS - N_EXPERTS)))[:, None, :]
    row2 = lambda a: a[:, None, :]

    ri = lax.broadcasted_iota(jnp.int32, (tm, tm), 0)
    ci = lax.broadcasted_iota(jnp.int32, (tm, tm), 1)
    tri = jnp.stack([ri >= ci, jnp.logical_and(ri >= ci, ri // dec_seq == ci // dec_seq)]).astype(BF16)
    lstrict = (ri > ci).astype(BF16)
    prow = lax.broadcasted_iota(jnp.int32, (LANES, H_A * LANES), 0)
    pcol = lax.broadcasted_iota(jnp.int32, (LANES, H_A * LANES), 1)
    phead = pcol // LANES
    poff = pcol % LANES - jnp.where(phead % 2 == 0, DH_A, 0)
    is_one = jnp.logical_and(prow == 3 * H_A, jnp.logical_and(poff >= 0, poff < 3))
    is_part = jnp.logical_and(jnp.logical_and(poff >= 3, poff < 6), prow == (poff - 3) * H_A + phead)
    place = jnp.logical_or(is_one, is_part).astype(BF16)

    clf = jnp.transpose(cache_logf, (0, 1, 3, 2)).reshape(depth * dec_batch * H_A, past)
    rsuf = _suffix_sums(clf).reshape(depth, dec_batch, H_A, past)
    ck = cache_k.reshape(depth, dec_batch, past, D_A)
    cv = cache_v.reshape(depth, dec_batch, past, D_A)

    kp = jnp.zeros((depth, n_prompt, D_A), F32)
    vp = jnp.zeros((depth, n_prompt, D_A), F32)
    ks = jnp.zeros((depth, n_sample, D_A), F32)
    vs = jnp.zeros((depth, n_sample, D_A), F32)
    logfs, gvns = [], []

    bm = MOE_ROWS
    n_assign = 2 * t_rows
    nblk = n_assign // bm
    expert_ids = jnp.arange(N_EXPERTS, dtype=jnp.int32)
    item_ids = jnp.arange(nblk + N_EXPERTS - 1, dtype=jnp.int32)
    q_tile = PROMPT_Q_TILE if seq % PROMPT_Q_TILE == 0 else _attention_tile(seq)
    k_tile = PROMPT_K_TILE if q_tile % PROMPT_K_TILE == 0 else q_tile
    c_tile = SAMPLE_K_TILE if past % SAMPLE_K_TILE == 0 else _attention_tile(past)
    assign_ids = jnp.arange(n_assign, dtype=jnp.int32)

    for l in range(depth):
        q, kp, vp, ks, vs, kaug, vta, logf, dT, dcol, bn, gvn = _in_proj(
            l, x, w_cat[l], bf_pad[l], row2(ln_v_g)[l], row2(ln_v_b)[l], wmix[l], bmix[l], row2(g_norm_b)[l], tri,
            place, kp, vp, ks, vs, n_prompt, seq)
        logfs.append(logf)
        gvns.append(gvn)

        dpair = jnp.pad(dT.reshape(HEAD_PAIRS, 2, t_rows), ((0, 0), (0, SUBLANES - 2), (0, 0)))
        a = _fox_prompt(q, kaug, vta, dpair, batch, seq, q_tile, k_tile)
        drow_s = jnp.transpose(dT[:, n_prompt:].reshape(H_A, dec_batch, dec_seq), (1, 0, 2))
        a = _fox_sample(l, q, ck, cv, rsuf, ks, vs, dcol, drow_s, a, n_prompt, c_tile)

        x1, x1b, route, counts = _out_proj(alpha, a, bn, x, row2(g_norm_a)[l], wo_b[l], row2(ln1_g)[l],
                                           row2(ln1_b)[l], wrh[l], wrl[l], br[l], lstrict)

        cnt = counts[0, N_GROUPS:N_GROUPS + N_EXPERTS].astype(jnp.int32)
        ends = jnp.cumsum(cnt)
        starts = ends - cnt
        eid = route[:, 0:2].astype(jnp.int32)
        rank = route[:, 2:4].astype(jnp.int32)
        onehot = eid[:, :, None] == expert_ids[None, None, :]
        pos = jnp.sum(jnp.where(onehot, starts[None, None, :], 0), axis=-1) + rank
        order = jnp.sort(eid.reshape(n_assign) * n_assign + assign_ids) % n_assign
        first_blk = starts // bm
        n_it = jnp.where(cnt > 0, (ends - 1) // bm - first_blk + 1, 0)
        it_end = jnp.cumsum(n_it)
        item_expert = jnp.minimum(jnp.sum(item_ids[:, None] >= it_end[None, :], axis=1), N_EXPERTS - 1).astype(jnp.int32)
        it_first = jnp.sum(jnp.where(item_expert[:, None] == expert_ids[None, :], (it_end - n_it)[None, :], 0), axis=1)
        it_blk0 = jnp.sum(jnp.where(item_expert[:, None] == expert_ids[None, :], first_blk[None, :], 0), axis=1)
        item_block = jnp.clip(it_blk0 + item_ids - it_first, 0, nblk - 1).astype(jnp.int32)

        xs = _gather_rows(x1b, order // 2)
        yb = _experts(l, item_block, item_expert, it_end[-1:].astype(jnp.int32), starts, ends, xs, w1, w3, w2)
        y0 = _gather_rows(yb, pos[:, 0])
        y1 = _gather_rows(yb, pos[:, 1])
        x = _combine(alpha, x1, y0, y1, route, row2(ln2_g)[l], row2(ln2_b)[l])

    y_prompt = x[:n_prompt].reshape(batch, seq, D_MODEL)
    y_sample = x[n_prompt:].reshape(dec_batch, dec_seq, D_MODEL)
    logf_all = jnp.stack(logfs)
    return (y_prompt, y_sample,
            kp.reshape(depth, batch, seq, H_A, DH_A), vp.reshape(depth, batch, seq, H_A, DH_A),
            logf_all[:, :n_prompt].reshape(depth, batch, seq, H_A),
            ks.reshape(depth, dec_batch, dec_seq, H_A, DH_A), vs.reshape(depth, dec_batch, dec_seq, H_A, DH_A),
            logf_all[:, n_prompt:].reshape(depth, dec_batch, dec_seq, H_A),
            jnp.stack(gvns).reshape(depth, dec_batch, dec_seq, D_B))
```

```python
import functools

import jax
import jax.numpy as jnp
from jax import lax
from jax.experimental import pallas as pl
from jax.experimental.pallas import tpu as pltpu
from jax.experimental.pallas import tpu_sc as plsc

F32 = jnp.float32
BF16 = jnp.bfloat16

D_MODEL = 1024
D_A = 512
H_A = 8
DH_A = 64
D_B = 512
G_B = 8
C_B = 64
GMLP_CHUNK = 128
N_GROUPS = 4
EXPERTS_PER_GROUP = 8
N_EXPERTS = N_GROUPS * EXPERTS_PER_GROUP
D_EXPERT = 512
LN_EPS = 1e-5
HEAD_PAIRS = H_A // 2
HALF_MODEL = D_MODEL // 2

LANES = 128
SUBLANES = 8
VMEM_LIMIT_BYTES = 56 * 1024 * 1024
SC_CORES = 2
SC_SUBCORES = 16
SC_GATHER_ROWS = 128

TOKEN_TILE = 512
MOE_ROWS = 512
PROMPT_Q_TILE = 1024
PROMPT_K_TILE = 1024
SAMPLE_K_TILE = 2048
PROMPT_HEADS = 4
NEG_BIG = -1e30
LOG2E = 1.4426950408889634


def _cparams(*sem):
    return pltpu.CompilerParams(dimension_semantics=sem, vmem_limit_bytes=VMEM_LIMIT_BYTES)


def _split3(x):
    hi = x.astype(BF16)
    r1 = x - hi.astype(F32)
    mid = r1.astype(BF16)
    lo = (r1 - mid.astype(F32)).astype(BF16)
    return hi, mid, lo


def _gelu_tanh(x):
    return 0.5 * x * (1.0 + jnp.tanh(0.7978845608028654 * (x + 0.044715 * (x * x * x))))


def _log_sigmoid(z):
    return jnp.minimum(z, 0.0) - jnp.log(1.0 + jnp.exp(-jnp.abs(z)))


def _layer_norm(x, g, b):
    mu = jnp.mean(x, axis=-1, keepdims=True)
    xc = x - mu
    var = jnp.mean(xc * xc, axis=-1, keepdims=True)
    return xc * lax.rsqrt(var + LN_EPS) * g + b


def _rms_norm(x, g):
    return x * lax.rsqrt(jnp.mean(x * x, axis=-1, keepdims=True) + LN_EPS) * g


def _in_proj_kernel(n_prompt_tiles, tiles_per_seq,
                    x_ref, w_ref, bf_ref, lnvg_ref, lnvb_ref, wmix_ref, bmix_ref, gnb_ref, tri_ref, place_ref,
                    kp_in, vp_in, ks_in, vs_in,
                    q_ref, kp_ref, vp_ref, ks_ref, vs_ref, kaug_ref, vta_ref, logf_ref, dT_ref, dcol_ref, bn_ref,
                    gvn_ref, carry_ref):
    del kp_in, vp_in, ks_in, vs_in
    i = pl.program_id(0)
    tm = x_ref.shape[0]
    is_sample = i >= n_prompt_tiles
    var = is_sample.astype(jnp.int32)

    p = jnp.dot(x_ref[...].astype(BF16), w_ref[...], preferred_element_type=F32)
    q = p[:, 0:D_A]
    k = p[:, D_A:2 * D_A]
    v = p[:, 2 * D_A:3 * D_A]
    u = p[:, 3 * D_A:3 * D_A + D_B]
    gv = p[:, 3 * D_A + D_B:3 * D_A + 2 * D_B]
    fl = p[:, 3 * D_A + 2 * D_B:]

    q_ref[...] = q.astype(BF16)
    row_v = lax.broadcasted_iota(jnp.int32, (LANES, tm), 0)
    tail = jnp.where(row_v == DH_A, 1.0, 0.0)
    for j in range(HEAD_PAIRS):
        vt_pair = v[:, j * LANES:(j + 1) * LANES].T
        vta_ref[2 * j] = jnp.where(row_v < DH_A, vt_pair, tail).astype(BF16)
        vta_ref[2 * j + 1] = jnp.where(row_v < DH_A, pltpu.roll(vt_pair, DH_A, 0), tail).astype(BF16)

    @pl.when(jnp.logical_not(is_sample))
    def _():
        kp_ref[0] = k
        vp_ref[0] = v

    @pl.when(is_sample)
    def _():
        ks_ref[0] = k
        vs_ref[0] = v

    logf = _log_sigmoid(fl + bf_ref[...])
    logf_ref[...] = logf[:, 0:H_A]
    hi, mid, lo = _split3(logf)
    parts = jnp.concatenate([hi, mid, lo], axis=1)
    cs = jnp.dot(tri_ref[var], parts, preferred_element_type=F32)
    cs = cs[:, 0:LANES] + cs[:, LANES:2 * LANES] + cs[:, 2 * LANES:3 * LANES]

    @pl.when(jnp.logical_or(is_sample, i % tiles_per_seq == 0))
    def _():
        carry_ref[...] = jnp.zeros_like(carry_ref)

    d = cs + carry_ref[...]
    carry_ref[...] = d[tm - 1:tm, :]
    d2 = d * LOG2E
    dcol_ref[...] = d2[:, 0:H_A]
    dT_ref[...] = d2.T[0:H_A, :]

    nh, nm, nl = _split3(-d2)
    lane_t = lax.broadcasted_iota(jnp.int32, (tm, LANES), 1)
    dparts = jnp.where(lane_t < H_A, nh,
                       jnp.where(lane_t < 2 * H_A, pltpu.roll(nm, H_A, 1),
                                 jnp.where(lane_t < 3 * H_A, pltpu.roll(nl, 2 * H_A, 1),
                                           jnp.where(lane_t == 3 * H_A, 1.0, 0.0).astype(BF16))))
    aug = jnp.dot(dparts, place_ref[...], preferred_element_type=F32)
    low_t = lane_t < DH_A
    for h in range(H_A):
        kh = k[:, (h // 2) * LANES:(h // 2 + 1) * LANES]
        own = low_t if h % 2 == 0 else jnp.logical_not(low_t)
        kaug_ref[h] = jnp.where(own, kh, aug[:, h * LANES:(h + 1) * LANES]).astype(BF16)

    ug = _gelu_tanh(u)
    vn = _layer_norm(_gelu_tanh(gv), lnvg_ref[...], lnvb_ref[...])

    @pl.when(is_sample)
    def _():
        gvn_ref[...] = vn

    r_io = lax.broadcasted_iota(jnp.int32, (GMLP_CHUNK, GMLP_CHUNK), 0)
    c_io = lax.broadcasted_iota(jnp.int32, (GMLP_CHUNK, GMLP_CHUNK), 1)
    causal = r_io >= c_io
    half = GMLP_CHUNK // 2
    same_half = (r_io >= half) == (c_io >= half)
    keep = jnp.logical_and(causal, jnp.logical_or(jnp.logical_not(is_sample), same_half))
    lane = lax.broadcasted_iota(jnp.int32, (GMLP_CHUNK, LANES), 1)
    low_lanes = lane < C_B
    vnb = vn.astype(BF16)
    bias = bmix_ref[var]
    mixed_rows = []
    for r in range(tm // GMLP_CHUNK):
        rows = slice(r * GMLP_CHUNK, (r + 1) * GMLP_CHUNK)
        cols_out = []
        for j in range(G_B // 2):
            vj = vnb[rows, j * LANES:(j + 1) * LANES]
            m0 = jnp.where(keep, wmix_ref[var, 2 * j], 0.0).astype(BF16)
            m1 = jnp.where(keep, wmix_ref[var, 2 * j + 1], 0.0).astype(BF16)
            y0 = jnp.dot(m0, vj, preferred_element_type=F32)
            y1 = jnp.dot(m1, vj, preferred_element_type=F32)
            cols_out.append(jnp.where(low_lanes, y0, y1))
        mixed_rows.append(jnp.concatenate(cols_out, axis=1) + bias)
    mixed = jnp.concatenate(mixed_rows, axis=0)
    b_out = ug * mixed
    bn_ref[...] = _rms_norm(b_out, gnb_ref[...]).astype(BF16)


def _in_proj(layer, x, w, bf, lnvg, lnvb, wmix, bmix, gnb, tri, place, kp, vp, ks, vs, n_prompt, seq):
    t_rows = x.shape[0]
    tm = TOKEN_TILE
    n_tiles = t_rows // tm
    npt = n_prompt // tm
    n_sample = t_rows - n_prompt
    tiles_per_seq = seq // tm
    full = lambda a: pl.BlockSpec(a.shape, lambda i: (0,) * a.ndim)
    any_spec = pl.BlockSpec(memory_space=pl.ANY)
    p_idx = lambda i: (layer, jnp.minimum(i, npt - 1), 0)
    s_idx = lambda i: (layer, jnp.maximum(i - npt, 0), 0)
    out_shapes = (
        jax.ShapeDtypeStruct((t_rows, D_A), BF16),
        jax.ShapeDtypeStruct(kp.shape, F32),
        jax.ShapeDtypeStruct(vp.shape, F32),
        jax.ShapeDtypeStruct(ks.shape, F32),
        jax.ShapeDtypeStruct(vs.shape, F32),
        jax.ShapeDtypeStruct((H_A, t_rows, LANES), BF16),
        jax.ShapeDtypeStruct((H_A, LANES, t_rows), BF16),
        jax.ShapeDtypeStruct((t_rows, H_A), F32),
        jax.ShapeDtypeStruct((H_A, t_rows), F32),
        jax.ShapeDtypeStruct((t_rows, H_A), F32),
        jax.ShapeDtypeStruct((t_rows, D_B), BF16),
        jax.ShapeDtypeStruct((n_sample, D_B), F32),
    )
    out_specs = (
        pl.BlockSpec((tm, D_A), lambda i: (i, 0)),
        pl.BlockSpec((1, tm, D_A), p_idx),
        pl.BlockSpec((1, tm, D_A), p_idx),
        pl.BlockSpec((1, tm, D_A), s_idx),
        pl.BlockSpec((1, tm, D_A), s_idx),
        pl.BlockSpec((H_A, tm, LANES), lambda i: (0, i, 0)),
        pl.BlockSpec((H_A, LANES, tm), lambda i: (0, 0, i)),
        pl.BlockSpec((tm, H_A), lambda i: (i, 0)),
        pl.BlockSpec((H_A, tm), lambda i: (0, i)),
        pl.BlockSpec((tm, H_A), lambda i: (i, 0)),
        pl.BlockSpec((tm, D_B), lambda i: (i, 0)),
        pl.BlockSpec((tm, D_B), lambda i: (jnp.maximum(i - npt, 0), 0)),
    )
    in_specs = [pl.BlockSpec((tm, D_MODEL), lambda i: (i, 0)),
                full(w), full(bf), full(lnvg), full(lnvb), full(wmix), full(bmix), full(gnb), full(tri), full(place),
                any_spec, any_spec, any_spec, any_spec]
    return pl.pallas_call(
        functools.partial(_in_proj_kernel, npt, tiles_per_seq),
        grid=(n_tiles,),
        in_specs=in_specs,
        out_specs=out_specs,
        out_shape=out_shapes,
        scratch_shapes=[pltpu.VMEM((1, LANES), F32)],
        input_output_aliases={10: 1, 11: 2, 12: 3, 13: 4},
        compiler_params=_cparams("arbitrary"),
        name="in_proj",
    )(x, w, bf, lnvg, lnvb, wmix, bmix, gnb, tri, place, kp, vp, ks, vs)


def _prompt_tile(qa_ref, ka, vt_prev, key_off, mode, m_ref, acc_ref, p_ref, alpha_ref):
    tk, tq = ka.shape[1], qa_ref.shape[1]
    nt = (((1,), (1,)), ((), ()))
    if mode == "masked":
        rel = lax.broadcasted_iota(jnp.int32, (tk, tq), 0) - lax.broadcasted_iota(jnp.int32, (tk, tq), 1)
        keep = rel <= key_off
    _prompt_pv(vt_prev, p_ref, alpha_ref, acc_ref)
    for h in range(PROMPT_HEADS):
        s = lax.dot_general(ka[h], qa_ref[h], nt, preferred_element_type=F32)
        if mode == "frozen":
            p_ref[h] = jnp.exp2(s - m_ref[h, 0:1, :]).astype(BF16)
            alpha_ref[h] = jnp.ones_like(alpha_ref[h])
            continue
        if mode == "masked":
            s = jnp.where(keep, s, NEG_BIG)
        m_prev = m_ref[h]
        m_new = jnp.maximum(m_prev, jnp.max(s, axis=0, keepdims=True))
        m_ref[h] = m_new
        p_ref[h] = jnp.exp2(s - m_new[0:1, :]).astype(BF16)
        alpha_ref[h] = jnp.exp2(m_prev - m_new)


def _prompt_pv(vt, p_ref, alpha_ref, acc_ref):
    for h in range(PROMPT_HEADS):
        acc_ref[h] = (alpha_ref[h, 0:1, :] * acc_ref[h]
                      + jnp.dot(vt[h], p_ref[h], preferred_element_type=F32))


def _fox_prompt_kernel(frozen, qi_tab, ki_tab, first_tab, last_tab, q_ref, ka_ref, vt_ref, dq_ref, a_in,
                       o_ref, bad_ref, qa_ref, m_ref, acc_ref, p_ref, alpha_ref, vtp_ref):
    del a_in
    t = pl.program_id(2)
    qi = qi_tab[t]
    ki = ki_tab[t]
    tq = q_ref.shape[0]
    tk = ka_ref.shape[1]

    @pl.when(jnp.logical_and(t == 0, jnp.logical_and(pl.program_id(0) == 0, pl.program_id(1) == 0)))
    def _():
        p_ref[...] = jnp.zeros_like(p_ref)

    @pl.when(first_tab[t] == 1)
    def _():
        m_ref[...] = jnp.full_like(m_ref, -jnp.inf)
        acc_ref[...] = jnp.zeros_like(acc_ref)
        alpha_ref[...] = jnp.zeros_like(alpha_ref)
        vtp_ref[...] = jnp.zeros_like(vtp_ref)
        lane = lax.broadcasted_iota(jnp.int32, (tq, LANES), 1)
        for pp in range(PROMPT_HEADS // 2):
            dqc = jnp.concatenate([dq_ref[pp], jnp.zeros((LANES - SUBLANES, tq), F32)], axis=0).T
            parts = [p.astype(F32) for p in _split3(dqc)]
            q2 = q_ref[:, pp * LANES:(pp + 1) * LANES].astype(F32)
            for h in range(2):
                base = DH_A if h == 0 else 0
                own = (lane < DH_A) if h == 0 else (lane >= DH_A)
                extra = jnp.where(jnp.logical_and(lane >= base + 3, lane < base + 6), 1.0, 0.0)
                for c in range(3):
                    extra = jnp.where(lane == base + c, parts[c][:, h:h + 1], extra)
                qa_ref[2 * pp + h] = jnp.where(own, q2, extra).astype(BF16)

    key_off = qi * tq - ki * tk
    unmasked = (ki + 1) * tk <= qi * tq

    @pl.when(unmasked)
    def _():
        _prompt_tile(qa_ref, ka_ref[...], vtp_ref[...], key_off, "frozen" if frozen else "full",
                     m_ref, acc_ref, p_ref, alpha_ref)
        vtp_ref[...] = vt_ref[...]

    @pl.when(jnp.logical_not(unmasked))
    def _():
        _prompt_tile(qa_ref, ka_ref[...], vtp_ref[...], key_off, "masked", m_ref, acc_ref, p_ref, alpha_ref)
        vtp_ref[...] = vt_ref[...]

    @pl.when(last_tab[t] == 1)
    def _():
        _prompt_pv(vt_ref[...], p_ref, alpha_ref, acc_ref)
        bad = jnp.zeros((1, tq), F32)
        for h in range(PROMPT_HEADS):
            acc = acc_ref[h]
            bad = jnp.maximum(bad, jnp.max(jnp.where(acc - acc == 0.0, 0.0, 1.0), axis=0, keepdims=True))
        bad_ref[0] = jnp.broadcast_to(jnp.max(bad, axis=1, keepdims=True), (SUBLANES, LANES))
        for pp in range(PROMPT_HEADS // 2):
            out_t = jnp.concatenate([acc_ref[2 * pp + h, 0:DH_A, :] / acc_ref[2 * pp + h, DH_A:DH_A + 1, :]
                                     for h in range(2)], axis=0)
            o_ref[:, pp * LANES:(pp + 1) * LANES] = out_t.T.astype(o_ref.dtype)


def _fox_prompt_call(frozen, q, kaug, vta, dpair, batch, seq, tq, tk):
    t_rows = q.shape[0]
    nq, nk = seq // tq, seq // tk
    kv_per_q = tq // tk
    pairs = []
    for a in range(nq):
        tiles = list(range((a + 1) * kv_per_q))
        pairs += [(a, b) for b in (reversed(tiles) if frozen else tiles)]
    n_steps = len(pairs)
    qi_tab = jnp.asarray([a for a, _ in pairs], jnp.int32)
    ki_tab = jnp.asarray([b for _, b in pairs], jnp.int32)
    first_tab = jnp.asarray([int(i == 0 or pairs[i - 1][0] != pairs[i][0]) for i in range(n_steps)], jnp.int32)
    last_tab = jnp.asarray([int(i == n_steps - 1 or pairs[i + 1][0] != pairs[i][0]) for i in range(n_steps)],
                           jnp.int32)
    nh = PROMPT_HEADS
    groups = H_A // nh
    wq = nh * DH_A
    grid_spec = pltpu.PrefetchScalarGridSpec(
        num_scalar_prefetch=4,
        grid=(batch, groups, n_steps),
        in_specs=[
            pl.BlockSpec((tq, wq), lambda b, g, t, qt, kt, ft, lt: (b * nq + qt[t], g)),
            pl.BlockSpec((nh, tk, LANES), lambda b, g, t, qt, kt, ft, lt: (g, b * nk + kt[t], 0)),
            pl.BlockSpec((nh, LANES, tk), lambda b, g, t, qt, kt, ft, lt: (g, 0, b * nk + kt[t])),
            pl.BlockSpec((nh // 2, SUBLANES, tq), lambda b, g, t, qt, kt, ft, lt: (g, 0, b * nq + qt[t])),
            pl.BlockSpec(memory_space=pl.ANY),
        ],
        out_specs=(pl.BlockSpec((tq, wq), lambda b, g, t, qt, kt, ft, lt: (b * nq + qt[t], g)),
                   pl.BlockSpec((1, SUBLANES, LANES),
                                lambda b, g, t, qt, kt, ft, lt: ((b * groups + g) * nq + qt[t], 0, 0))),
        scratch_shapes=[pltpu.VMEM((nh, tq, LANES), BF16),
                        pltpu.VMEM((nh, SUBLANES, tq), F32),
                        pltpu.VMEM((nh, LANES, tq), F32),
                        pltpu.VMEM((nh, tk, tq), BF16), pltpu.VMEM((nh, SUBLANES, tq), F32),
                        pltpu.VMEM((nh, LANES, tk), BF16)],
    )
    return pl.pallas_call(
        functools.partial(_fox_prompt_kernel, frozen),
        grid_spec=grid_spec,
        out_shape=(jax.ShapeDtypeStruct((t_rows, D_A), BF16),
                   jax.ShapeDtypeStruct((batch * groups * nq, SUBLANES, LANES), F32)),
        input_output_aliases={8: 0},
        compiler_params=_cparams("arbitrary", "arbitrary", "arbitrary"),
        name="fox_prompt_frozen" if frozen else "fox_prompt",
    )(qi_tab, ki_tab, first_tab, last_tab, q, kaug, vta, dpair, jnp.zeros((t_rows, D_A), BF16))


def _fox_prompt(q, kaug, vta, dpair, batch, seq, tq, tk):
    args = (q, kaug, vta, dpair, batch, seq, tq, tk)
    a_fast, bad = _fox_prompt_call(True, *args)
    return lax.cond(jnp.max(bad) > 0.0, lambda: _fox_prompt_call(False, *args)[0], lambda: a_fast)


def _sample_tile(q2, k2, v2, bias_a, bias_b, mask, m_ref, l_ref, acc_ref):
    tq = q2.shape[0]
    low = lax.broadcasted_iota(jnp.int32, (tq, LANES), 1) < DH_A
    zero = jnp.zeros_like(q2)
    nt = (((1,), (1,)), ((), ()))
    pvs, alphas = [], []
    for h, (qh, bias) in enumerate(((jnp.where(low, q2, zero), bias_a), (jnp.where(low, zero, q2), bias_b))):
        s = lax.dot_general(qh, k2, nt, preferred_element_type=F32) + bias
        if mask is not None:
            s = jnp.where(mask, s, NEG_BIG)
        m_prev = m_ref[h]
        m_new = jnp.maximum(m_prev, jnp.max(s, axis=1, keepdims=True))
        alpha = jnp.exp2(m_prev - m_new)
        p = jnp.exp2(s - m_new[:, 0:1])
        l_ref[h] = alpha * l_ref[h] + jnp.sum(p, axis=1, keepdims=True)
        m_ref[h] = m_new
        pvs.append(jnp.dot(p.astype(BF16), v2, preferred_element_type=F32))
        alphas.append(alpha)
    acc_ref[...] = jnp.where(low, alphas[0], alphas[1]) * acc_ref[...] + jnp.where(low, pvs[0], pvs[1])


def _fox_sample_kernel(n_cache_tiles, q_ref, ck_ref, cv_ref, r_ref, nk_ref, nv_ref, dqc_ref, dqr_ref,
                       a_in, o_ref, m_ref, l_ref, acc_ref):
    del a_in
    s_idx = pl.program_id(1)
    tq = q_ref.shape[0]

    @pl.when(s_idx == 0)
    def _():
        m_ref[...] = jnp.full_like(m_ref, -jnp.inf)
        l_ref[...] = jnp.zeros_like(l_ref)
        acc_ref[...] = jnp.zeros_like(acc_ref)

    dq = dqc_ref[...]

    def pair_refs(p):
        cols = pl.ds(p * LANES, LANES)
        return m_ref.at[p], l_ref.at[p], acc_ref.at[:, cols]

    @pl.when(s_idx < n_cache_tiles)
    def _():
        r = r_ref[0, 0]
        for p in range(HEAD_PAIRS):
            cols = slice(p * LANES, (p + 1) * LANES)
            bias_a = dq[:, 2 * p:2 * p + 1] + r[2 * p:2 * p + 1, :]
            bias_b = dq[:, 2 * p + 1:2 * p + 2] + r[2 * p + 1:2 * p + 2, :]
            _sample_tile(q_ref[:, cols], ck_ref[0, 0, :, cols].astype(BF16), cv_ref[0, 0, :, cols].astype(BF16),
                         bias_a, bias_b, None, *pair_refs(p))

    @pl.when(s_idx == n_cache_tiles)
    def _():
        dk = dqr_ref[0]
        causal = lax.broadcasted_iota(jnp.int32, (tq, tq), 0) >= lax.broadcasted_iota(jnp.int32, (tq, tq), 1)
        for p in range(HEAD_PAIRS):
            cols = slice(p * LANES, (p + 1) * LANES)
            bias_a = dq[:, 2 * p:2 * p + 1] - dk[2 * p:2 * p + 1, :]
            bias_b = dq[:, 2 * p + 1:2 * p + 2] - dk[2 * p + 1:2 * p + 2, :]
            _sample_tile(q_ref[:, cols], nk_ref[0, :, cols].astype(BF16), nv_ref[0, :, cols].astype(BF16),
                         bias_a, bias_b, causal, *pair_refs(p))
        low = lax.broadcasted_iota(jnp.int32, (tq, LANES), 1) < DH_A
        for p in range(HEAD_PAIRS):
            cols = slice(p * LANES, (p + 1) * LANES)
            o_ref[:, cols] = (acc_ref[:, cols] / jnp.where(low, l_ref[p, 0], l_ref[p, 1])).astype(o_ref.dtype)


def _fox_sample(layer, q, cache_k, cache_v, rsuf, ks, vs, dcol, drow_s, a_buf, n_prompt, tile):
    depth, dec_batch, past, _ = cache_k.shape
    dec_seq = ks.shape[1] // dec_batch
    nct = past // tile
    q0 = n_prompt // dec_seq
    ci = lambda b, s: (layer, b, jnp.minimum(s, nct - 1), 0)
    return pl.pallas_call(
        functools.partial(_fox_sample_kernel, nct),
        grid=(dec_batch, nct + 1),
        in_specs=[
            pl.BlockSpec((dec_seq, D_A), lambda b, s: (q0 + b, 0)),
            pl.BlockSpec((1, 1, tile, D_A), ci),
            pl.BlockSpec((1, 1, tile, D_A), ci),
            pl.BlockSpec((1, 1, H_A, tile), lambda b, s: (layer, b, 0, jnp.minimum(s, nct - 1))),
            pl.BlockSpec((1, dec_seq, D_A), lambda b, s: (layer, b, 0)),
            pl.BlockSpec((1, dec_seq, D_A), lambda b, s: (layer, b, 0)),
            pl.BlockSpec((dec_seq, H_A), lambda b, s: (q0 + b, 0)),
            pl.BlockSpec((1, H_A, dec_seq), lambda b, s: (b, 0, 0)),
            pl.BlockSpec(memory_space=pl.ANY),
        ],
        out_specs=pl.BlockSpec((dec_seq, D_A), lambda b, s: (q0 + b, 0)),
        out_shape=jax.ShapeDtypeStruct(a_buf.shape, a_buf.dtype),
        scratch_shapes=[pltpu.VMEM((HEAD_PAIRS, 2, dec_seq, LANES), F32),
                        pltpu.VMEM((HEAD_PAIRS, 2, dec_seq, LANES), F32),
                        pltpu.VMEM((dec_seq, D_A), F32)],
        input_output_aliases={8: 0},
        compiler_params=_cparams("parallel", "arbitrary"),
        name="fox_sample",
    )(q, cache_k, cache_v, rsuf, ks, vs, dcol, drow_s, a_buf)


def _suffix_sum_kernel(x_ref, u_ref, o_ref, carry_ref):
    j = pl.program_id(0)

    @pl.when(j == 0)
    def _():
        carry_ref[...] = jnp.zeros_like(carry_ref)

    x = x_ref[...]
    hi, mid, lo = _split3(x)
    u = u_ref[...]
    loc = (jnp.dot(hi, u, preferred_element_type=F32) + jnp.dot(mid, u, preferred_element_type=F32)
           + jnp.dot(lo, u, preferred_element_type=F32))
    o_ref[...] = (loc + carry_ref[:, 0:1]) * LOG2E
    carry_ref[...] = carry_ref[...] + jnp.sum(x, axis=1, keepdims=True)


def _suffix_sums(x):
    rows, n = x.shape
    tb = min(512, n)
    nb = n // tb
    u = (lax.broadcasted_iota(jnp.int32, (tb, tb), 0) > lax.broadcasted_iota(jnp.int32, (tb, tb), 1)).astype(BF16)
    return pl.pallas_call(
        _suffix_sum_kernel,
        grid=(nb,),
        in_specs=[pl.BlockSpec((rows, tb), lambda j: (0, nb - 1 - j)),
                  pl.BlockSpec((tb, tb), lambda j: (0, 0))],
        out_specs=pl.BlockSpec((rows, tb), lambda j: (0, nb - 1 - j)),
        out_shape=jax.ShapeDtypeStruct((rows, n), F32),
        scratch_shapes=[pltpu.VMEM((rows, LANES), F32)],
        compiler_params=_cparams("arbitrary"),
        name="cache_suffix_sums",
    )(x, u)


def _out_proj_kernel(alpha, a_ref, bn_ref, x_ref, gna_ref, wo_ref, g1_ref, b1_ref, wrh_ref, wrl_ref, br_ref, ls_ref,
                     x1_ref, x1b_ref, route_ref, counts_ref, carry_ref):
    i = pl.program_id(0)
    tm = x_ref.shape[0]

    @pl.when(i == 0)
    def _():
        carry_ref[...] = jnp.zeros_like(carry_ref)

    an = _rms_norm(a_ref[...].astype(F32), gna_ref[...]).astype(BF16)
    mix = (jnp.dot(an, wo_ref[0:D_A, :], preferred_element_type=F32)
           + jnp.dot(bn_ref[...], wo_ref[D_A:, :], preferred_element_type=F32))
    x1 = _layer_norm(alpha * x_ref[...] + mix, g1_ref[...], b1_ref[...])
    x1_ref[...] = x1
    x1h = x1.astype(BF16)
    x1b_ref[...] = _pack_halves(x1)

    x1l = (x1 - x1h.astype(F32)).astype(BF16)
    logits = (jnp.dot(x1h, wrh_ref[...], preferred_element_type=F32)
              + jnp.dot(x1l, wrh_ref[...], preferred_element_type=F32)
              + jnp.dot(x1h, wrl_ref[...], preferred_element_type=F32)) + br_ref[...]
    lane = lax.broadcasted_iota(jnp.int32, (tm, LANES), 1)
    is_group = lane < N_GROUPS
    gl = jnp.where(is_group, logits, NEG_BIG)
    gmax = jnp.max(gl, axis=1, keepdims=True)
    g_idx = jnp.min(jnp.where(gl == gmax, lane, LANES), axis=1, keepdims=True)
    g_prob = 1.0 / jnp.sum(jnp.exp(gl - gmax), axis=1, keepdims=True)
    in_group = jnp.logical_and(lane >= N_GROUPS, lane < N_GROUPS + N_EXPERTS)
    in_group = jnp.logical_and(in_group, lax.shift_right_arithmetic(lane - N_GROUPS, 3) == g_idx)
    el = jnp.where(in_group, logits, NEG_BIG)
    e1 = jnp.max(el, axis=1, keepdims=True)
    i1 = jnp.min(jnp.where(el == e1, lane, LANES), axis=1, keepdims=True)
    el2 = jnp.where(lane == i1, NEG_BIG, el)
    e2 = jnp.max(el2, axis=1, keepdims=True)
    i2 = jnp.min(jnp.where(el2 == e2, lane, LANES), axis=1, keepdims=True)
    t2 = jnp.exp(e2 - e1)
    w1 = 1.0 / (1.0 + t2)
    gate1 = g_prob * w1
    gate2 = g_prob * (t2 * w1)

    oh1 = lane == i1
    oh2 = lane == i2
    both = (oh1.astype(F32) + oh2.astype(F32)).astype(BF16)
    before = jnp.dot(ls_ref[...], both, preferred_element_type=F32) + carry_ref[...]
    rank1 = jnp.sum(jnp.where(oh1, before, 0.0), axis=1, keepdims=True)
    rank2 = jnp.sum(jnp.where(oh2, before, 0.0), axis=1, keepdims=True)
    carry_ref[...] = carry_ref[...] + jnp.sum(both.astype(F32), axis=0, keepdims=True)
    counts_ref[...] = carry_ref[...]

    route = jnp.where(lane == 0, (i1 - N_GROUPS).astype(F32), 0.0)
    route = jnp.where(lane == 1, (i2 - N_GROUPS).astype(F32), route)
    route = jnp.where(lane == 2, rank1, route)
    route = jnp.where(lane == 3, rank2, route)
    route = jnp.where(lane == 4, gate1, route)
    route = jnp.where(lane == 5, gate2, route)
    route_ref[...] = route


def _out_proj(alpha, a, bn, x, gna, wo, g1, b1, wrh, wrl, br, lstrict):
    t_rows = x.shape[0]
    tm = TOKEN_TILE
    full = lambda arr: pl.BlockSpec(arr.shape, lambda i: (0,) * arr.ndim)
    row = lambda w: pl.BlockSpec((tm, w), lambda i: (i, 0))
    return pl.pallas_call(
        functools.partial(_out_proj_kernel, alpha),
        grid=(t_rows // tm,),
        in_specs=[row(D_A), row(D_B), row(D_MODEL), full(gna), full(wo), full(g1), full(b1), full(wrh), full(wrl),
                  full(br), full(lstrict)],
        out_specs=(row(D_MODEL), row(HALF_MODEL), row(LANES), pl.BlockSpec((1, LANES), lambda i: (0, 0))),
        out_shape=(jax.ShapeDtypeStruct((t_rows, D_MODEL), F32),
                   jax.ShapeDtypeStruct((t_rows, HALF_MODEL), jnp.int32),
                   jax.ShapeDtypeStruct((t_rows, LANES), F32),
                   jax.ShapeDtypeStruct((1, LANES), F32)),
        scratch_shapes=[pltpu.VMEM((1, LANES), F32)],
        compiler_params=_cparams("arbitrary"),
        name="out_proj_router",
    )(a, bn, x, gna, wo, g1, b1, wrh, wrl, br, lstrict)


def _pack_halves(x):
    lo = lax.bitcast_convert_type(x[:, :HALF_MODEL].astype(BF16).astype(F32), jnp.uint32)
    hi = lax.bitcast_convert_type(x[:, HALF_MODEL:].astype(BF16).astype(F32), jnp.uint32)
    return lax.bitcast_convert_type((hi & jnp.uint32(0xFFFF0000)) | (lo >> 16), jnp.int32)


def _unpack_halves(w):
    u = lax.bitcast_convert_type(w, jnp.uint32)
    return (lax.bitcast_convert_type(u << 16, F32), lax.bitcast_convert_type(u & jnp.uint32(0xFFFF0000), F32))


def _expert_kernel(ib_ref, ie_ref, ni_ref, st_ref, en_ref, x_ref, w1_ref, w3_ref, w2_ref, o_ref,
                   w1b_ref, w3b_ref, w2b_ref):
    i = pl.program_id(0)
    bm = x_ref.shape[0]

    @pl.when(i < ni_ref[0])
    def _():
        e = ie_ref[i]
        b = ib_ref[i]
        prev = jnp.maximum(i - 1, 0)
        first_item = i == 0

        @pl.when(jnp.logical_or(first_item, ie_ref[prev] != e))
        def _():
            w1b_ref[...] = w1_ref[0, 0].astype(BF16)
            w3b_ref[...] = w3_ref[0, 0].astype(BF16)
            w2b_ref[...] = w2_ref[0, 0].astype(BF16)

        x_lo, x_hi = (v.astype(BF16) for v in _unpack_halves(x_ref[...]))
        h1 = (jnp.dot(x_lo, w1b_ref[:HALF_MODEL, :], preferred_element_type=F32)
              + jnp.dot(x_hi, w1b_ref[HALF_MODEL:, :], preferred_element_type=F32))
        h3 = (jnp.dot(x_lo, w3b_ref[:HALF_MODEL, :], preferred_element_type=F32)
              + jnp.dot(x_hi, w3b_ref[HALF_MODEL:, :], preferred_element_type=F32))
        h = (h1 * (1.0 / (1.0 + jnp.exp(-h1))) * h3).astype(BF16)
        y = _pack_halves(jnp.dot(h, w2b_ref[...], preferred_element_type=F32))
        first_visit = jnp.logical_or(first_item, ib_ref[prev] != b)

        @pl.when(first_visit)
        def _():
            o_ref[...] = y

        @pl.when(jnp.logical_not(first_visit))
        def _():
            row = b * bm + lax.broadcasted_iota(jnp.int32, (bm, 1), 0)
            mine = jnp.logical_and(row >= st_ref[e], row < en_ref[e])
            o_ref[...] = jnp.where(mine, y, o_ref[...])


def _experts(layer, item_block, item_expert, n_items, starts, ends, xs, w1, w3, w2):
    n_rows = xs.shape[0]
    bm = MOE_ROWS
    n_max = item_block.shape[0]
    item = lambda i, ni: jnp.minimum(i, ni[0] - 1)
    blk = lambda i, ib, ie, ni, st, en: (ib[item(i, ni)], 0)
    wsel = lambda i, ib, ie, ni, st, en: (layer, ie[item(i, ni)], 0, 0)
    grid_spec = pltpu.PrefetchScalarGridSpec(
        num_scalar_prefetch=5,
        grid=(n_max,),
        in_specs=[pl.BlockSpec((bm, HALF_MODEL), blk),
                  pl.BlockSpec((1, 1, D_MODEL, D_EXPERT), wsel),
                  pl.BlockSpec((1, 1, D_MODEL, D_EXPERT), wsel),
                  pl.BlockSpec((1, 1, D_EXPERT, D_MODEL), wsel)],
        out_specs=pl.BlockSpec((bm, HALF_MODEL), blk),
        scratch_shapes=[pltpu.VMEM((D_MODEL, D_EXPERT), BF16), pltpu.VMEM((D_MODEL, D_EXPERT), BF16),
                        pltpu.VMEM((D_EXPERT, D_MODEL), BF16)],
    )
    return pl.pallas_call(
        _expert_kernel,
        grid_spec=grid_spec,
        out_shape=jax.ShapeDtypeStruct((n_rows, HALF_MODEL), jnp.int32),
        compiler_params=_cparams("arbitrary"),
        name="experts",
    )(item_block, item_expert, n_items, starts, ends, xs, w1, w3, w2)


def _combine_kernel(alpha, n_first, x1_ref, y0_ref, y1_ref, route_ref, g2_ref, b2_ref, *o_refs):
    route = route_ref[...]
    g0, g1 = route[:, 4:5], route[:, 5:6]
    y0_lo, y0_hi = _unpack_halves(y0_ref[...])
    y1_lo, y1_hi = _unpack_halves(y1_ref[...])
    z = jnp.concatenate([alpha * x1_ref[:, :HALF_MODEL] + (g0 * y0_lo + g1 * y1_lo),
                         alpha * x1_ref[:, HALF_MODEL:] + (g0 * y0_hi + g1 * y1_hi)], axis=1)
    out = _layer_norm(z, g2_ref[...], b2_ref[...])
    if n_first is None:
        o_refs[0][...] = out
    else:
        first = pl.program_id(0) < n_first

        @pl.when(first)
        def _():
            o_refs[0][...] = out

        @pl.when(jnp.logical_not(first))
        def _():
            o_refs[1][...] = out


def _combine(alpha, x1, y01, route, g2, b2, split_rows=None):
    t_rows = x1.shape[0]
    tm = TOKEN_TILE
    n_tiles = t_rows // tm
    full = lambda arr: pl.BlockSpec(arr.shape, lambda i: (0,) * arr.ndim)
    row = lambda w: pl.BlockSpec((tm, w), lambda i: (i, 0))
    if split_rows is None:
        n_first = None
        out_specs = row(D_MODEL)
        out_shape = jax.ShapeDtypeStruct((t_rows, D_MODEL), F32)
    else:
        n_first = split_rows // tm
        out_specs = (pl.BlockSpec((tm, D_MODEL), lambda i: (jnp.minimum(i, n_first - 1), 0)),
                     pl.BlockSpec((tm, D_MODEL), lambda i: (jnp.maximum(i - n_first, 0), 0)))
        out_shape = (jax.ShapeDtypeStruct((split_rows, D_MODEL), F32),
                     jax.ShapeDtypeStruct((t_rows - split_rows, D_MODEL), F32))
    return pl.pallas_call(
        functools.partial(_combine_kernel, alpha, n_first),
        grid=(n_tiles,),
        in_specs=[row(D_MODEL), row(HALF_MODEL), pl.BlockSpec((tm, HALF_MODEL), lambda i: (i + n_tiles, 0)),
                  row(LANES), full(g2), full(b2)],
        out_specs=out_specs,
        out_shape=out_shape,
        compiler_params=_cparams("arbitrary"),
        name="moe_combine",
    )(x1, y01, y01, route, g2, b2)


def _gather_rows(table, idx):
    n_rows, width = idx.shape[0], table.shape[1]
    workers = SC_CORES * SC_SUBCORES
    step = workers * SC_GATHER_ROWS
    n_pad = -(-n_rows // step) * step
    if n_pad != n_rows:
        idx = jnp.pad(idx, (0, n_pad - n_rows))
    per_worker = n_pad // workers
    n_chunks = per_worker // SC_GATHER_ROWS
    mesh = plsc.VectorSubcoreMesh(core_axis_name="c", subcore_axis_name="s")

    @functools.partial(
        pl.kernel, mesh=mesh,
        out_type=jax.ShapeDtypeStruct((n_pad, width), table.dtype),
        scratch_types=[pltpu.VMEM((SC_GATHER_ROWS,), jnp.int32),
                       pltpu.VMEM((SC_GATHER_ROWS, width), table.dtype),
                       pltpu.SemaphoreType.DMA],
        name="sc_gather_rows",
    )
    def gather(table_hbm, idx_hbm, out_hbm, idx_v, rows_v, sem):
        base = (lax.axis_index("s") * SC_CORES + lax.axis_index("c")) * per_worker

        @pl.loop(0, n_chunks)
        def _(c):
            off = base + c * SC_GATHER_ROWS
            pltpu.sync_copy(idx_hbm.at[pl.ds(off, SC_GATHER_ROWS)], idx_v)
            pltpu.async_copy(table_hbm.at[idx_v], rows_v, sem).wait()
            pltpu.sync_copy(rows_v, out_hbm.at[pl.ds(off, SC_GATHER_ROWS)])

    out = gather(table, idx)
    return out if n_pad == n_rows else out[:n_rows]


def _attention_tile(seq):
    for t in (512, 256, 128):
        if seq % t == 0:
            return t
    raise ValueError("sequence length must be a multiple of 128")


def kernel(x_prompt, x_sample, cache_k, cache_v, cache_logf, w_in, b_f, ln_v_g, ln_v_b, w_s, b_s,
           g_norm_a, g_norm_b, w_out, ln1_g, ln1_b, w_gr, b_gr, w_er, b_er, w1, w3, w2, ln2_g, ln2_b):
    batch, seq, _ = x_prompt.shape
    dec_batch, dec_seq, _ = x_sample.shape
    depth = w_in.shape[0]
    past = cache_k.shape[2]
    n_prompt = batch * seq
    n_sample = dec_batch * dec_seq
    t_rows = n_prompt + n_sample
    alpha = float((2 * depth) ** 0.25)
    tm = TOKEN_TILE
    assert seq % tm == 0 and n_sample % tm == 0 and tm % dec_seq == 0 and dec_seq == GMLP_CHUNK // 2
    assert past % 128 == 0

    x = jnp.concatenate([x_prompt.reshape(n_prompt, D_MODEL), x_sample.reshape(n_sample, D_MODEL)], axis=0)

    sp = (D_A, 2 * D_A, 3 * D_A, 3 * D_A + H_A, 3 * D_A + H_A + D_B)
    wq, wk, wv, wf, wu, wgv = (w_in[..., a:b] for a, b in zip((0,) + sp, sp + (w_in.shape[-1],)))
    wf_pad = jnp.pad(wf, ((0, 0), (0, 0), (0, LANES - H_A)))
    w_cat = jnp.concatenate([wq * (LOG2E * DH_A ** -0.5), wk, wv, wu, wgv, wf_pad], axis=-1).astype(BF16)
    bf_pad = jnp.pad(b_f, ((0, 0), (0, LANES - H_A)))[:, None, :]
    half = GMLP_CHUNK // 2
    wmix = jnp.stack([w_s, jnp.tile(w_s[:, :, :half, :half], (1, 1, 2, 2))], axis=1)
    bs_t = jnp.swapaxes(b_s, 1, 2)
    bs_var = jnp.stack([bs_t, jnp.tile(bs_t[:, :half], (1, 2, 1))], axis=1)
    bmix = jnp.repeat(bs_var, C_B, axis=-1)
    wo_b = w_out.astype(BF16)
    wr = jnp.pad(jnp.concatenate([w_gr, w_er], axis=-1), ((0, 0), (0, 0), (0, LANES - N_GROUPS - N_EXPERTS)))
    wrh = wr.astype(BF16)
    wrl = (wr - wrh.astype(F32)).astype(BF16)
    br = jnp.pad(jnp.concatenate([b_gr, b_er], axis=-1), ((0, 0), (0, LANES - N_GROUPS - N_EXPERTS)))[:, None, :]
    row2 = lambda a: a[:, None, :]

    ri = lax.broadcasted_iota(jnp.int32, (tm, tm), 0)
    ci = lax.broadcasted_iota(jnp.int32, (tm, tm), 1)
    tri = jnp.stack([ri >= ci, jnp.logical_and(ri >= ci, ri // dec_seq == ci // dec_seq)]).astype(BF16)
    lstrict = (ri > ci).astype(BF16)
    prow = lax.broadcasted_iota(jnp.int32, (LANES, H_A * LANES), 0)
    pcol = lax.broadcasted_iota(jnp.int32, (LANES, H_A * LANES), 1)
    phead = pcol // LANES
    poff = pcol % LANES - jnp.where(phead % 2 == 0, DH_A, 0)
    is_one = jnp.logical_and(prow == 3 * H_A, jnp.logical_and(poff >= 0, poff < 3))
    is_part = jnp.logical_and(jnp.logical_and(poff >= 3, poff < 6), prow == (poff - 3) * H_A + phead)
    place = jnp.logical_or(is_one, is_part).astype(BF16)

    clf = jnp.transpose(cache_logf, (0, 1, 3, 2)).reshape(depth * dec_batch * H_A, past)
    rsuf = _suffix_sums(clf).reshape(depth, dec_batch, H_A, past)
    ck = cache_k.reshape(depth, dec_batch, past, D_A)
    cv = cache_v.reshape(depth, dec_batch, past, D_A)

    kp = jnp.zeros((depth, n_prompt, D_A), F32)
    vp = jnp.zeros((depth, n_prompt, D_A), F32)
    ks = jnp.zeros((depth, n_sample, D_A), F32)
    vs = jnp.zeros((depth, n_sample, D_A), F32)
    logfs, gvns = [], []

    bm = MOE_ROWS
    n_assign = 2 * t_rows
    nblk = n_assign // bm
    expert_ids = jnp.arange(N_EXPERTS, dtype=jnp.int32)
    item_ids = jnp.arange(nblk + N_EXPERTS - 1, dtype=jnp.int32)
    q_tile = PROMPT_Q_TILE if seq % PROMPT_Q_TILE == 0 else _attention_tile(seq)
    k_tile = PROMPT_K_TILE if q_tile % PROMPT_K_TILE == 0 else q_tile
    c_tile = SAMPLE_K_TILE if past % SAMPLE_K_TILE == 0 else _attention_tile(past)
    assign_ids = jnp.arange(n_assign, dtype=jnp.int32)

    for l in range(depth):
        q, kp, vp, ks, vs, kaug, vta, logf, dT, dcol, bn, gvn = _in_proj(
            l, x, w_cat[l], bf_pad[l], row2(ln_v_g)[l], row2(ln_v_b)[l], wmix[l], bmix[l], row2(g_norm_b)[l], tri,
            place, kp, vp, ks, vs, n_prompt, seq)
        logfs.append(logf)
        gvns.append(gvn)

        dpair = jnp.pad(dT.reshape(HEAD_PAIRS, 2, t_rows), ((0, 0), (0, SUBLANES - 2), (0, 0)))
        a = _fox_prompt(q, kaug, vta, dpair, batch, seq, q_tile, k_tile)
        drow_s = jnp.transpose(dT[:, n_prompt:].reshape(H_A, dec_batch, dec_seq), (1, 0, 2))
        a = _fox_sample(l, q, ck, cv, rsuf, ks, vs, dcol, drow_s, a, n_prompt, c_tile)

        x1, x1b, route, counts = _out_proj(alpha, a, bn, x, row2(g_norm_a)[l], wo_b[l], row2(ln1_g)[l],
                                           row2(ln1_b)[l], wrh[l], wrl[l], br[l], lstrict)

        cnt = counts[0, N_GROUPS:N_GROUPS + N_EXPERTS].astype(jnp.int32)
        ends = jnp.cumsum(cnt)
        starts = ends - cnt
        eid = route[:, 0:2].astype(jnp.int32)
        rank = route[:, 2:4].astype(jnp.int32)
        onehot = eid[:, :, None] == expert_ids[None, None, :]
        pos = jnp.sum(jnp.where(onehot, starts[None, None, :], 0), axis=-1) + rank
        order = jnp.sort(eid.reshape(n_assign) * n_assign + assign_ids) % n_assign
        first_blk = starts // bm
        n_it = jnp.where(cnt > 0, (ends - 1) // bm - first_blk + 1, 0)
        it_end = jnp.cumsum(n_it)
        item_expert = jnp.minimum(jnp.sum(item_ids[:, None] >= it_end[None, :], axis=1), N_EXPERTS - 1).astype(jnp.int32)
        it_first = jnp.sum(jnp.where(item_expert[:, None] == expert_ids[None, :], (it_end - n_it)[None, :], 0), axis=1)
        it_blk0 = jnp.sum(jnp.where(item_expert[:, None] == expert_ids[None, :], first_blk[None, :], 0), axis=1)
        item_block = jnp.clip(it_blk0 + item_ids - it_first, 0, nblk - 1).astype(jnp.int32)

        xs = _gather_rows(x1b, order // 2)
        yb = _experts(l, item_block, item_expert, it_end[-1:].astype(jnp.int32), starts, ends, xs, w1, w3, w2)
        y01 = _gather_rows(yb, jnp.concatenate([pos[:, 0], pos[:, 1]]))
        x = _combine(alpha, x1, y01, route, row2(ln2_g)[l], row2(ln2_b)[l],
                     split_rows=n_prompt if l == depth - 1 else None)

    y_prompt = x[0].reshape(batch, seq, D_MODEL)
    y_sample = x[1].reshape(dec_batch, dec_seq, D_MODEL)
    logf_all = jnp.stack(logfs)
    return (y_prompt, y_sample,
            kp.reshape(depth, batch, seq, H_A, DH_A), vp.reshape(depth, batch, seq, H_A, DH_A),
            logf_all[:, :n_prompt].reshape(depth, batch, seq, H_A),
            ks.reshape(depth, dec_batch, dec_seq, H_A, DH_A), vs.reshape(depth, dec_batch, dec_seq, H_A, DH_A),
            logf_all[:, n_prompt:].reshape(depth, dec_batch, dec_seq, H_A),
            jnp.stack(gvns).reshape(depth, dec_batch, dec_seq, D_B))
```

```python
import functools

import jax
import jax.numpy as jnp
from jax import lax
from jax.experimental import pallas as pl
from jax.experimental.pallas import tpu as pltpu
from jax.experimental.pallas import tpu_sc as plsc

F32 = jnp.float32
BF16 = jnp.bfloat16

D_MODEL = 1024
D_A = 512
H_A = 8
DH_A = 64
D_B = 512
G_B = 8
C_B = 64
GMLP_CHUNK = 128
N_GROUPS = 4
EXPERTS_PER_GROUP = 8
N_EXPERTS = N_GROUPS * EXPERTS_PER_GROUP
D_EXPERT = 512
LN_EPS = 1e-5
HEAD_PAIRS = H_A // 2
HALF_MODEL = D_MODEL // 2

LANES = 128
SUBLANES = 8
VMEM_LIMIT_BYTES = 56 * 1024 * 1024
SC_CORES = 2
SC_SUBCORES = 16
SC_GATHER_ROWS = 128

TOKEN_TILE = 512
MOE_ROWS = 512
PROMPT_Q_TILE = 1024
PROMPT_K_TILE = 1024
SAMPLE_K_TILE = 2048
PROMPT_HEADS = 4
NEG_BIG = -1e30
LOG2E = 1.4426950408889634


def _cparams(*sem):
    return pltpu.CompilerParams(dimension_semantics=sem, vmem_limit_bytes=VMEM_LIMIT_BYTES)


def _split3(x):
    hi = x.astype(BF16)
    r1 = x - hi.astype(F32)
    mid = r1.astype(BF16)
    lo = (r1 - mid.astype(F32)).astype(BF16)
    return hi, mid, lo


def _gelu_tanh(x):
    return 0.5 * x * (1.0 + jnp.tanh(0.7978845608028654 * (x + 0.044715 * (x * x * x))))


def _log_sigmoid(z):
    return jnp.minimum(z, 0.0) - jnp.log(1.0 + jnp.exp(-jnp.abs(z)))


def _layer_norm(x, g, b):
    mu = jnp.mean(x, axis=-1, keepdims=True)
    xc = x - mu
    var = jnp.mean(xc * xc, axis=-1, keepdims=True)
    return xc * lax.rsqrt(var + LN_EPS) * g + b


def _rms_norm(x, g):
    return x * lax.rsqrt(jnp.mean(x * x, axis=-1, keepdims=True) + LN_EPS) * g


def _in_proj_kernel(n_prompt_tiles, tiles_per_seq,
                    x_ref, w_ref, bf_ref, lnvg_ref, lnvb_ref, wmix_ref, bmix_ref, gnb_ref, tri_ref, place_ref,
                    kp_in, vp_in, ks_in, vs_in,
                    q_ref, kp_ref, vp_ref, ks_ref, vs_ref, kaug_ref, vta_ref, logf_ref, dT_ref, dcol_ref, bn_ref,
                    gvn_ref, carry_ref):
    del kp_in, vp_in, ks_in, vs_in
    i = pl.program_id(0)
    tm = x_ref.shape[0]
    is_sample = i >= n_prompt_tiles
    var = is_sample.astype(jnp.int32)

    p = jnp.dot(x_ref[...].astype(BF16), w_ref[...], preferred_element_type=F32)
    q = p[:, 0:D_A]
    k = p[:, D_A:2 * D_A]
    v = p[:, 2 * D_A:3 * D_A]
    u = p[:, 3 * D_A:3 * D_A + D_B]
    gv = p[:, 3 * D_A + D_B:3 * D_A + 2 * D_B]
    fl = p[:, 3 * D_A + 2 * D_B:]

    q_ref[...] = q.astype(BF16)
    row_v = lax.broadcasted_iota(jnp.int32, (LANES, tm), 0)
    tail = jnp.where(row_v == DH_A, 1.0, 0.0)
    for j in range(HEAD_PAIRS):
        vt_pair = v[:, j * LANES:(j + 1) * LANES].T
        vta_ref[2 * j] = jnp.where(row_v < DH_A, vt_pair, tail).astype(BF16)
        vta_ref[2 * j + 1] = jnp.where(row_v < DH_A, pltpu.roll(vt_pair, DH_A, 0), tail).astype(BF16)

    @pl.when(jnp.logical_not(is_sample))
    def _():
        kp_ref[0] = k
        vp_ref[0] = v

    @pl.when(is_sample)
    def _():
        ks_ref[0] = k
        vs_ref[0] = v

    logf = _log_sigmoid(fl + bf_ref[...])
    logf_ref[...] = logf[:, 0:H_A]
    hi, mid, lo = _split3(logf)
    parts = jnp.concatenate([hi, mid, lo], axis=1)
    cs = jnp.dot(tri_ref[var], parts, preferred_element_type=F32)
    cs = cs[:, 0:LANES] + cs[:, LANES:2 * LANES] + cs[:, 2 * LANES:3 * LANES]

    @pl.when(jnp.logical_or(is_sample, i % tiles_per_seq == 0))
    def _():
        carry_ref[...] = jnp.zeros_like(carry_ref)

    d = cs + carry_ref[...]
    carry_ref[...] = d[tm - 1:tm, :]
    d2 = d * LOG2E
    dcol_ref[...] = d2[:, 0:H_A]
    dT_ref[...] = d2.T[0:H_A, :]

    nh, nm, nl = _split3(-d2)
    lane_t = lax.broadcasted_iota(jnp.int32, (tm, LANES), 1)
    dparts = jnp.where(lane_t < H_A, nh,
                       jnp.where(lane_t < 2 * H_A, pltpu.roll(nm, H_A, 1),
                                 jnp.where(lane_t < 3 * H_A, pltpu.roll(nl, 2 * H_A, 1),
                                           jnp.where(lane_t == 3 * H_A, 1.0, 0.0).astype(BF16))))
    aug = jnp.dot(dparts, place_ref[...], preferred_element_type=F32)
    low_t = lane_t < DH_A
    for h in range(H_A):
        kh = k[:, (h // 2) * LANES:(h // 2 + 1) * LANES]
        own = low_t if h % 2 == 0 else jnp.logical_not(low_t)
        kaug_ref[h] = jnp.where(own, kh, aug[:, h * LANES:(h + 1) * LANES]).astype(BF16)

    ug = _gelu_tanh(u)
    vn = _layer_norm(_gelu_tanh(gv), lnvg_ref[...], lnvb_ref[...])

    @pl.when(is_sample)
    def _():
        gvn_ref[...] = vn

    r_io = lax.broadcasted_iota(jnp.int32, (GMLP_CHUNK, GMLP_CHUNK), 0)
    c_io = lax.broadcasted_iota(jnp.int32, (GMLP_CHUNK, GMLP_CHUNK), 1)
    causal = r_io >= c_io
    half = GMLP_CHUNK // 2
    same_half = (r_io >= half) == (c_io >= half)
    keep = jnp.logical_and(causal, jnp.logical_or(jnp.logical_not(is_sample), same_half))
    lane = lax.broadcasted_iota(jnp.int32, (GMLP_CHUNK, LANES), 1)
    low_lanes = lane < C_B
    vnb = vn.astype(BF16)
    bias = bmix_ref[var]
    mixed_rows = []
    for r in range(tm // GMLP_CHUNK):
        rows = slice(r * GMLP_CHUNK, (r + 1) * GMLP_CHUNK)
        cols_out = []
        for j in range(G_B // 2):
            vj = vnb[rows, j * LANES:(j + 1) * LANES]
            m0 = jnp.where(keep, wmix_ref[var, 2 * j], 0.0).astype(BF16)
            m1 = jnp.where(keep, wmix_ref[var, 2 * j + 1], 0.0).astype(BF16)
            y0 = jnp.dot(m0, vj, preferred_element_type=F32)
            y1 = jnp.dot(m1, vj, preferred_element_type=F32)
            cols_out.append(jnp.where(low_lanes, y0, y1))
        mixed_rows.append(jnp.concatenate(cols_out, axis=1) + bias)
    mixed = jnp.concatenate(mixed_rows, axis=0)
    b_out = ug * mixed
    bn_ref[...] = _rms_norm(b_out, gnb_ref[...]).astype(BF16)


def _in_proj(layer, x, w, bf, lnvg, lnvb, wmix, bmix, gnb, tri, place, kp, vp, ks, vs, n_prompt, seq):
    t_rows = x.shape[0]
    tm = TOKEN_TILE
    n_tiles = t_rows // tm
    npt = n_prompt // tm
    n_sample = t_rows - n_prompt
    tiles_per_seq = seq // tm
    full = lambda a: pl.BlockSpec(a.shape, lambda i: (0,) * a.ndim)
    any_spec = pl.BlockSpec(memory_space=pl.ANY)
    p_idx = lambda i: (layer, jnp.minimum(i, npt - 1), 0)
    s_idx = lambda i: (layer, jnp.maximum(i - npt, 0), 0)
    out_shapes = (
        jax.ShapeDtypeStruct((t_rows, D_A), BF16),
        jax.ShapeDtypeStruct(kp.shape, F32),
        jax.ShapeDtypeStruct(vp.shape, F32),
        jax.ShapeDtypeStruct(ks.shape, F32),
        jax.ShapeDtypeStruct(vs.shape, F32),
        jax.ShapeDtypeStruct((H_A, t_rows, LANES), BF16),
        jax.ShapeDtypeStruct((H_A, LANES, t_rows), BF16),
        jax.ShapeDtypeStruct((t_rows, H_A), F32),
        jax.ShapeDtypeStruct((H_A, t_rows), F32),
        jax.ShapeDtypeStruct((t_rows, H_A), F32),
        jax.ShapeDtypeStruct((t_rows, D_B), BF16),
        jax.ShapeDtypeStruct((n_sample, D_B), F32),
    )
    out_specs = (
        pl.BlockSpec((tm, D_A), lambda i: (i, 0)),
        pl.BlockSpec((1, tm, D_A), p_idx),
        pl.BlockSpec((1, tm, D_A), p_idx),
        pl.BlockSpec((1, tm, D_A), s_idx),
        pl.BlockSpec((1, tm, D_A), s_idx),
        pl.BlockSpec((H_A, tm, LANES), lambda i: (0, i, 0)),
        pl.BlockSpec((H_A, LANES, tm), lambda i: (0, 0, i)),
        pl.BlockSpec((tm, H_A), lambda i: (i, 0)),
        pl.BlockSpec((H_A, tm), lambda i: (0, i)),
        pl.BlockSpec((tm, H_A), lambda i: (i, 0)),
        pl.BlockSpec((tm, D_B), lambda i: (i, 0)),
        pl.BlockSpec((tm, D_B), lambda i: (jnp.maximum(i - npt, 0), 0)),
    )
    in_specs = [pl.BlockSpec((tm, D_MODEL), lambda i: (i, 0)),
                full(w), full(bf), full(lnvg), full(lnvb), full(wmix), full(bmix), full(gnb), full(tri), full(place),
                any_spec, any_spec, any_spec, any_spec]
    return pl.pallas_call(
        functools.partial(_in_proj_kernel, npt, tiles_per_seq),
        grid=(n_tiles,),
        in_specs=in_specs,
        out_specs=out_specs,
        out_shape=out_shapes,
        scratch_shapes=[pltpu.VMEM((1, LANES), F32)],
        input_output_aliases={10: 1, 11: 2, 12: 3, 13: 4},
        compiler_params=_cparams("arbitrary"),
        name="in_proj",
    )(x, w, bf, lnvg, lnvb, wmix, bmix, gnb, tri, place, kp, vp, ks, vs)


def _prompt_tile(qa_ref, ka, vt_prev, key_off, masked, shifted, m_ref, acc_ref, p_ref, alpha_ref):
    tk, tq = ka.shape[1], qa_ref.shape[1]
    nt = (((1,), (1,)), ((), ()))
    if masked:
        rel = lax.broadcasted_iota(jnp.int32, (tk, tq), 0) - lax.broadcasted_iota(jnp.int32, (tk, tq), 1)
        keep = rel <= key_off
    _prompt_pv(vt_prev, p_ref, alpha_ref if shifted else None, acc_ref)
    for h in range(PROMPT_HEADS):
        s = lax.dot_general(ka[h], qa_ref[h], nt, preferred_element_type=F32)
        if masked:
            s = jnp.where(keep, s, NEG_BIG)
        if not shifted:
            p_ref[h] = jnp.exp2(s).astype(BF16)
            continue
        m_prev = m_ref[h]
        m_new = jnp.maximum(m_prev, jnp.max(s, axis=0, keepdims=True))
        m_ref[h] = m_new
        p_ref[h] = jnp.exp2(s - m_new[0:1, :]).astype(BF16)
        alpha_ref[h] = jnp.exp2(m_prev - m_new)


def _prompt_pv(vt, p_ref, alpha_ref, acc_ref):
    for h in range(PROMPT_HEADS):
        pv = jnp.dot(vt[h], p_ref[h], preferred_element_type=F32)
        acc_ref[h] = acc_ref[h] + pv if alpha_ref is None else alpha_ref[h, 0:1, :] * acc_ref[h] + pv


def _fox_prompt_kernel(shifted, qi_tab, ki_tab, first_tab, last_tab, q_ref, ka_ref, vt_ref, dq_ref, a_in,
                       o_ref, bad_ref, qa_ref, m_ref, acc_ref, p_ref, alpha_ref, vtp_ref):
    del a_in
    t = pl.program_id(2)
    qi = qi_tab[t]
    ki = ki_tab[t]
    tq = q_ref.shape[0]
    tk = ka_ref.shape[1]

    @pl.when(jnp.logical_and(t == 0, jnp.logical_and(pl.program_id(0) == 0, pl.program_id(1) == 0)))
    def _():
        p_ref[...] = jnp.zeros_like(p_ref)

    @pl.when(first_tab[t] == 1)
    def _():
        m_ref[...] = jnp.full_like(m_ref, -jnp.inf)
        acc_ref[...] = jnp.zeros_like(acc_ref)
        alpha_ref[...] = jnp.zeros_like(alpha_ref)
        vtp_ref[...] = jnp.zeros_like(vtp_ref)
        lane = lax.broadcasted_iota(jnp.int32, (tq, LANES), 1)
        for pp in range(PROMPT_HEADS // 2):
            dqc = jnp.concatenate([dq_ref[pp], jnp.zeros((LANES - SUBLANES, tq), F32)], axis=0).T
            parts = [p.astype(F32) for p in _split3(dqc)]
            q2 = q_ref[:, pp * LANES:(pp + 1) * LANES].astype(F32)
            for h in range(2):
                base = DH_A if h == 0 else 0
                own = (lane < DH_A) if h == 0 else (lane >= DH_A)
                extra = jnp.where(jnp.logical_and(lane >= base + 3, lane < base + 6), 1.0, 0.0)
                for c in range(3):
                    extra = jnp.where(lane == base + c, parts[c][:, h:h + 1], extra)
                qa_ref[2 * pp + h] = jnp.where(own, q2, extra).astype(BF16)

    key_off = qi * tq - ki * tk
    unmasked = (ki + 1) * tk <= qi * tq

    @pl.when(unmasked)
    def _():
        _prompt_tile(qa_ref, ka_ref[...], vtp_ref[...], key_off, False, shifted, m_ref, acc_ref, p_ref, alpha_ref)
        vtp_ref[...] = vt_ref[...]

    @pl.when(jnp.logical_not(unmasked))
    def _():
        _prompt_tile(qa_ref, ka_ref[...], vtp_ref[...], key_off, True, shifted, m_ref, acc_ref, p_ref, alpha_ref)
        vtp_ref[...] = vt_ref[...]

    @pl.when(last_tab[t] == 1)
    def _():
        _prompt_pv(vt_ref[...], p_ref, alpha_ref if shifted else None, acc_ref)
        bad = jnp.zeros((1, tq), F32)
        for h in range(PROMPT_HEADS):
            acc = acc_ref[h]
            bad = jnp.maximum(bad, jnp.max(jnp.where(acc - acc == 0.0, 0.0, 1.0), axis=0, keepdims=True))
            bad = jnp.maximum(bad, jnp.where(acc[DH_A:DH_A + 1, :] > 0.0, 0.0, 1.0))
        bad_ref[0] = jnp.broadcast_to(jnp.max(bad, axis=1, keepdims=True), (SUBLANES, LANES))
        for pp in range(PROMPT_HEADS // 2):
            out_t = jnp.concatenate([acc_ref[2 * pp + h, 0:DH_A, :] / acc_ref[2 * pp + h, DH_A:DH_A + 1, :]
                                     for h in range(2)], axis=0)
            o_ref[:, pp * LANES:(pp + 1) * LANES] = out_t.T.astype(o_ref.dtype)


def _fox_prompt_call(shifted, q, kaug, vta, dpair, batch, seq, tq, tk):
    t_rows = q.shape[0]
    nq, nk = seq // tq, seq // tk
    kv_per_q = tq // tk
    pairs = [(a, b) for a in range(nq) for b in range((a + 1) * kv_per_q)]
    n_steps = len(pairs)
    qi_tab = jnp.asarray([a for a, _ in pairs], jnp.int32)
    ki_tab = jnp.asarray([b for _, b in pairs], jnp.int32)
    first_tab = jnp.asarray([int(i == 0 or pairs[i - 1][0] != pairs[i][0]) for i in range(n_steps)], jnp.int32)
    last_tab = jnp.asarray([int(i == n_steps - 1 or pairs[i + 1][0] != pairs[i][0]) for i in range(n_steps)],
                           jnp.int32)
    nh = PROMPT_HEADS
    groups = H_A // nh
    wq = nh * DH_A
    grid_spec = pltpu.PrefetchScalarGridSpec(
        num_scalar_prefetch=4,
        grid=(batch, groups, n_steps),
        in_specs=[
            pl.BlockSpec((tq, wq), lambda b, g, t, qt, kt, ft, lt: (b * nq + qt[t], g)),
            pl.BlockSpec((nh, tk, LANES), lambda b, g, t, qt, kt, ft, lt: (g, b * nk + kt[t], 0)),
            pl.BlockSpec((nh, LANES, tk), lambda b, g, t, qt, kt, ft, lt: (g, 0, b * nk + kt[t])),
            pl.BlockSpec((nh // 2, SUBLANES, tq), lambda b, g, t, qt, kt, ft, lt: (g, 0, b * nq + qt[t])),
            pl.BlockSpec(memory_space=pl.ANY),
        ],
        out_specs=(pl.BlockSpec((tq, wq), lambda b, g, t, qt, kt, ft, lt: (b * nq + qt[t], g)),
                   pl.BlockSpec((1, SUBLANES, LANES),
                                lambda b, g, t, qt, kt, ft, lt: ((b * groups + g) * nq + qt[t], 0, 0))),
        scratch_shapes=[pltpu.VMEM((nh, tq, LANES), BF16),
                        pltpu.VMEM((nh, SUBLANES, tq), F32),
                        pltpu.VMEM((nh, LANES, tq), F32),
                        pltpu.VMEM((nh, tk, tq), BF16), pltpu.VMEM((nh, SUBLANES, tq), F32),
                        pltpu.VMEM((nh, LANES, tk), BF16)],
    )
    return pl.pallas_call(
        functools.partial(_fox_prompt_kernel, shifted),
        grid_spec=grid_spec,
        out_shape=(jax.ShapeDtypeStruct((t_rows, D_A), BF16),
                   jax.ShapeDtypeStruct((batch * groups * nq, SUBLANES, LANES), F32)),
        input_output_aliases={8: 0},
        compiler_params=_cparams("arbitrary", "arbitrary", "arbitrary"),
        name="fox_prompt_shifted" if shifted else "fox_prompt",
    )(qi_tab, ki_tab, first_tab, last_tab, q, kaug, vta, dpair, jnp.zeros((t_rows, D_A), BF16))


def _fox_prompt(q, kaug, vta, dpair, batch, seq, tq, tk):
    args = (q, kaug, vta, dpair, batch, seq, tq, tk)
    a_fast, bad = _fox_prompt_call(False, *args)
    return lax.cond(jnp.max(bad) > 0.0, lambda: _fox_prompt_call(True, *args)[0], lambda: a_fast)


def _sample_tile(shifted, q2, k2, v2, bias_a, bias_b, mask, m_ref, l_ref, acc_ref):
    tq = q2.shape[0]
    low = lax.broadcasted_iota(jnp.int32, (tq, LANES), 1) < DH_A
    zero = jnp.zeros_like(q2)
    nt = (((1,), (1,)), ((), ()))
    pvs, alphas = [], []
    for h, (qh, bias) in enumerate(((jnp.where(low, q2, zero), bias_a), (jnp.where(low, zero, q2), bias_b))):
        s = lax.dot_general(qh, k2, nt, preferred_element_type=F32) + bias
        if mask is not None:
            s = jnp.where(mask, s, NEG_BIG)
        if not shifted:
            p = jnp.exp2(s)
            l_ref[h] = l_ref[h] + jnp.sum(p, axis=1, keepdims=True)
            pvs.append(jnp.dot(p.astype(BF16), v2, preferred_element_type=F32))
            continue
        m_prev = m_ref[h]
        m_new = jnp.maximum(m_prev, jnp.max(s, axis=1, keepdims=True))
        alpha = jnp.exp2(m_prev - m_new)
        p = jnp.exp2(s - m_new[:, 0:1])
        l_ref[h] = alpha * l_ref[h] + jnp.sum(p, axis=1, keepdims=True)
        m_ref[h] = m_new
        pvs.append(jnp.dot(p.astype(BF16), v2, preferred_element_type=F32))
        alphas.append(alpha)
    if shifted:
        acc_ref[...] = jnp.where(low, alphas[0], alphas[1]) * acc_ref[...] + jnp.where(low, pvs[0], pvs[1])
    else:
        acc_ref[...] = acc_ref[...] + jnp.where(low, pvs[0], pvs[1])


def _fox_sample_kernel(shifted, n_cache_tiles, q_ref, ck_ref, cv_ref, r_ref, nk_ref, nv_ref, dqc_ref, dqr_ref,
                       a_in, o_ref, bad_ref, m_ref, l_ref, acc_ref):
    del a_in
    s_idx = pl.program_id(1)
    tq = q_ref.shape[0]

    @pl.when(s_idx == 0)
    def _():
        m_ref[...] = jnp.full_like(m_ref, -jnp.inf)
        l_ref[...] = jnp.zeros_like(l_ref)
        acc_ref[...] = jnp.zeros_like(acc_ref)

    dq = dqc_ref[...]

    def pair_refs(p):
        cols = pl.ds(p * LANES, LANES)
        return m_ref.at[p], l_ref.at[p], acc_ref.at[:, cols]

    @pl.when(s_idx < n_cache_tiles)
    def _():
        r = r_ref[0, 0]
        for p in range(HEAD_PAIRS):
            cols = slice(p * LANES, (p + 1) * LANES)
            bias_a = dq[:, 2 * p:2 * p + 1] + r[2 * p:2 * p + 1, :]
            bias_b = dq[:, 2 * p + 1:2 * p + 2] + r[2 * p + 1:2 * p + 2, :]
            _sample_tile(shifted, q_ref[:, cols], ck_ref[0, 0, :, cols].astype(BF16),
                         cv_ref[0, 0, :, cols].astype(BF16), bias_a, bias_b, None, *pair_refs(p))

    @pl.when(s_idx == n_cache_tiles)
    def _():
        dk = dqr_ref[0]
        causal = lax.broadcasted_iota(jnp.int32, (tq, tq), 0) >= lax.broadcasted_iota(jnp.int32, (tq, tq), 1)
        for p in range(HEAD_PAIRS):
            cols = slice(p * LANES, (p + 1) * LANES)
            bias_a = dq[:, 2 * p:2 * p + 1] - dk[2 * p:2 * p + 1, :]
            bias_b = dq[:, 2 * p + 1:2 * p + 2] - dk[2 * p + 1:2 * p + 2, :]
            _sample_tile(shifted, q_ref[:, cols], nk_ref[0, :, cols].astype(BF16), nv_ref[0, :, cols].astype(BF16),
                         bias_a, bias_b, causal, *pair_refs(p))
        low = lax.broadcasted_iota(jnp.int32, (tq, LANES), 1) < DH_A
        bad = jnp.zeros((tq, LANES), F32)
        for p in range(HEAD_PAIRS):
            cols = slice(p * LANES, (p + 1) * LANES)
            acc = acc_ref[:, cols]
            den = jnp.where(low, l_ref[p, 0], l_ref[p, 1])
            bad = jnp.maximum(bad, jnp.where(jnp.logical_and(acc - acc == 0.0, den > 0.0), 0.0, 1.0))
            o_ref[:, cols] = (acc / den).astype(o_ref.dtype)
        bad = jnp.max(jnp.max(bad, axis=0, keepdims=True), axis=1, keepdims=True)
        bad_ref[0] = jnp.broadcast_to(bad, (SUBLANES, LANES))


def _fox_sample_call(shifted, layer, q, cache_k, cache_v, rsuf, ks, vs, dcol, drow_s, a_buf, n_prompt, tile):
    depth, dec_batch, past, _ = cache_k.shape
    dec_seq = ks.shape[1] // dec_batch
    nct = past // tile
    q0 = n_prompt // dec_seq
    ci = lambda b, s: (layer, b, jnp.minimum(s, nct - 1), 0)
    return pl.pallas_call(
        functools.partial(_fox_sample_kernel, shifted, nct),
        grid=(dec_batch, nct + 1),
        in_specs=[
            pl.BlockSpec((dec_seq, D_A), lambda b, s: (q0 + b, 0)),
            pl.BlockSpec((1, 1, tile, D_A), ci),
            pl.BlockSpec((1, 1, tile, D_A), ci),
            pl.BlockSpec((1, 1, H_A, tile), lambda b, s: (layer, b, 0, jnp.minimum(s, nct - 1))),
            pl.BlockSpec((1, dec_seq, D_A), lambda b, s: (layer, b, 0)),
            pl.BlockSpec((1, dec_seq, D_A), lambda b, s: (layer, b, 0)),
            pl.BlockSpec((dec_seq, H_A), lambda b, s: (q0 + b, 0)),
            pl.BlockSpec((1, H_A, dec_seq), lambda b, s: (b, 0, 0)),
            pl.BlockSpec(memory_space=pl.ANY),
        ],
        out_specs=(pl.BlockSpec((dec_seq, D_A), lambda b, s: (q0 + b, 0)),
                   pl.BlockSpec((1, SUBLANES, LANES), lambda b, s: (b, 0, 0))),
        out_shape=(jax.ShapeDtypeStruct(a_buf.shape, a_buf.dtype),
                   jax.ShapeDtypeStruct((dec_batch, SUBLANES, LANES), F32)),
        scratch_shapes=[pltpu.VMEM((HEAD_PAIRS, 2, dec_seq, LANES), F32),
                        pltpu.VMEM((HEAD_PAIRS, 2, dec_seq, LANES), F32),
                        pltpu.VMEM((dec_seq, D_A), F32)],
        input_output_aliases={8: 0},
        compiler_params=_cparams("parallel", "arbitrary"),
        name="fox_sample_shifted" if shifted else "fox_sample",
    )(q, cache_k, cache_v, rsuf, ks, vs, dcol, drow_s, a_buf)


def _fox_sample(*args):
    a_fast, bad = _fox_sample_call(False, *args)
    return lax.cond(jnp.max(bad) > 0.0, lambda: _fox_sample_call(True, *args)[0], lambda: a_fast)


def _suffix_sum_kernel(x_ref, u_ref, o_ref, carry_ref):
    j = pl.program_id(0)

    @pl.when(j == 0)
    def _():
        carry_ref[...] = jnp.zeros_like(carry_ref)

    x = x_ref[...]
    hi, mid, lo = _split3(x)
    u = u_ref[...]
    loc = (jnp.dot(hi, u, preferred_element_type=F32) + jnp.dot(mid, u, preferred_element_type=F32)
           + jnp.dot(lo, u, preferred_element_type=F32))
    o_ref[...] = (loc + carry_ref[:, 0:1]) * LOG2E
    carry_ref[...] = carry_ref[...] + jnp.sum(x, axis=1, keepdims=True)


def _suffix_sums(x):
    rows, n = x.shape
    tb = min(512, n)
    nb = n // tb
    u = (lax.broadcasted_iota(jnp.int32, (tb, tb), 0) > lax.broadcasted_iota(jnp.int32, (tb, tb), 1)).astype(BF16)
    return pl.pallas_call(
        _suffix_sum_kernel,
        grid=(nb,),
        in_specs=[pl.BlockSpec((rows, tb), lambda j: (0, nb - 1 - j)),
                  pl.BlockSpec((tb, tb), lambda j: (0, 0))],
        out_specs=pl.BlockSpec((rows, tb), lambda j: (0, nb - 1 - j)),
        out_shape=jax.ShapeDtypeStruct((rows, n), F32),
        scratch_shapes=[pltpu.VMEM((rows, LANES), F32)],
        compiler_params=_cparams("arbitrary"),
        name="cache_suffix_sums",
    )(x, u)


def _out_proj_kernel(alpha, a_ref, bn_ref, x_ref, gna_ref, wo_ref, g1_ref, b1_ref, wrh_ref, wrl_ref, br_ref, ls_ref,
                     x1_ref, x1b_ref, route_ref, counts_ref, carry_ref):
    i = pl.program_id(0)
    tm = x_ref.shape[0]

    @pl.when(i == 0)
    def _():
        carry_ref[...] = jnp.zeros_like(carry_ref)

    an = _rms_norm(a_ref[...].astype(F32), gna_ref[...]).astype(BF16)
    mix = (jnp.dot(an, wo_ref[0:D_A, :], preferred_element_type=F32)
           + jnp.dot(bn_ref[...], wo_ref[D_A:, :], preferred_element_type=F32))
    x1 = _layer_norm(alpha * x_ref[...] + mix, g1_ref[...], b1_ref[...])
    x1_ref[...] = x1
    x1h = x1.astype(BF16)
    x1b_ref[...] = _pack_halves(x1)

    x1l = (x1 - x1h.astype(F32)).astype(BF16)
    logits = (jnp.dot(x1h, wrh_ref[...], preferred_element_type=F32)
              + jnp.dot(x1l, wrh_ref[...], preferred_element_type=F32)
              + jnp.dot(x1h, wrl_ref[...], preferred_element_type=F32)) + br_ref[...]
    lane = lax.broadcasted_iota(jnp.int32, (tm, LANES), 1)
    is_group = lane < N_GROUPS
    gl = jnp.where(is_group, logits, NEG_BIG)
    gmax = jnp.max(gl, axis=1, keepdims=True)
    g_idx = jnp.min(jnp.where(gl == gmax, lane, LANES), axis=1, keepdims=True)
    g_prob = 1.0 / jnp.sum(jnp.exp(gl - gmax), axis=1, keepdims=True)
    in_group = jnp.logical_and(lane >= N_GROUPS, lane < N_GROUPS + N_EXPERTS)
    in_group = jnp.logical_and(in_group, lax.shift_right_arithmetic(lane - N_GROUPS, 3) == g_idx)
    el = jnp.where(in_group, logits, NEG_BIG)
    e1 = jnp.max(el, axis=1, keepdims=True)
    i1 = jnp.min(jnp.where(el == e1, lane, LANES), axis=1, keepdims=True)
    el2 = jnp.where(lane == i1, NEG_BIG, el)
    e2 = jnp.max(el2, axis=1, keepdims=True)
    i2 = jnp.min(jnp.where(el2 == e2, lane, LANES), axis=1, keepdims=True)
    t2 = jnp.exp(e2 - e1)
    w1 = 1.0 / (1.0 + t2)
    gate1 = g_prob * w1
    gate2 = g_prob * (t2 * w1)

    oh1 = lane == i1
    oh2 = lane == i2
    both = (oh1.astype(F32) + oh2.astype(F32)).astype(BF16)
    before = jnp.dot(ls_ref[...], both, preferred_element_type=F32) + carry_ref[...]
    rank1 = jnp.sum(jnp.where(oh1, before, 0.0), axis=1, keepdims=True)
    rank2 = jnp.sum(jnp.where(oh2, before, 0.0), axis=1, keepdims=True)
    carry_ref[...] = carry_ref[...] + jnp.sum(both.astype(F32), axis=0, keepdims=True)
    counts_ref[...] = carry_ref[...]

    route = jnp.where(lane == 0, (i1 - N_GROUPS).astype(F32), 0.0)
    route = jnp.where(lane == 1, (i2 - N_GROUPS).astype(F32), route)
    route = jnp.where(lane == 2, rank1, route)
    route = jnp.where(lane == 3, rank2, route)
    route = jnp.where(lane == 4, gate1, route)
    route = jnp.where(lane == 5, gate2, route)
    route_ref[...] = route


def _out_proj(alpha, a, bn, x, gna, wo, g1, b1, wrh, wrl, br, lstrict):
    t_rows = x.shape[0]
    tm = TOKEN_TILE
    full = lambda arr: pl.BlockSpec(arr.shape, lambda i: (0,) * arr.ndim)
    row = lambda w: pl.BlockSpec((tm, w), lambda i: (i, 0))
    return pl.pallas_call(
        functools.partial(_out_proj_kernel, alpha),
        grid=(t_rows // tm,),
        in_specs=[row(D_A), row(D_B), row(D_MODEL), full(gna), full(wo), full(g1), full(b1), full(wrh), full(wrl),
                  full(br), full(lstrict)],
        out_specs=(row(D_MODEL), row(HALF_MODEL), row(LANES), pl.BlockSpec((1, LANES), lambda i: (0, 0))),
        out_shape=(jax.ShapeDtypeStruct((t_rows, D_MODEL), F32),
                   jax.ShapeDtypeStruct((t_rows, HALF_MODEL), jnp.int32),
                   jax.ShapeDtypeStruct((t_rows, LANES), F32),
                   jax.ShapeDtypeStruct((1, LANES), F32)),
        scratch_shapes=[pltpu.VMEM((1, LANES), F32)],
        compiler_params=_cparams("arbitrary"),
        name="out_proj_router",
    )(a, bn, x, gna, wo, g1, b1, wrh, wrl, br, lstrict)


def _pack_halves(x):
    lo = lax.bitcast_convert_type(x[:, :HALF_MODEL].astype(BF16).astype(F32), jnp.uint32)
    hi = lax.bitcast_convert_type(x[:, HALF_MODEL:].astype(BF16).astype(F32), jnp.uint32)
    return lax.bitcast_convert_type((hi & jnp.uint32(0xFFFF0000)) | (lo >> 16), jnp.int32)


def _unpack_halves(w):
    u = lax.bitcast_convert_type(w, jnp.uint32)
    return (lax.bitcast_convert_type(u << 16, F32), lax.bitcast_convert_type(u & jnp.uint32(0xFFFF0000), F32))


def _expert_kernel(ib_ref, ie_ref, ni_ref, st_ref, en_ref, x_ref, w1_ref, w3_ref, w2_ref, o_ref,
                   w1b_ref, w3b_ref, w2b_ref):
    i = pl.program_id(0)
    bm = x_ref.shape[0]

    @pl.when(i < ni_ref[0])
    def _():
        e = ie_ref[i]
        b = ib_ref[i]
        prev = jnp.maximum(i - 1, 0)
        first_item = i == 0

        @pl.when(jnp.logical_or(first_item, ie_ref[prev] != e))
        def _():
            w1b_ref[...] = w1_ref[0, 0].astype(BF16)
            w3b_ref[...] = w3_ref[0, 0].astype(BF16)
            w2b_ref[...] = w2_ref[0, 0].astype(BF16)

        x_lo, x_hi = (v.astype(BF16) for v in _unpack_halves(x_ref[...]))
        h1 = (jnp.dot(x_lo, w1b_ref[:HALF_MODEL, :], preferred_element_type=F32)
              + jnp.dot(x_hi, w1b_ref[HALF_MODEL:, :], preferred_element_type=F32))
        h3 = (jnp.dot(x_lo, w3b_ref[:HALF_MODEL, :], preferred_element_type=F32)
              + jnp.dot(x_hi, w3b_ref[HALF_MODEL:, :], preferred_element_type=F32))
        h = (h1 * (1.0 / (1.0 + jnp.exp(-h1))) * h3).astype(BF16)
        y = _pack_halves(jnp.dot(h, w2b_ref[...], preferred_element_type=F32))
        first_visit = jnp.logical_or(first_item, ib_ref[prev] != b)

        @pl.when(first_visit)
        def _():
            o_ref[...] = y

        @pl.when(jnp.logical_not(first_visit))
        def _():
            row = b * bm + lax.broadcasted_iota(jnp.int32, (bm, 1), 0)
            mine = jnp.logical_and(row >= st_ref[e], row < en_ref[e])
            o_ref[...] = jnp.where(mine, y, o_ref[...])


def _experts(layer, item_block, item_expert, n_items, starts, ends, xs, w1, w3, w2):
    n_rows = xs.shape[0]
    bm = MOE_ROWS
    n_max = item_block.shape[0]
    item = lambda i, ni: jnp.minimum(i, ni[0] - 1)
    blk = lambda i, ib, ie, ni, st, en: (ib[item(i, ni)], 0)
    wsel = lambda i, ib, ie, ni, st, en: (layer, ie[item(i, ni)], 0, 0)
    grid_spec = pltpu.PrefetchScalarGridSpec(
        num_scalar_prefetch=5,
        grid=(n_max,),
        in_specs=[pl.BlockSpec((bm, HALF_MODEL), blk),
                  pl.BlockSpec((1, 1, D_MODEL, D_EXPERT), wsel),
                  pl.BlockSpec((1, 1, D_MODEL, D_EXPERT), wsel),
                  pl.BlockSpec((1, 1, D_EXPERT, D_MODEL), wsel)],
        out_specs=pl.BlockSpec((bm, HALF_MODEL), blk),
        scratch_shapes=[pltpu.VMEM((D_MODEL, D_EXPERT), BF16), pltpu.VMEM((D_MODEL, D_EXPERT), BF16),
                        pltpu.VMEM((D_EXPERT, D_MODEL), BF16)],
    )
    return pl.pallas_call(
        _expert_kernel,
        grid_spec=grid_spec,
        out_shape=jax.ShapeDtypeStruct((n_rows, HALF_MODEL), jnp.int32),
        compiler_params=_cparams("arbitrary"),
        name="experts",
    )(item_block, item_expert, n_items, starts, ends, xs, w1, w3, w2)


def _combine_kernel(alpha, n_first, x1_ref, y0_ref, y1_ref, route_ref, g2_ref, b2_ref, *o_refs):
    route = route_ref[...]
    g0, g1 = route[:, 4:5], route[:, 5:6]
    y0_lo, y0_hi = _unpack_halves(y0_ref[...])
    y1_lo, y1_hi = _unpack_halves(y1_ref[...])
    z = jnp.concatenate([alpha * x1_ref[:, :HALF_MODEL] + (g0 * y0_lo + g1 * y1_lo),
                         alpha * x1_ref[:, HALF_MODEL:] + (g0 * y0_hi + g1 * y1_hi)], axis=1)
    out = _layer_norm(z, g2_ref[...], b2_ref[...])
    if n_first is None:
        o_refs[0][...] = out
    else:
        first = pl.program_id(0) < n_first

        @pl.when(first)
        def _():
            o_refs[0][...] = out

        @pl.when(jnp.logical_not(first))
        def _():
            o_refs[1][...] = out


def _combine(alpha, x1, y01, route, g2, b2, split_rows=None):
    t_rows = x1.shape[0]
    tm = TOKEN_TILE
    n_tiles = t_rows // tm
    full = lambda arr: pl.BlockSpec(arr.shape, lambda i: (0,) * arr.ndim)
    row = lambda w: pl.BlockSpec((tm, w), lambda i: (i, 0))
    if split_rows is None:
        n_first = None
        out_specs = row(D_MODEL)
        out_shape = jax.ShapeDtypeStruct((t_rows, D_MODEL), F32)
    else:
        n_first = split_rows // tm
        out_specs = (pl.BlockSpec((tm, D_MODEL), lambda i: (jnp.minimum(i, n_first - 1), 0)),
                     pl.BlockSpec((tm, D_MODEL), lambda i: (jnp.maximum(i - n_first, 0), 0)))
        out_shape = (jax.ShapeDtypeStruct((split_rows, D_MODEL), F32),
                     jax.ShapeDtypeStruct((t_rows - split_rows, D_MODEL), F32))
    return pl.pallas_call(
        functools.partial(_combine_kernel, alpha, n_first),
        grid=(n_tiles,),
        in_specs=[row(D_MODEL), row(HALF_MODEL), pl.BlockSpec((tm, HALF_MODEL), lambda i: (i + n_tiles, 0)),
                  row(LANES), full(g2), full(b2)],
        out_specs=out_specs,
        out_shape=out_shape,
        compiler_params=_cparams("arbitrary"),
        name="moe_combine",
    )(x1, y01, y01, route, g2, b2)


def _gather_rows(table, idx):
    n_rows, width = idx.shape[0], table.shape[1]
    workers = SC_CORES * SC_SUBCORES
    step = workers * SC_GATHER_ROWS
    n_pad = -(-n_rows // step) * step
    if n_pad != n_rows:
        idx = jnp.pad(idx, (0, n_pad - n_rows))
    per_worker = n_pad // workers
    n_chunks = per_worker // SC_GATHER_ROWS
    mesh = plsc.VectorSubcoreMesh(core_axis_name="c", subcore_axis_name="s")

    @functools.partial(
        pl.kernel, mesh=mesh,
        out_type=jax.ShapeDtypeStruct((n_pad, width), table.dtype),
        scratch_types=[pltpu.VMEM((SC_GATHER_ROWS,), jnp.int32),
                       pltpu.VMEM((SC_GATHER_ROWS, width), table.dtype),
                       pltpu.SemaphoreType.DMA],
        name="sc_gather_rows",
    )
    def gather(table_hbm, idx_hbm, out_hbm, idx_v, rows_v, sem):
        base = (lax.axis_index("s") * SC_CORES + lax.axis_index("c")) * per_worker

        @pl.loop(0, n_chunks)
        def _(c):
            off = base + c * SC_GATHER_ROWS
            pltpu.sync_copy(idx_hbm.at[pl.ds(off, SC_GATHER_ROWS)], idx_v)
            pltpu.async_copy(table_hbm.at[idx_v], rows_v, sem).wait()
            pltpu.sync_copy(rows_v, out_hbm.at[pl.ds(off, SC_GATHER_ROWS)])

    out = gather(table, idx)
    return out if n_pad == n_rows else out[:n_rows]


def _attention_tile(seq):
    for t in (512, 256, 128):
        if seq % t == 0:
            return t
    raise ValueError("sequence length must be a multiple of 128")


def kernel(x_prompt, x_sample, cache_k, cache_v, cache_logf, w_in, b_f, ln_v_g, ln_v_b, w_s, b_s,
           g_norm_a, g_norm_b, w_out, ln1_g, ln1_b, w_gr, b_gr, w_er, b_er, w1, w3, w2, ln2_g, ln2_b):
    batch, seq, _ = x_prompt.shape
    dec_batch, dec_seq, _ = x_sample.shape
    depth = w_in.shape[0]
    past = cache_k.shape[2]
    n_prompt = batch * seq
    n_sample = dec_batch * dec_seq
    t_rows = n_prompt + n_sample
    alpha = float((2 * depth) ** 0.25)
    tm = TOKEN_TILE
    assert seq % tm == 0 and n_sample % tm == 0 and tm % dec_seq == 0 and dec_seq == GMLP_CHUNK // 2
    assert past % 128 == 0

    x = jnp.concatenate([x_prompt.reshape(n_prompt, D_MODEL), x_sample.reshape(n_sample, D_MODEL)], axis=0)

    sp = (D_A, 2 * D_A, 3 * D_A, 3 * D_A + H_A, 3 * D_A + H_A + D_B)
    wq, wk, wv, wf, wu, wgv = (w_in[..., a:b] for a, b in zip((0,) + sp, sp + (w_in.shape[-1],)))
    wf_pad = jnp.pad(wf, ((0, 0), (0, 0), (0, LANES - H_A)))
    w_cat = jnp.concatenate([wq * (LOG2E * DH_A ** -0.5), wk, wv, wu, wgv, wf_pad], axis=-1).astype(BF16)
    bf_pad = jnp.pad(b_f, ((0, 0), (0, LANES - H_A)))[:, None, :]
    half = GMLP_CHUNK // 2
    wmix = jnp.stack([w_s, jnp.tile(w_s[:, :, :half, :half], (1, 1, 2, 2))], axis=1)
    bs_t = jnp.swapaxes(b_s, 1, 2)
    bs_var = jnp.stack([bs_t, jnp.tile(bs_t[:, :half], (1, 2, 1))], axis=1)
    bmix = jnp.repeat(bs_var, C_B, axis=-1)
    wo_b = w_out.astype(BF16)
    wr = jnp.pad(jnp.concatenate([w_gr, w_er], axis=-1), ((0, 0), (0, 0), (0, LANES - N_GROUPS - N_EXPERTS)))
    wrh = wr.astype(BF16)
    wrl = (wr - wrh.astype(F32)).astype(BF16)
    br = jnp.pad(jnp.concatenate([b_gr, b_er], axis=-1), ((0, 0), (0, LANES - N_GROUPS - N_EXPERTS)))[:, None, :]
    row2 = lambda a: a[:, None, :]

    ri = lax.broadcasted_iota(jnp.int32, (tm, tm), 0)
    ci = lax.broadcasted_iota(jnp.int32, (tm, tm), 1)
    tri = jnp.stack([ri >= ci, jnp.logical_and(ri >= ci, ri // dec_seq == ci // dec_seq)]).astype(BF16)
    lstrict = (ri > ci).astype(BF16)
    prow = lax.broadcasted_iota(jnp.int32, (LANES, H_A * LANES), 0)
    pcol = lax.broadcasted_iota(jnp.int32, (LANES, H_A * LANES), 1)
    phead = pcol // LANES
    poff = pcol % LANES - jnp.where(phead % 2 == 0, DH_A, 0)
    is_one = jnp.logical_and(prow == 3 * H_A, jnp.logical_and(poff >= 0, poff < 3))
    is_part = jnp.logical_and(jnp.logical_and(poff >= 3, poff < 6), prow == (poff - 3) * H_A + phead)
    place = jnp.logical_or(is_one, is_part).astype(BF16)

    clf = jnp.transpose(cache_logf, (0, 1, 3, 2)).reshape(depth * dec_batch * H_A, past)
    rsuf = _suffix_sums(clf).reshape(depth, dec_batch, H_A, past)
    ck = cache_k.reshape(depth, dec_batch, past, D_A)
    cv = cache_v.reshape(depth, dec_batch, past, D_A)

    kp = jnp.zeros((depth, n_prompt, D_A), F32)
    vp = jnp.zeros((depth, n_prompt, D_A), F32)
    ks = jnp.zeros((depth, n_sample, D_A), F32)
    vs = jnp.zeros((depth, n_sample, D_A), F32)
    logfs, gvns = [], []

    bm = MOE_ROWS
    n_assign = 2 * t_rows
    nblk = n_assign // bm
    expert_ids = jnp.arange(N_EXPERTS, dtype=jnp.int32)
    item_ids = jnp.arange(nblk + N_EXPERTS - 1, dtype=jnp.int32)
    q_tile = PROMPT_Q_TILE if seq % PROMPT_Q_TILE == 0 else _attention_tile(seq)
    k_tile = PROMPT_K_TILE if q_tile % PROMPT_K_TILE == 0 else q_tile
    c_tile = SAMPLE_K_TILE if past % SAMPLE_K_TILE == 0 else _attention_tile(past)
    assign_ids = jnp.arange(n_assign, dtype=jnp.int32)

    for l in range(depth):
        q, kp, vp, ks, vs, kaug, vta, logf, dT, dcol, bn, gvn = _in_proj(
            l, x, w_cat[l], bf_pad[l], row2(ln_v_g)[l], row2(ln_v_b)[l], wmix[l], bmix[l], row2(g_norm_b)[l], tri,
            place, kp, vp, ks, vs, n_prompt, seq)
        logfs.append(logf)
        gvns.append(gvn)

        dpair = jnp.pad(dT.reshape(HEAD_PAIRS, 2, t_rows), ((0, 0), (0, SUBLANES - 2), (0, 0)))
        a = _fox_prompt(q, kaug, vta, dpair, batch, seq, q_tile, k_tile)
        drow_s = jnp.transpose(dT[:, n_prompt:].reshape(H_A, dec_batch, dec_seq), (1, 0, 2))
        a = _fox_sample(l, q, ck, cv, rsuf, ks, vs, dcol, drow_s, a, n_prompt, c_tile)

        x1, x1b, route, counts = _out_proj(alpha, a, bn, x, row2(g_norm_a)[l], wo_b[l], row2(ln1_g)[l],
                                           row2(ln1_b)[l], wrh[l], wrl[l], br[l], lstrict)

        cnt = counts[0, N_GROUPS:N_GROUPS + N_EXPERTS].astype(jnp.int32)
        ends = jnp.cumsum(cnt)
        starts = ends - cnt
        eid = route[:, 0:2].astype(jnp.int32)
        rank = route[:, 2:4].astype(jnp.int32)
        onehot = eid[:, :, None] == expert_ids[None, None, :]
        pos = jnp.sum(jnp.where(onehot, starts[None, None, :], 0), axis=-1) + rank
        order = jnp.sort(eid.reshape(n_assign) * n_assign + assign_ids) % n_assign
        first_blk = starts // bm
        n_it = jnp.where(cnt > 0, (ends - 1) // bm - first_blk + 1, 0)
        it_end = jnp.cumsum(n_it)
        item_expert = jnp.minimum(jnp.sum(item_ids[:, None] >= it_end[None, :], axis=1), N_EXPERTS - 1).astype(jnp.int32)
        it_first = jnp.sum(jnp.where(item_expert[:, None] == expert_ids[None, :], (it_end - n_it)[None, :], 0), axis=1)
        it_blk0 = jnp.sum(jnp.where(item_expert[:, None] == expert_ids[None, :], first_blk[None, :], 0), axis=1)
        item_block = jnp.clip(it_blk0 + item_ids - it_first, 0, nblk - 1).astype(jnp.int32)

        xs = _gather_rows(x1b, order // 2)
        yb = _experts(l, item_block, item_expert, it_end[-1:].astype(jnp.int32), starts, ends, xs, w1, w3, w2)
        y01 = _gather_rows(yb, jnp.concatenate([pos[:, 0], pos[:, 1]]))
        x = _combine(alpha, x1, y01, route, row2(ln2_g)[l], row2(ln2_b)[l],
                     split_rows=n_prompt if l == depth - 1 else None)

    y_prompt = x[0].reshape(batch, seq, D_MODEL)
    y_sample = x[1].reshape(dec_batch, dec_seq, D_MODEL)
    logf_all = jnp.stack(logfs)
    return (y_prompt, y_sample,
            kp.reshape(depth, batch, seq, H_A, DH_A), vp.reshape(depth, batch, seq, H_A, DH_A),
            logf_all[:, :n_prompt].reshape(depth, batch, seq, H_A),
            ks.reshape(depth, dec_batch, dec_seq, H_A, DH_A), vs.reshape(depth, dec_batch, dec_seq, H_A, DH_A),
            logf_all[:, n_prompt:].reshape(depth, dec_batch, dec_seq, H_A),
            jnp.stack(gvns).reshape(depth, dec_batch, dec_seq, D_B))
```

```python
import functools

import jax
import jax.numpy as jnp
from jax import lax
from jax.experimental import pallas as pl
from jax.experimental.pallas import tpu as pltpu
from jax.experimental.pallas import tpu_sc as plsc

F32 = jnp.float32
BF16 = jnp.bfloat16

D_MODEL = 1024
D_A = 512
H_A = 8
DH_A = 64
D_B = 512
G_B = 8
C_B = 64
GMLP_CHUNK = 128
N_GROUPS = 4
EXPERTS_PER_GROUP = 8
N_EXPERTS = N_GROUPS * EXPERTS_PER_GROUP
D_EXPERT = 512
LN_EPS = 1e-5
HEAD_PAIRS = H_A // 2
HALF_MODEL = D_MODEL // 2

LANES = 128
SUBLANES = 8
VMEM_LIMIT_BYTES = 56 * 1024 * 1024
SC_CORES = 2
SC_SUBCORES = 16
SC_GATHER_ROWS = 128

TOKEN_TILE = 512
MOE_ROWS = 512
PROMPT_Q_TILE = 1024
PROMPT_K_TILE = 1024
SAMPLE_K_TILE = 2048
PROMPT_HEADS = 4
V_ROWS = DH_A + 16
NEG_BIG = -1e30
LOG2E = 1.4426950408889634


def _cparams(*sem):
    return pltpu.CompilerParams(dimension_semantics=sem, vmem_limit_bytes=VMEM_LIMIT_BYTES)


def _split3(x):
    hi = x.astype(BF16)
    r1 = x - hi.astype(F32)
    mid = r1.astype(BF16)
    lo = (r1 - mid.astype(F32)).astype(BF16)
    return hi, mid, lo


def _gelu_tanh(x):
    return 0.5 * x * (1.0 + jnp.tanh(0.7978845608028654 * (x + 0.044715 * (x * x * x))))


def _log_sigmoid(z):
    return jnp.minimum(z, 0.0) - jnp.log(1.0 + jnp.exp(-jnp.abs(z)))


def _layer_norm(x, g, b):
    mu = jnp.mean(x, axis=-1, keepdims=True)
    xc = x - mu
    var = jnp.mean(xc * xc, axis=-1, keepdims=True)
    return xc * lax.rsqrt(var + LN_EPS) * g + b


def _rms_norm(x, g):
    return x * lax.rsqrt(jnp.mean(x * x, axis=-1, keepdims=True) + LN_EPS) * g


def _in_proj_kernel(n_prompt_tiles, tiles_per_seq,
                    x_ref, w_ref, bf_ref, lnvg_ref, lnvb_ref, wmix_ref, bmix_ref, gnb_ref, tri_ref, place_ref,
                    kp_in, vp_in, ks_in, vs_in,
                    q_ref, kp_ref, vp_ref, ks_ref, vs_ref, kaug_ref, vta_ref, logf_ref, dT_ref, dcol_ref, bn_ref,
                    gvn_ref, carry_ref):
    del kp_in, vp_in, ks_in, vs_in
    i = pl.program_id(0)
    tm = x_ref.shape[0]
    is_sample = i >= n_prompt_tiles
    var = is_sample.astype(jnp.int32)

    p = jnp.dot(x_ref[...].astype(BF16), w_ref[...], preferred_element_type=F32)
    q = p[:, 0:D_A]
    k = p[:, D_A:2 * D_A]
    v = p[:, 2 * D_A:3 * D_A]
    u = p[:, 3 * D_A:3 * D_A + D_B]
    gv = p[:, 3 * D_A + D_B:3 * D_A + 2 * D_B]
    fl = p[:, 3 * D_A + 2 * D_B:]

    q_ref[...] = q.astype(BF16)
    tail = jnp.where(lax.broadcasted_iota(jnp.int32, (V_ROWS - DH_A, tm), 0) == 0, 1.0, 0.0)
    for j in range(HEAD_PAIRS):
        vt_pair = v[:, j * LANES:(j + 1) * LANES].T
        vta_ref[2 * j] = jnp.concatenate([vt_pair[0:DH_A], tail], axis=0).astype(BF16)
        vta_ref[2 * j + 1] = jnp.concatenate([vt_pair[DH_A:], tail], axis=0).astype(BF16)

    ks_ref[0] = k
    vs_ref[0] = v

    logf = _log_sigmoid(fl + bf_ref[...])
    logf_ref[...] = logf[:, 0:H_A]
    hi, mid, lo = _split3(logf)
    parts = jnp.concatenate([hi, mid, lo], axis=1)
    cs = jnp.dot(tri_ref[var], parts, preferred_element_type=F32)
    cs = cs[:, 0:LANES] + cs[:, LANES:2 * LANES] + cs[:, 2 * LANES:3 * LANES]

    restart = jnp.logical_or(is_sample, i % tiles_per_seq == 0)
    d = cs + jnp.where(restart, 0.0, carry_ref[...])
    carry_ref[...] = d[tm - 1:tm, :]
    d2 = d * LOG2E
    dcol_ref[...] = d2[:, 0:H_A]
    dT_ref[...] = d2.T[0:H_A, :]

    nh, nm, nl = _split3(-d2)
    lane_t = lax.broadcasted_iota(jnp.int32, (tm, LANES), 1)
    dparts = jnp.where(lane_t < H_A, nh,
                       jnp.where(lane_t < 2 * H_A, pltpu.roll(nm, H_A, 1),
                                 jnp.where(lane_t < 3 * H_A, pltpu.roll(nl, 2 * H_A, 1),
                                           jnp.where(lane_t == 3 * H_A, 1.0, 0.0).astype(BF16))))
    aug = jnp.dot(dparts, place_ref[...], preferred_element_type=F32)
    low_t = lane_t < DH_A
    for h in range(H_A):
        kh = k[:, (h // 2) * LANES:(h // 2 + 1) * LANES]
        own = low_t if h % 2 == 0 else jnp.logical_not(low_t)
        kaug_ref[h] = jnp.where(own, kh, aug[:, h * LANES:(h + 1) * LANES]).astype(BF16)

    ug = _gelu_tanh(u)
    vn = _layer_norm(_gelu_tanh(gv), lnvg_ref[...], lnvb_ref[...])

    gvn_ref[...] = vn

    r_io = lax.broadcasted_iota(jnp.int32, (GMLP_CHUNK, GMLP_CHUNK), 0)
    c_io = lax.broadcasted_iota(jnp.int32, (GMLP_CHUNK, GMLP_CHUNK), 1)
    causal = r_io >= c_io
    half = GMLP_CHUNK // 2
    same_half = (r_io >= half) == (c_io >= half)
    keep = jnp.logical_and(causal, jnp.logical_or(jnp.logical_not(is_sample), same_half))
    lane = lax.broadcasted_iota(jnp.int32, (GMLP_CHUNK, LANES), 1)
    low_lanes = lane < C_B
    vnb = vn.astype(BF16)
    bias = bmix_ref[var]
    mixed_rows = []
    for r in range(tm // GMLP_CHUNK):
        rows = slice(r * GMLP_CHUNK, (r + 1) * GMLP_CHUNK)
        cols_out = []
        for j in range(G_B // 2):
            vj = vnb[rows, j * LANES:(j + 1) * LANES]
            m0 = jnp.where(keep, wmix_ref[var, 2 * j], 0.0).astype(BF16)
            m1 = jnp.where(keep, wmix_ref[var, 2 * j + 1], 0.0).astype(BF16)
            y0 = jnp.dot(m0, vj, preferred_element_type=F32)
            y1 = jnp.dot(m1, vj, preferred_element_type=F32)
            cols_out.append(jnp.where(low_lanes, y0, y1))
        mixed_rows.append(jnp.concatenate(cols_out, axis=1) + bias)
    mixed = jnp.concatenate(mixed_rows, axis=0)
    b_out = ug * mixed
    bn_ref[...] = _rms_norm(b_out, gnb_ref[...]).astype(BF16)

    @pl.when(jnp.logical_not(is_sample))
    def _():
        kp_ref[0] = k
        vp_ref[0] = v


def _in_proj(layer, x, w, bf, lnvg, lnvb, wmix, bmix, gnb, tri, place, kp, vp, ks, vs, n_prompt, seq):
    t_rows = x.shape[0]
    tm = TOKEN_TILE
    n_tiles = t_rows // tm
    npt = n_prompt // tm
    n_sample = t_rows - n_prompt
    tiles_per_seq = seq // tm
    full = lambda a: pl.BlockSpec(a.shape, lambda i: (0,) * a.ndim)
    any_spec = pl.BlockSpec(memory_space=pl.ANY)
    p_idx = lambda i: (layer, jnp.minimum(i, npt - 1), 0)
    s_idx = lambda i: (layer, jnp.maximum(i - npt, 0), 0)
    out_shapes = (
        jax.ShapeDtypeStruct((t_rows, D_A), BF16),
        jax.ShapeDtypeStruct(kp.shape, F32),
        jax.ShapeDtypeStruct(vp.shape, F32),
        jax.ShapeDtypeStruct(ks.shape, F32),
        jax.ShapeDtypeStruct(vs.shape, F32),
        jax.ShapeDtypeStruct((H_A, t_rows, LANES), BF16),
        jax.ShapeDtypeStruct((H_A, V_ROWS, t_rows), BF16),
        jax.ShapeDtypeStruct((t_rows, H_A), F32),
        jax.ShapeDtypeStruct((H_A, t_rows), F32),
        jax.ShapeDtypeStruct((t_rows, H_A), F32),
        jax.ShapeDtypeStruct((t_rows, D_B), BF16),
        jax.ShapeDtypeStruct((n_sample, D_B), F32),
    )
    out_specs = (
        pl.BlockSpec((tm, D_A), lambda i: (i, 0)),
        pl.BlockSpec((1, tm, D_A), p_idx),
        pl.BlockSpec((1, tm, D_A), p_idx),
        pl.BlockSpec((1, tm, D_A), s_idx),
        pl.BlockSpec((1, tm, D_A), s_idx),
        pl.BlockSpec((H_A, tm, LANES), lambda i: (0, i, 0)),
        pl.BlockSpec((H_A, V_ROWS, tm), lambda i: (0, 0, i)),
        pl.BlockSpec((tm, H_A), lambda i: (i, 0)),
        pl.BlockSpec((H_A, tm), lambda i: (0, i)),
        pl.BlockSpec((tm, H_A), lambda i: (i, 0)),
        pl.BlockSpec((tm, D_B), lambda i: (i, 0)),
        pl.BlockSpec((tm, D_B), lambda i: (jnp.maximum(i - npt, 0), 0)),
    )
    in_specs = [pl.BlockSpec((tm, D_MODEL), lambda i: (i, 0)),
                full(w), full(bf), full(lnvg), full(lnvb), full(wmix), full(bmix), full(gnb), full(tri), full(place),
                any_spec, any_spec, any_spec, any_spec]
    return pl.pallas_call(
        functools.partial(_in_proj_kernel, npt, tiles_per_seq),
        grid=(n_tiles,),
        in_specs=in_specs,
        out_specs=out_specs,
        out_shape=out_shapes,
        scratch_shapes=[pltpu.VMEM((1, LANES), F32)],
        input_output_aliases={10: 1, 11: 2, 12: 3, 13: 4},
        compiler_params=_cparams("arbitrary"),
        name="in_proj",
    )(x, w, bf, lnvg, lnvb, wmix, bmix, gnb, tri, place, kp, vp, ks, vs)


def _prompt_tile(qa_ref, ka, vt_prev, key_off, masked, shifted, m_ref, acc_ref, p_ref, alpha_ref):
    tk, tq = ka.shape[1], qa_ref.shape[1]
    nt = (((1,), (1,)), ((), ()))
    if masked:
        rel = lax.broadcasted_iota(jnp.int32, (tk, tq), 0) - lax.broadcasted_iota(jnp.int32, (tk, tq), 1)
        keep = rel <= key_off
    _prompt_pv(vt_prev, p_ref, alpha_ref if shifted else None, acc_ref)
    for h in range(PROMPT_HEADS):
        s = lax.dot_general(ka[h], qa_ref[h], nt, preferred_element_type=F32)
        if masked:
            s = jnp.where(keep, s, NEG_BIG)
        if not shifted:
            p_ref[h] = jnp.exp2(s).astype(BF16)
            continue
        m_prev = m_ref[h]
        m_new = jnp.maximum(m_prev, jnp.max(s, axis=0, keepdims=True))
        m_ref[h] = m_new
        p_ref[h] = jnp.exp2(s - m_new[0:1, :]).astype(BF16)
        alpha_ref[h] = jnp.exp2(m_prev - m_new)


def _prompt_pv(vt, p_ref, alpha_ref, acc_ref):
    for h in range(PROMPT_HEADS):
        pv = jnp.dot(vt[h], p_ref[h], preferred_element_type=F32)
        acc_ref[h] = acc_ref[h] + pv if alpha_ref is None else alpha_ref[h, 0:1, :] * acc_ref[h] + pv


def _fox_prompt_kernel(shifted, qi_tab, ki_tab, first_tab, last_tab, q_ref, ka_ref, vt_ref, dq_ref, a_in,
                       o_ref, bad_ref, qa_ref, m_ref, acc_ref, p_ref, alpha_ref, vtp_ref):
    del a_in
    t = pl.program_id(2)
    qi = qi_tab[t]
    ki = ki_tab[t]
    tq = q_ref.shape[0]
    tk = ka_ref.shape[1]

    @pl.when(jnp.logical_and(t == 0, jnp.logical_and(pl.program_id(0) == 0, pl.program_id(1) == 0)))
    def _():
        p_ref[...] = jnp.zeros_like(p_ref)

    @pl.when(first_tab[t] == 1)
    def _():
        m_ref[...] = jnp.full_like(m_ref, -jnp.inf)
        acc_ref[...] = jnp.zeros_like(acc_ref)
        alpha_ref[...] = jnp.zeros_like(alpha_ref)
        vtp_ref[...] = jnp.zeros_like(vtp_ref)
        lane = lax.broadcasted_iota(jnp.int32, (tq, LANES), 1)
        for pp in range(PROMPT_HEADS // 2):
            dqc = jnp.concatenate([dq_ref[pp], jnp.zeros((LANES - SUBLANES, tq), F32)], axis=0).T
            parts = [p.astype(F32) for p in _split3(dqc)]
            q2 = q_ref[:, pp * LANES:(pp + 1) * LANES].astype(F32)
            for h in range(2):
                base = DH_A if h == 0 else 0
                own = (lane < DH_A) if h == 0 else (lane >= DH_A)
                extra = jnp.where(jnp.logical_and(lane >= base + 3, lane < base + 6), 1.0, 0.0)
                for c in range(3):
                    extra = jnp.where(lane == base + c, parts[c][:, h:h + 1], extra)
                qa_ref[2 * pp + h] = jnp.where(own, q2, extra).astype(BF16)

    key_off = qi * tq - ki * tk
    unmasked = (ki + 1) * tk <= qi * tq

    @pl.when(unmasked)
    def _():
        _prompt_tile(qa_ref, ka_ref[...], vtp_ref[...], key_off, False, shifted, m_ref, acc_ref, p_ref, alpha_ref)
        vtp_ref[...] = vt_ref[...]

    @pl.when(jnp.logical_not(unmasked))
    def _():
        _prompt_tile(qa_ref, ka_ref[...], vtp_ref[...], key_off, True, shifted, m_ref, acc_ref, p_ref, alpha_ref)
        vtp_ref[...] = vt_ref[...]

    @pl.when(last_tab[t] == 1)
    def _():
        _prompt_pv(vt_ref[...], p_ref, alpha_ref if shifted else None, acc_ref)
        bad = jnp.zeros((1, tq), F32)
        for h in range(PROMPT_HEADS):
            acc = acc_ref[h]
            bad = jnp.maximum(bad, jnp.max(jnp.where(acc - acc == 0.0, 0.0, 1.0), axis=0, keepdims=True))
            bad = jnp.maximum(bad, jnp.where(acc[DH_A:DH_A + 1, :] > 0.0, 0.0, 1.0))
        bad_ref[0] = jnp.broadcast_to(jnp.max(bad, axis=1, keepdims=True), (SUBLANES, LANES))
        for pp in range(PROMPT_HEADS // 2):
            out_t = jnp.concatenate([acc_ref[2 * pp + h, 0:DH_A, :] / acc_ref[2 * pp + h, DH_A:DH_A + 1, :]
                                     for h in range(2)], axis=0)
            o_ref[:, pp * LANES:(pp + 1) * LANES] = out_t.T.astype(o_ref.dtype)


def _fox_prompt_call(shifted, q, kaug, vta, dpair, batch, seq, tq, tk):
    t_rows = q.shape[0]
    nq, nk = seq // tq, seq // tk
    kv_per_q = tq // tk
    pairs = [(a, b) for a in range(nq) for b in range((a + 1) * kv_per_q)]
    n_steps = len(pairs)
    qi_tab = jnp.asarray([a for a, _ in pairs], jnp.int32)
    ki_tab = jnp.asarray([b for _, b in pairs], jnp.int32)
    first_tab = jnp.asarray([int(i == 0 or pairs[i - 1][0] != pairs[i][0]) for i in range(n_steps)], jnp.int32)
    last_tab = jnp.asarray([int(i == n_steps - 1 or pairs[i + 1][0] != pairs[i][0]) for i in range(n_steps)],
                           jnp.int32)
    nh = PROMPT_HEADS
    groups = H_A // nh
    wq = nh * DH_A
    grid_spec = pltpu.PrefetchScalarGridSpec(
        num_scalar_prefetch=4,
        grid=(batch, groups, n_steps),
        in_specs=[
            pl.BlockSpec((tq, wq), lambda b, g, t, qt, kt, ft, lt: (b * nq + qt[t], g)),
            pl.BlockSpec((nh, tk, LANES), lambda b, g, t, qt, kt, ft, lt: (g, b * nk + kt[t], 0)),
            pl.BlockSpec((nh, V_ROWS, tk), lambda b, g, t, qt, kt, ft, lt: (g, 0, b * nk + kt[t])),
            pl.BlockSpec((nh // 2, SUBLANES, tq), lambda b, g, t, qt, kt, ft, lt: (g, 0, b * nq + qt[t])),
            pl.BlockSpec(memory_space=pl.ANY),
        ],
        out_specs=(pl.BlockSpec((tq, wq), lambda b, g, t, qt, kt, ft, lt: (b * nq + qt[t], g)),
                   pl.BlockSpec((1, SUBLANES, LANES),
                                lambda b, g, t, qt, kt, ft, lt: ((b * groups + g) * nq + qt[t], 0, 0))),
        scratch_shapes=[pltpu.VMEM((nh, tq, LANES), BF16),
                        pltpu.VMEM((nh, SUBLANES, tq), F32),
                        pltpu.VMEM((nh, V_ROWS, tq), F32),
                        pltpu.VMEM((nh, tk, tq), BF16), pltpu.VMEM((nh, SUBLANES, tq), F32),
                        pltpu.VMEM((nh, V_ROWS, tk), BF16)],
    )
    return pl.pallas_call(
        functools.partial(_fox_prompt_kernel, shifted),
        grid_spec=grid_spec,
        out_shape=(jax.ShapeDtypeStruct((t_rows, D_A), BF16),
                   jax.ShapeDtypeStruct((batch * groups * nq, SUBLANES, LANES), F32)),
        input_output_aliases={8: 0},
        compiler_params=_cparams("arbitrary", "arbitrary", "arbitrary"),
        name="fox_prompt_shifted" if shifted else "fox_prompt",
    )(qi_tab, ki_tab, first_tab, last_tab, q, kaug, vta, dpair, jnp.zeros((t_rows, D_A), BF16))


def _fox_prompt(q, kaug, vta, dpair, batch, seq, tq, tk):
    args = (q, kaug, vta, dpair, batch, seq, tq, tk)
    a_fast, bad = _fox_prompt_call(False, *args)
    return lax.cond(jnp.max(bad) > 0.0, lambda: _fox_prompt_call(True, *args)[0], lambda: a_fast)


def _sample_tile(shifted, q2, k2, v2, bias_a, bias_b, mask, m_ref, l_ref, acc_ref):
    tq = q2.shape[0]
    low = lax.broadcasted_iota(jnp.int32, (tq, LANES), 1) < DH_A
    zero = jnp.zeros_like(q2)
    nt = (((1,), (1,)), ((), ()))
    pvs, alphas = [], []
    for h, (qh, bias) in enumerate(((jnp.where(low, q2, zero), bias_a), (jnp.where(low, zero, q2), bias_b))):
        s = lax.dot_general(qh, k2, nt, preferred_element_type=F32) + bias
        if mask is not None:
            s = jnp.where(mask, s, NEG_BIG)
        if not shifted:
            p = jnp.exp2(s)
            l_ref[h] = l_ref[h] + jnp.sum(p, axis=1, keepdims=True)
            pvs.append(jnp.dot(p.astype(BF16), v2, preferred_element_type=F32))
            continue
        m_prev = m_ref[h]
        m_new = jnp.maximum(m_prev, jnp.max(s, axis=1, keepdims=True))
        alpha = jnp.exp2(m_prev - m_new)
        p = jnp.exp2(s - m_new[:, 0:1])
        l_ref[h] = alpha * l_ref[h] + jnp.sum(p, axis=1, keepdims=True)
        m_ref[h] = m_new
        pvs.append(jnp.dot(p.astype(BF16), v2, preferred_element_type=F32))
        alphas.append(alpha)
    if shifted:
        acc_ref[...] = jnp.where(low, alphas[0], alphas[1]) * acc_ref[...] + jnp.where(low, pvs[0], pvs[1])
    else:
        acc_ref[...] = acc_ref[...] + jnp.where(low, pvs[0], pvs[1])


def _fox_sample_kernel(shifted, n_cache_tiles, q_ref, ck_ref, cv_ref, r_ref, nk_ref, nv_ref, dqc_ref, dqr_ref,
                       a_in, o_ref, bad_ref, m_ref, l_ref, acc_ref):
    del a_in
    s_idx = pl.program_id(1)
    tq = q_ref.shape[0]

    @pl.when(s_idx == 0)
    def _():
        m_ref[...] = jnp.full_like(m_ref, -jnp.inf)
        l_ref[...] = jnp.zeros_like(l_ref)
        acc_ref[...] = jnp.zeros_like(acc_ref)

    dq = dqc_ref[...]

    def pair_refs(p):
        cols = pl.ds(p * LANES, LANES)
        return m_ref.at[p], l_ref.at[p], acc_ref.at[:, cols]

    @pl.when(s_idx < n_cache_tiles)
    def _():
        r = r_ref[0, 0]
        for p in range(HEAD_PAIRS):
            cols = slice(p * LANES, (p + 1) * LANES)
            bias_a = dq[:, 2 * p:2 * p + 1] + r[2 * p:2 * p + 1, :]
            bias_b = dq[:, 2 * p + 1:2 * p + 2] + r[2 * p + 1:2 * p + 2, :]
            _sample_tile(shifted, q_ref[:, cols], ck_ref[0, 0, :, cols].astype(BF16),
                         cv_ref[0, 0, :, cols].astype(BF16), bias_a, bias_b, None, *pair_refs(p))

    @pl.when(s_idx == n_cache_tiles)
    def _():
        dk = dqr_ref[0]
        causal = lax.broadcasted_iota(jnp.int32, (tq, tq), 0) >= lax.broadcasted_iota(jnp.int32, (tq, tq), 1)
        for p in range(HEAD_PAIRS):
            cols = slice(p * LANES, (p + 1) * LANES)
            bias_a = dq[:, 2 * p:2 * p + 1] - dk[2 * p:2 * p + 1, :]
            bias_b = dq[:, 2 * p + 1:2 * p + 2] - dk[2 * p + 1:2 * p + 2, :]
            _sample_tile(shifted, q_ref[:, cols], nk_ref[0, :, cols].astype(BF16), nv_ref[0, :, cols].astype(BF16),
                         bias_a, bias_b, causal, *pair_refs(p))
        low = lax.broadcasted_iota(jnp.int32, (tq, LANES), 1) < DH_A
        bad = jnp.zeros((tq, LANES), F32)
        for p in range(HEAD_PAIRS):
            cols = slice(p * LANES, (p + 1) * LANES)
            acc = acc_ref[:, cols]
            den = jnp.where(low, l_ref[p, 0], l_ref[p, 1])
            bad = jnp.maximum(bad, jnp.where(jnp.logical_and(acc - acc == 0.0, den > 0.0), 0.0, 1.0))
            o_ref[:, cols] = (acc / den).astype(o_ref.dtype)
        bad = jnp.max(jnp.max(bad, axis=0, keepdims=True), axis=1, keepdims=True)
        bad_ref[0] = jnp.broadcast_to(bad, (SUBLANES, LANES))


def _fox_sample_call(shifted, layer, q, cache_k, cache_v, rsuf, ks, vs, dcol, drow_s, a_buf, n_prompt, tile):
    depth, dec_batch, past, _ = cache_k.shape
    dec_seq = ks.shape[1] // dec_batch
    nct = past // tile
    q0 = n_prompt // dec_seq
    ci = lambda b, s: (layer, b, jnp.minimum(s, nct - 1), 0)
    return pl.pallas_call(
        functools.partial(_fox_sample_kernel, shifted, nct),
        grid=(dec_batch, nct + 1),
        in_specs=[
            pl.BlockSpec((dec_seq, D_A), lambda b, s: (q0 + b, 0)),
            pl.BlockSpec((1, 1, tile, D_A), ci),
            pl.BlockSpec((1, 1, tile, D_A), ci),
            pl.BlockSpec((1, 1, H_A, tile), lambda b, s: (layer, b, 0, jnp.minimum(s, nct - 1))),
            pl.BlockSpec((1, dec_seq, D_A), lambda b, s: (layer, b, 0)),
            pl.BlockSpec((1, dec_seq, D_A), lambda b, s: (layer, b, 0)),
            pl.BlockSpec((dec_seq, H_A), lambda b, s: (q0 + b, 0)),
            pl.BlockSpec((1, H_A, dec_seq), lambda b, s: (b, 0, 0)),
            pl.BlockSpec(memory_space=pl.ANY),
        ],
        out_specs=(pl.BlockSpec((dec_seq, D_A), lambda b, s: (q0 + b, 0)),
                   pl.BlockSpec((1, SUBLANES, LANES), lambda b, s: (b, 0, 0))),
        out_shape=(jax.ShapeDtypeStruct(a_buf.shape, a_buf.dtype),
                   jax.ShapeDtypeStruct((dec_batch, SUBLANES, LANES), F32)),
        scratch_shapes=[pltpu.VMEM((HEAD_PAIRS, 2, dec_seq, LANES), F32),
                        pltpu.VMEM((HEAD_PAIRS, 2, dec_seq, LANES), F32),
                        pltpu.VMEM((dec_seq, D_A), F32)],
        input_output_aliases={8: 0},
        compiler_params=_cparams("parallel", "arbitrary"),
        name="fox_sample_shifted" if shifted else "fox_sample",
    )(q, cache_k, cache_v, rsuf, ks, vs, dcol, drow_s, a_buf)


def _fox_sample(*args):
    a_fast, bad = _fox_sample_call(False, *args)
    return lax.cond(jnp.max(bad) > 0.0, lambda: _fox_sample_call(True, *args)[0], lambda: a_fast)


def _suffix_sum_kernel(x_ref, u_ref, o_ref, carry_ref):
    j = pl.program_id(0)

    @pl.when(j == 0)
    def _():
        carry_ref[...] = jnp.zeros_like(carry_ref)

    x = x_ref[...]
    hi, mid, lo = _split3(x)
    u = u_ref[...]
    loc = (jnp.dot(hi, u, preferred_element_type=F32) + jnp.dot(mid, u, preferred_element_type=F32)
           + jnp.dot(lo, u, preferred_element_type=F32))
    o_ref[...] = (loc + carry_ref[:, 0:1]) * LOG2E
    carry_ref[...] = carry_ref[...] + jnp.sum(x, axis=1, keepdims=True)


def _suffix_sums(x):
    rows, n = x.shape
    tb = min(512, n)
    nb = n // tb
    u = (lax.broadcasted_iota(jnp.int32, (tb, tb), 0) > lax.broadcasted_iota(jnp.int32, (tb, tb), 1)).astype(BF16)
    return pl.pallas_call(
        _suffix_sum_kernel,
        grid=(nb,),
        in_specs=[pl.BlockSpec((rows, tb), lambda j: (0, nb - 1 - j)),
                  pl.BlockSpec((tb, tb), lambda j: (0, 0))],
        out_specs=pl.BlockSpec((rows, tb), lambda j: (0, nb - 1 - j)),
        out_shape=jax.ShapeDtypeStruct((rows, n), F32),
        scratch_shapes=[pltpu.VMEM((rows, LANES), F32)],
        compiler_params=_cparams("arbitrary"),
        name="cache_suffix_sums",
    )(x, u)


def _out_proj_kernel(alpha, a_ref, bn_ref, x_ref, gna_ref, wo_ref, g1_ref, b1_ref, wrh_ref, wrl_ref, br_ref, ls_ref,
                     x1_ref, x1b_ref, route_ref, counts_ref, carry_ref):
    i = pl.program_id(0)
    tm = x_ref.shape[0]

    @pl.when(i == 0)
    def _():
        carry_ref[...] = jnp.zeros_like(carry_ref)

    an = _rms_norm(a_ref[...].astype(F32), gna_ref[...]).astype(BF16)
    mix = (jnp.dot(an, wo_ref[0:D_A, :], preferred_element_type=F32)
           + jnp.dot(bn_ref[...], wo_ref[D_A:, :], preferred_element_type=F32))
    x1 = _layer_norm(alpha * x_ref[...] + mix, g1_ref[...], b1_ref[...])
    x1_ref[...] = x1
    x1h = x1.astype(BF16)
    x1b_ref[...] = _pack_halves(x1)

    x1l = (x1 - x1h.astype(F32)).astype(BF16)
    logits = (jnp.dot(x1h, wrh_ref[...], preferred_element_type=F32)
              + jnp.dot(x1l, wrh_ref[...], preferred_element_type=F32)
              + jnp.dot(x1h, wrl_ref[...], preferred_element_type=F32)) + br_ref[...]
    lane = lax.broadcasted_iota(jnp.int32, (tm, LANES), 1)
    is_group = lane < N_GROUPS
    gl = jnp.where(is_group, logits, NEG_BIG)
    gmax = jnp.max(gl, axis=1, keepdims=True)
    g_idx = jnp.min(jnp.where(gl == gmax, lane, LANES), axis=1, keepdims=True)
    g_prob = 1.0 / jnp.sum(jnp.exp(gl - gmax), axis=1, keepdims=True)
    in_group = jnp.logical_and(lane >= N_GROUPS, lane < N_GROUPS + N_EXPERTS)
    in_group = jnp.logical_and(in_group, lax.shift_right_arithmetic(lane - N_GROUPS, 3) == g_idx)
    el = jnp.where(in_group, logits, NEG_BIG)
    e1 = jnp.max(el, axis=1, keepdims=True)
    i1 = jnp.min(jnp.where(el == e1, lane, LANES), axis=1, keepdims=True)
    el2 = jnp.where(lane == i1, NEG_BIG, el)
    e2 = jnp.max(el2, axis=1, keepdims=True)
    i2 = jnp.min(jnp.where(el2 == e2, lane, LANES), axis=1, keepdims=True)
    t2 = jnp.exp(e2 - e1)
    w1 = 1.0 / (1.0 + t2)
    gate1 = g_prob * w1
    gate2 = g_prob * (t2 * w1)

    oh1 = lane == i1
    oh2 = lane == i2
    both = (oh1.astype(F32) + oh2.astype(F32)).astype(BF16)
    before = jnp.dot(ls_ref[...], both, preferred_element_type=F32) + carry_ref[...]
    rank1 = jnp.sum(jnp.where(oh1, before, 0.0), axis=1, keepdims=True)
    rank2 = jnp.sum(jnp.where(oh2, before, 0.0), axis=1, keepdims=True)
    carry_ref[...] = carry_ref[...] + jnp.sum(both.astype(F32), axis=0, keepdims=True)
    counts_ref[...] = carry_ref[...]

    route = jnp.where(lane == 0, (i1 - N_GROUPS).astype(F32), 0.0)
    route = jnp.where(lane == 1, (i2 - N_GROUPS).astype(F32), route)
    route = jnp.where(lane == 2, rank1, route)
    route = jnp.where(lane == 3, rank2, route)
    route = jnp.where(lane == 4, gate1, route)
    route = jnp.where(lane == 5, gate2, route)
    route_ref[...] = route


def _out_proj(alpha, a, bn, x, gna, wo, g1, b1, wrh, wrl, br, lstrict):
    t_rows = x.shape[0]
    tm = TOKEN_TILE
    full = lambda arr: pl.BlockSpec(arr.shape, lambda i: (0,) * arr.ndim)
    row = lambda w: pl.BlockSpec((tm, w), lambda i: (i, 0))
    return pl.pallas_call(
        functools.partial(_out_proj_kernel, alpha),
        grid=(t_rows // tm,),
        in_specs=[row(D_A), row(D_B), row(D_MODEL), full(gna), full(wo), full(g1), full(b1), full(wrh), full(wrl),
                  full(br), full(lstrict)],
        out_specs=(row(D_MODEL), row(HALF_MODEL), row(LANES), pl.BlockSpec((1, LANES), lambda i: (0, 0))),
        out_shape=(jax.ShapeDtypeStruct((t_rows, D_MODEL), F32),
                   jax.ShapeDtypeStruct((t_rows, HALF_MODEL), jnp.int32),
                   jax.ShapeDtypeStruct((t_rows, LANES), F32),
                   jax.ShapeDtypeStruct((1, LANES), F32)),
        scratch_shapes=[pltpu.VMEM((1, LANES), F32)],
        compiler_params=_cparams("arbitrary"),
        name="out_proj_router",
    )(a, bn, x, gna, wo, g1, b1, wrh, wrl, br, lstrict)


def _pack_halves(x):
    lo = lax.bitcast_convert_type(x[:, :HALF_MODEL].astype(BF16).astype(F32), jnp.uint32)
    hi = lax.bitcast_convert_type(x[:, HALF_MODEL:].astype(BF16).astype(F32), jnp.uint32)
    return lax.bitcast_convert_type((hi & jnp.uint32(0xFFFF0000)) | (lo >> 16), jnp.int32)


def _unpack_halves(w):
    u = lax.bitcast_convert_type(w, jnp.uint32)
    return (lax.bitcast_convert_type(u << 16, F32), lax.bitcast_convert_type(u & jnp.uint32(0xFFFF0000), F32))


def _expert_kernel(ib_ref, ie_ref, ni_ref, st_ref, en_ref, x_ref, w1_ref, w3_ref, w2_ref, o_ref,
                   w1b_ref, w3b_ref, w2b_ref):
    i = pl.program_id(0)
    bm = x_ref.shape[0]

    @pl.when(i < ni_ref[0])
    def _():
        e = ie_ref[i]
        b = ib_ref[i]
        prev = jnp.maximum(i - 1, 0)
        first_item = i == 0

        @pl.when(jnp.logical_or(first_item, ie_ref[prev] != e))
        def _():
            w1b_ref[...] = w1_ref[0, 0].astype(BF16)
            w3b_ref[...] = w3_ref[0, 0].astype(BF16)
            w2b_ref[...] = w2_ref[0, 0].astype(BF16)

        x_lo, x_hi = (v.astype(BF16) for v in _unpack_halves(x_ref[...]))
        h1 = (jnp.dot(x_lo, w1b_ref[:HALF_MODEL, :], preferred_element_type=F32)
              + jnp.dot(x_hi, w1b_ref[HALF_MODEL:, :], preferred_element_type=F32))
        h3 = (jnp.dot(x_lo, w3b_ref[:HALF_MODEL, :], preferred_element_type=F32)
              + jnp.dot(x_hi, w3b_ref[HALF_MODEL:, :], preferred_element_type=F32))
        h = (h1 * (1.0 / (1.0 + jnp.exp(-h1))) * h3).astype(BF16)
        y = _pack_halves(jnp.dot(h, w2b_ref[...], preferred_element_type=F32))
        first_visit = jnp.logical_or(first_item, ib_ref[prev] != b)

        @pl.when(first_visit)
        def _():
            o_ref[...] = y

        @pl.when(jnp.logical_not(first_visit))
        def _():
            row = b * bm + lax.broadcasted_iota(jnp.int32, (bm, 1), 0)
            mine = jnp.logical_and(row >= st_ref[e], row < en_ref[e])
            o_ref[...] = jnp.where(mine, y, o_ref[...])


def _experts(layer, item_block, item_expert, n_items, starts, ends, xs, w1, w3, w2):
    n_rows = xs.shape[0]
    bm = MOE_ROWS
    n_max = item_block.shape[0]
    item = lambda i, ni: jnp.minimum(i, ni[0] - 1)
    blk = lambda i, ib, ie, ni, st, en: (ib[item(i, ni)], 0)
    wsel = lambda i, ib, ie, ni, st, en: (layer, ie[item(i, ni)], 0, 0)
    grid_spec = pltpu.PrefetchScalarGridSpec(
        num_scalar_prefetch=5,
        grid=(n_max,),
        in_specs=[pl.BlockSpec((bm, HALF_MODEL), blk),
                  pl.BlockSpec((1, 1, D_MODEL, D_EXPERT), wsel),
                  pl.BlockSpec((1, 1, D_MODEL, D_EXPERT), wsel),
                  pl.BlockSpec((1, 1, D_EXPERT, D_MODEL), wsel)],
        out_specs=pl.BlockSpec((bm, HALF_MODEL), blk),
        scratch_shapes=[pltpu.VMEM((D_MODEL, D_EXPERT), BF16), pltpu.VMEM((D_MODEL, D_EXPERT), BF16),
                        pltpu.VMEM((D_EXPERT, D_MODEL), BF16)],
    )
    return pl.pallas_call(
        _expert_kernel,
        grid_spec=grid_spec,
        out_shape=jax.ShapeDtypeStruct((n_rows, HALF_MODEL), jnp.int32),
        compiler_params=_cparams("arbitrary"),
        name="experts",
    )(item_block, item_expert, n_items, starts, ends, xs, w1, w3, w2)


def _combine_kernel(alpha, n_first, x1_ref, y0_ref, y1_ref, route_ref, g2_ref, b2_ref, *o_refs):
    route = route_ref[...]
    g0, g1 = route[:, 4:5], route[:, 5:6]
    y0_lo, y0_hi = _unpack_halves(y0_ref[...])
    y1_lo, y1_hi = _unpack_halves(y1_ref[...])
    z = jnp.concatenate([alpha * x1_ref[:, :HALF_MODEL] + (g0 * y0_lo + g1 * y1_lo),
                         alpha * x1_ref[:, HALF_MODEL:] + (g0 * y0_hi + g1 * y1_hi)], axis=1)
    out = _layer_norm(z, g2_ref[...], b2_ref[...])
    if n_first is None:
        o_refs[0][...] = out
    else:
        first = pl.program_id(0) < n_first

        @pl.when(first)
        def _():
            o_refs[0][...] = out

        @pl.when(jnp.logical_not(first))
        def _():
            o_refs[1][...] = out


def _combine(alpha, x1, y01, route, g2, b2, split_rows=None):
    t_rows = x1.shape[0]
    tm = TOKEN_TILE
    n_tiles = t_rows // tm
    full = lambda arr: pl.BlockSpec(arr.shape, lambda i: (0,) * arr.ndim)
    row = lambda w: pl.BlockSpec((tm, w), lambda i: (i, 0))
    if split_rows is None:
        n_first = None
        out_specs = row(D_MODEL)
        out_shape = jax.ShapeDtypeStruct((t_rows, D_MODEL), F32)
    else:
        n_first = split_rows // tm
        out_specs = (pl.BlockSpec((tm, D_MODEL), lambda i: (jnp.minimum(i, n_first - 1), 0)),
                     pl.BlockSpec((tm, D_MODEL), lambda i: (jnp.maximum(i - n_first, 0), 0)))
        out_shape = (jax.ShapeDtypeStruct((split_rows, D_MODEL), F32),
                     jax.ShapeDtypeStruct((t_rows - split_rows, D_MODEL), F32))
    return pl.pallas_call(
        functools.partial(_combine_kernel, alpha, n_first),
        grid=(n_tiles,),
        in_specs=[row(D_MODEL), row(HALF_MODEL), pl.BlockSpec((tm, HALF_MODEL), lambda i: (i + n_tiles, 0)),
                  row(LANES), full(g2), full(b2)],
        out_specs=out_specs,
        out_shape=out_shape,
        compiler_params=_cparams("arbitrary"),
        name="moe_combine",
    )(x1, y01, y01, route, g2, b2)


def _gather_rows(table, idx):
    n_rows, width = idx.shape[0], table.shape[1]
    workers = SC_CORES * SC_SUBCORES
    step = workers * SC_GATHER_ROWS
    n_pad = -(-n_rows // step) * step
    if n_pad != n_rows:
        idx = jnp.pad(idx, (0, n_pad - n_rows))
    per_worker = n_pad // workers
    n_chunks = per_worker // SC_GATHER_ROWS
    mesh = plsc.VectorSubcoreMesh(core_axis_name="c", subcore_axis_name="s")

    @functools.partial(
        pl.kernel, mesh=mesh,
        out_type=jax.ShapeDtypeStruct((n_pad, width), table.dtype),
        scratch_types=[pltpu.VMEM((SC_GATHER_ROWS,), jnp.int32),
                       pltpu.VMEM((SC_GATHER_ROWS, width), table.dtype),
                       pltpu.SemaphoreType.DMA],
        name="sc_gather_rows",
    )
    def gather(table_hbm, idx_hbm, out_hbm, idx_v, rows_v, sem):
        base = (lax.axis_index("s") * SC_CORES + lax.axis_index("c")) * per_worker

        @pl.loop(0, n_chunks)
        def _(c):
            off = base + c * SC_GATHER_ROWS
            pltpu.sync_copy(idx_hbm.at[pl.ds(off, SC_GATHER_ROWS)], idx_v)
            pltpu.async_copy(table_hbm.at[idx_v], rows_v, sem).wait()
            pltpu.sync_copy(rows_v, out_hbm.at[pl.ds(off, SC_GATHER_ROWS)])

    out = gather(table, idx)
    return out if n_pad == n_rows else out[:n_rows]


def _attention_tile(seq):
    for t in (512, 256, 128):
        if seq % t == 0:
            return t
    raise ValueError("sequence length must be a multiple of 128")


def kernel(x_prompt, x_sample, cache_k, cache_v, cache_logf, w_in, b_f, ln_v_g, ln_v_b, w_s, b_s,
           g_norm_a, g_norm_b, w_out, ln1_g, ln1_b, w_gr, b_gr, w_er, b_er, w1, w3, w2, ln2_g, ln2_b):
    batch, seq, _ = x_prompt.shape
    dec_batch, dec_seq, _ = x_sample.shape
    depth = w_in.shape[0]
    past = cache_k.shape[2]
    n_prompt = batch * seq
    n_sample = dec_batch * dec_seq
    t_rows = n_prompt + n_sample
    alpha = float((2 * depth) ** 0.25)
    tm = TOKEN_TILE
    assert seq % tm == 0 and n_sample % tm == 0 and tm % dec_seq == 0 and dec_seq == GMLP_CHUNK // 2
    assert past % 128 == 0

    x = jnp.concatenate([x_prompt.reshape(n_prompt, D_MODEL), x_sample.reshape(n_sample, D_MODEL)], axis=0)

    sp = (D_A, 2 * D_A, 3 * D_A, 3 * D_A + H_A, 3 * D_A + H_A + D_B)
    wq, wk, wv, wf, wu, wgv = (w_in[..., a:b] for a, b in zip((0,) + sp, sp + (w_in.shape[-1],)))
    wf_pad = jnp.pad(wf, ((0, 0), (0, 0), (0, LANES - H_A)))
    w_cat = jnp.concatenate([wq * (LOG2E * DH_A ** -0.5), wk, wv, wu, wgv, wf_pad], axis=-1).astype(BF16)
    bf_pad = jnp.pad(b_f, ((0, 0), (0, LANES - H_A)))[:, None, :]
    half = GMLP_CHUNK // 2
    wmix = jnp.stack([w_s, jnp.tile(w_s[:, :, :half, :half], (1, 1, 2, 2))], axis=1)
    bs_t = jnp.swapaxes(b_s, 1, 2)
    bs_var = jnp.stack([bs_t, jnp.tile(bs_t[:, :half], (1, 2, 1))], axis=1)
    bmix = jnp.repeat(bs_var, C_B, axis=-1)
    wo_b = w_out.astype(BF16)
    wr = jnp.pad(jnp.concatenate([w_gr, w_er], axis=-1), ((0, 0), (0, 0), (0, LANES - N_GROUPS - N_EXPERTS)))
    wrh = wr.astype(BF16)
    wrl = (wr - wrh.astype(F32)).astype(BF16)
    br = jnp.pad(jnp.concatenate([b_gr, b_er], axis=-1), ((0, 0), (0, LANES - N_GROUPS - N_EXPERTS)))[:, None, :]
    row2 = lambda a: a[:, None, :]

    ri = lax.broadcasted_iota(jnp.int32, (tm, tm), 0)
    ci = lax.broadcasted_iota(jnp.int32, (tm, tm), 1)
    tri = jnp.stack([ri >= ci, jnp.logical_and(ri >= ci, ri // dec_seq == ci // dec_seq)]).astype(BF16)
    lstrict = (ri > ci).astype(BF16)
    prow = lax.broadcasted_iota(jnp.int32, (LANES, H_A * LANES), 0)
    pcol = lax.broadcasted_iota(jnp.int32, (LANES, H_A * LANES), 1)
    phead = pcol // LANES
    poff = pcol % LANES - jnp.where(phead % 2 == 0, DH_A, 0)
    is_one = jnp.logical_and(prow == 3 * H_A, jnp.logical_and(poff >= 0, poff < 3))
    is_part = jnp.logical_and(jnp.logical_and(poff >= 3, poff < 6), prow == (poff - 3) * H_A + phead)
    place = jnp.logical_or(is_one, is_part).astype(BF16)

    clf = jnp.transpose(cache_logf, (0, 1, 3, 2)).reshape(depth * dec_batch * H_A, past)
    rsuf = _suffix_sums(clf).reshape(depth, dec_batch, H_A, past)
    ck = cache_k.reshape(depth, dec_batch, past, D_A)
    cv = cache_v.reshape(depth, dec_batch, past, D_A)

    kp = jnp.zeros((depth, n_prompt, D_A), F32)
    vp = jnp.zeros((depth, n_prompt, D_A), F32)
    ks = jnp.zeros((depth, n_sample, D_A), F32)
    vs = jnp.zeros((depth, n_sample, D_A), F32)
    logfs, gvns = [], []

    bm = MOE_ROWS
    n_assign = 2 * t_rows
    nblk = n_assign // bm
    expert_ids = jnp.arange(N_EXPERTS, dtype=jnp.int32)
    item_ids = jnp.arange(nblk + N_EXPERTS - 1, dtype=jnp.int32)
    q_tile = PROMPT_Q_TILE if seq % PROMPT_Q_TILE == 0 else _attention_tile(seq)
    k_tile = PROMPT_K_TILE if q_tile % PROMPT_K_TILE == 0 else q_tile
    c_tile = SAMPLE_K_TILE if past % SAMPLE_K_TILE == 0 else _attention_tile(past)
    assign_ids = jnp.arange(n_assign, dtype=jnp.int32)

    for l in range(depth):
        q, kp, vp, ks, vs, kaug, vta, logf, dT, dcol, bn, gvn = _in_proj(
            l, x, w_cat[l], bf_pad[l], row2(ln_v_g)[l], row2(ln_v_b)[l], wmix[l], bmix[l], row2(g_norm_b)[l], tri,
            place, kp, vp, ks, vs, n_prompt, seq)
        logfs.append(logf)
        gvns.append(gvn)

        dpair = jnp.pad(dT.reshape(HEAD_PAIRS, 2, t_rows), ((0, 0), (0, SUBLANES - 2), (0, 0)))
        a = _fox_prompt(q, kaug, vta, dpair, batch, seq, q_tile, k_tile)
        drow_s = jnp.transpose(dT[:, n_prompt:].reshape(H_A, dec_batch, dec_seq), (1, 0, 2))
        a = _fox_sample(l, q, ck, cv, rsuf, ks, vs, dcol, drow_s, a, n_prompt, c_tile)

        x1, x1b, route, counts = _out_proj(alpha, a, bn, x, row2(g_norm_a)[l], wo_b[l], row2(ln1_g)[l],
                                           row2(ln1_b)[l], wrh[l], wrl[l], br[l], lstrict)

        cnt = counts[0, N_GROUPS:N_GROUPS + N_EXPERTS].astype(jnp.int32)
        ends = jnp.cumsum(cnt)
        starts = ends - cnt
        eid = route[:, 0:2].astype(jnp.int32)
        rank = route[:, 2:4].astype(jnp.int32)
        onehot = eid[:, :, None] == expert_ids[None, None, :]
        pos = jnp.sum(jnp.where(onehot, starts[None, None, :], 0), axis=-1) + rank
        order = jnp.sort(eid.reshape(n_assign) * n_assign + assign_ids) % n_assign
        first_blk = starts // bm
        n_it = jnp.where(cnt > 0, (ends - 1) // bm - first_blk + 1, 0)
        it_end = jnp.cumsum(n_it)
        item_expert = jnp.minimum(jnp.sum(item_ids[:, None] >= it_end[None, :], axis=1), N_EXPERTS - 1).astype(jnp.int32)
        it_first = jnp.sum(jnp.where(item_expert[:, None] == expert_ids[None, :], (it_end - n_it)[None, :], 0), axis=1)
        it_blk0 = jnp.sum(jnp.where(item_expert[:, None] == expert_ids[None, :], first_blk[None, :], 0), axis=1)
        item_block = jnp.clip(it_blk0 + item_ids - it_first, 0, nblk - 1).astype(jnp.int32)

        xs = _gather_rows(x1b, order // 2)
        yb = _experts(l, item_block, item_expert, it_end[-1:].astype(jnp.int32), starts, ends, xs, w1, w3, w2)
        y01 = _gather_rows(yb, jnp.concatenate([pos[:, 0], pos[:, 1]]))
        x = _combine(alpha, x1, y01, route, row2(ln2_g)[l], row2(ln2_b)[l],
                     split_rows=n_prompt if l == depth - 1 else None)

    y_prompt = x[0].reshape(batch, seq, D_MODEL)
    y_sample = x[1].reshape(dec_batch, dec_seq, D_MODEL)
    logf_all = jnp.stack(logfs)
    return (y_prompt, y_sample,
            kp.reshape(depth, batch, seq, H_A, DH_A), vp.reshape(depth, batch, seq, H_A, DH_A),
            logf_all[:, :n_prompt].reshape(depth, batch, seq, H_A),
            ks.reshape(depth, dec_batch, dec_seq, H_A, DH_A), vs.reshape(depth, dec_batch, dec_seq, H_A, DH_A),
            logf_all[:, n_prompt:].reshape(depth, dec_batch, dec_seq, H_A),
            jnp.stack(gvns).reshape(depth, dec_batch, dec_seq, D_B))
```

```python
import functools

import jax
import jax.numpy as jnp
from jax import lax
from jax.experimental import pallas as pl
from jax.experimental.pallas import tpu as pltpu
from jax.experimental.pallas import tpu_sc as plsc

F32 = jnp.float32
BF16 = jnp.bfloat16

D_MODEL = 1024
D_A = 512
H_A = 8
DH_A = 64
D_B = 512
G_B = 8
C_B = 64
GMLP_CHUNK = 128
N_GROUPS = 4
EXPERTS_PER_GROUP = 8
N_EXPERTS = N_GROUPS * EXPERTS_PER_GROUP
D_EXPERT = 512
LN_EPS = 1e-5
HEAD_PAIRS = H_A // 2
HALF_MODEL = D_MODEL // 2

LANES = 128
SUBLANES = 8
VMEM_LIMIT_BYTES = 56 * 1024 * 1024
SC_CORES = 2
SC_SUBCORES = 16
SC_GATHER_ROWS = 128

TOKEN_TILE = 512
MOE_ROWS = 512
PROMPT_Q_TILE = 1024
PROMPT_K_TILE = 1024
SAMPLE_K_TILE = 2048
PROMPT_HEADS = 4
V_ROWS = LANES
NEG_BIG = -1e30
LOG2E = 1.4426950408889634


def _cparams(*sem):
    return pltpu.CompilerParams(dimension_semantics=sem, vmem_limit_bytes=VMEM_LIMIT_BYTES)


def _split3(x):
    hi = x.astype(BF16)
    r1 = x - hi.astype(F32)
    mid = r1.astype(BF16)
    lo = (r1 - mid.astype(F32)).astype(BF16)
    return hi, mid, lo


def _gelu_tanh(x):
    return 0.5 * x * (1.0 + jnp.tanh(0.7978845608028654 * (x + 0.044715 * (x * x * x))))


def _log_sigmoid(z):
    return jnp.minimum(z, 0.0) - jnp.log(1.0 + jnp.exp(-jnp.abs(z)))


def _layer_norm(x, g, b):
    mu = jnp.mean(x, axis=-1, keepdims=True)
    xc = x - mu
    var = jnp.mean(xc * xc, axis=-1, keepdims=True)
    return xc * lax.rsqrt(var + LN_EPS) * g + b


def _rms_norm(x, g):
    return x * lax.rsqrt(jnp.mean(x * x, axis=-1, keepdims=True) + LN_EPS) * g


def _in_proj_kernel(n_prompt_tiles, tiles_per_seq,
                    x_ref, w_ref, bf_ref, lnvg_ref, lnvb_ref, wmix_ref, bmix_ref, gnb_ref, tri_ref, place_ref,
                    kp_in, vp_in, ks_in, vs_in,
                    q_ref, kp_ref, vp_ref, ks_ref, vs_ref, kaug_ref, vta_ref, logf_ref, dT_ref, dcol_ref, bn_ref,
                    gvn_ref, carry_ref):
    del kp_in, vp_in, ks_in, vs_in
    i = pl.program_id(0)
    tm = x_ref.shape[0]
    is_sample = i >= n_prompt_tiles
    var = is_sample.astype(jnp.int32)

    p = jnp.dot(x_ref[...].astype(BF16), w_ref[...], preferred_element_type=F32)
    q = p[:, 0:D_A]
    k = p[:, D_A:2 * D_A]
    v = p[:, 2 * D_A:3 * D_A]
    u = p[:, 3 * D_A:3 * D_A + D_B]
    gv = p[:, 3 * D_A + D_B:3 * D_A + 2 * D_B]
    fl = p[:, 3 * D_A + 2 * D_B:]

    q_ref[...] = q.astype(BF16)
    tail = jnp.where(lax.broadcasted_iota(jnp.int32, (V_ROWS - DH_A, tm), 0) == 0, 1.0, 0.0)
    for j in range(HEAD_PAIRS):
        vt_pair = v[:, j * LANES:(j + 1) * LANES].T
        vta_ref[2 * j] = jnp.concatenate([vt_pair[0:DH_A], tail], axis=0).astype(BF16)
        vta_ref[2 * j + 1] = jnp.concatenate([vt_pair[DH_A:], tail], axis=0).astype(BF16)

    ks_ref[0] = k
    vs_ref[0] = v

    logf = _log_sigmoid(fl + bf_ref[...])
    logf_ref[...] = logf[:, 0:H_A]
    hi, mid, lo = _split3(logf)
    parts = jnp.concatenate([hi, mid, lo], axis=1)
    cs = jnp.dot(tri_ref[var], parts, preferred_element_type=F32)
    cs = cs[:, 0:LANES] + cs[:, LANES:2 * LANES] + cs[:, 2 * LANES:3 * LANES]

    restart = jnp.logical_or(is_sample, i % tiles_per_seq == 0)
    d = cs + jnp.where(restart, 0.0, carry_ref[...])
    carry_ref[...] = d[tm - 1:tm, :]
    d2 = d * LOG2E
    dcol_ref[...] = d2[:, 0:H_A]
    dT_ref[...] = d2.T[0:H_A, :]

    nh, nm, nl = _split3(-d2)
    lane_t = lax.broadcasted_iota(jnp.int32, (tm, LANES), 1)
    dparts = jnp.where(lane_t < H_A, nh,
                       jnp.where(lane_t < 2 * H_A, pltpu.roll(nm, H_A, 1),
                                 jnp.where(lane_t < 3 * H_A, pltpu.roll(nl, 2 * H_A, 1),
                                           jnp.where(lane_t == 3 * H_A, 1.0, 0.0).astype(BF16))))
    aug = jnp.dot(dparts, place_ref[...], preferred_element_type=F32)
    low_t = lane_t < DH_A
    for h in range(H_A):
        kh = k[:, (h // 2) * LANES:(h // 2 + 1) * LANES]
        own = low_t if h % 2 == 0 else jnp.logical_not(low_t)
        kaug_ref[h] = jnp.where(own, kh, aug[:, h * LANES:(h + 1) * LANES]).astype(BF16)

    ug = _gelu_tanh(u)
    vn = _layer_norm(_gelu_tanh(gv), lnvg_ref[...], lnvb_ref[...])

    gvn_ref[...] = vn

    r_io = lax.broadcasted_iota(jnp.int32, (GMLP_CHUNK, GMLP_CHUNK), 0)
    c_io = lax.broadcasted_iota(jnp.int32, (GMLP_CHUNK, GMLP_CHUNK), 1)
    causal = r_io >= c_io
    half = GMLP_CHUNK // 2
    same_half = (r_io >= half) == (c_io >= half)
    keep = jnp.logical_and(causal, jnp.logical_or(jnp.logical_not(is_sample), same_half))
    lane = lax.broadcasted_iota(jnp.int32, (GMLP_CHUNK, LANES), 1)
    low_lanes = lane < C_B
    vnb = vn.astype(BF16)
    bias = bmix_ref[var]
    mixed_rows = []
    for r in range(tm // GMLP_CHUNK):
        rows = slice(r * GMLP_CHUNK, (r + 1) * GMLP_CHUNK)
        cols_out = []
        for j in range(G_B // 2):
            vj = vnb[rows, j * LANES:(j + 1) * LANES]
            m0 = jnp.where(keep, wmix_ref[var, 2 * j], 0.0).astype(BF16)
            m1 = jnp.where(keep, wmix_ref[var, 2 * j + 1], 0.0).astype(BF16)
            y0 = jnp.dot(m0, vj, preferred_element_type=F32)
            y1 = jnp.dot(m1, vj, preferred_element_type=F32)
            cols_out.append(jnp.where(low_lanes, y0, y1))
        mixed_rows.append(jnp.concatenate(cols_out, axis=1) + bias)
    mixed = jnp.concatenate(mixed_rows, axis=0)
    b_out = ug * mixed
    bn_ref[...] = _rms_norm(b_out, gnb_ref[...]).astype(BF16)

    @pl.when(jnp.logical_not(is_sample))
    def _():
        kp_ref[0] = k
        vp_ref[0] = v


def _in_proj(layer, x, w, bf, lnvg, lnvb, wmix, bmix, gnb, tri, place, kp, vp, ks, vs, n_prompt, seq):
    t_rows = x.shape[0]
    tm = TOKEN_TILE
    n_tiles = t_rows // tm
    npt = n_prompt // tm
    n_sample = t_rows - n_prompt
    tiles_per_seq = seq // tm
    full = lambda a: pl.BlockSpec(a.shape, lambda i: (0,) * a.ndim)
    any_spec = pl.BlockSpec(memory_space=pl.ANY)
    p_idx = lambda i: (layer, jnp.minimum(i, npt - 1), 0)
    s_idx = lambda i: (layer, jnp.maximum(i - npt, 0), 0)
    out_shapes = (
        jax.ShapeDtypeStruct((t_rows, D_A), BF16),
        jax.ShapeDtypeStruct(kp.shape, F32),
        jax.ShapeDtypeStruct(vp.shape, F32),
        jax.ShapeDtypeStruct(ks.shape, F32),
        jax.ShapeDtypeStruct(vs.shape, F32),
        jax.ShapeDtypeStruct((H_A, t_rows, LANES), BF16),
        jax.ShapeDtypeStruct((H_A, V_ROWS, t_rows), BF16),
        jax.ShapeDtypeStruct((t_rows, H_A), F32),
        jax.ShapeDtypeStruct((H_A, t_rows), F32),
        jax.ShapeDtypeStruct((t_rows, H_A), F32),
        jax.ShapeDtypeStruct((t_rows, D_B), BF16),
        jax.ShapeDtypeStruct((n_sample, D_B), F32),
    )
    out_specs = (
        pl.BlockSpec((tm, D_A), lambda i: (i, 0)),
        pl.BlockSpec((1, tm, D_A), p_idx),
        pl.BlockSpec((1, tm, D_A), p_idx),
        pl.BlockSpec((1, tm, D_A), s_idx),
        pl.BlockSpec((1, tm, D_A), s_idx),
        pl.BlockSpec((H_A, tm, LANES), lambda i: (0, i, 0)),
        pl.BlockSpec((H_A, V_ROWS, tm), lambda i: (0, 0, i)),
        pl.BlockSpec((tm, H_A), lambda i: (i, 0)),
        pl.BlockSpec((H_A, tm), lambda i: (0, i)),
        pl.BlockSpec((tm, H_A), lambda i: (i, 0)),
        pl.BlockSpec((tm, D_B), lambda i: (i, 0)),
        pl.BlockSpec((tm, D_B), lambda i: (jnp.maximum(i - npt, 0), 0)),
    )
    in_specs = [pl.BlockSpec((tm, D_MODEL), lambda i: (i, 0)),
                full(w), full(bf), full(lnvg), full(lnvb), full(wmix), full(bmix), full(gnb), full(tri), full(place),
                any_spec, any_spec, any_spec, any_spec]
    return pl.pallas_call(
        functools.partial(_in_proj_kernel, npt, tiles_per_seq),
        grid=(n_tiles,),
        in_specs=in_specs,
        out_specs=out_specs,
        out_shape=out_shapes,
        scratch_shapes=[pltpu.VMEM((1, LANES), F32)],
        input_output_aliases={10: 1, 11: 2, 12: 3, 13: 4},
        compiler_params=_cparams("arbitrary"),
        name="in_proj",
    )(x, w, bf, lnvg, lnvb, wmix, bmix, gnb, tri, place, kp, vp, ks, vs)


def _prompt_tile(qa_ref, ka, vt_prev, key_off, masked, shifted, m_ref, acc_ref, p_ref, alpha_ref):
    tk, tq = ka.shape[1], qa_ref.shape[1]
    nt = (((1,), (1,)), ((), ()))
    if masked:
        rel = lax.broadcasted_iota(jnp.int32, (tk, tq), 0) - lax.broadcasted_iota(jnp.int32, (tk, tq), 1)
        keep = rel <= key_off
    _prompt_pv(vt_prev, p_ref, alpha_ref if shifted else None, acc_ref)
    for h in range(PROMPT_HEADS):
        s = lax.dot_general(ka[h], qa_ref[h], nt, preferred_element_type=F32)
        if masked:
            s = jnp.where(keep, s, NEG_BIG)
        if not shifted:
            p_ref[h] = jnp.exp2(s).astype(BF16)
            continue
        m_prev = m_ref[h]
        m_new = jnp.maximum(m_prev, jnp.max(s, axis=0, keepdims=True))
        m_ref[h] = m_new
        p_ref[h] = jnp.exp2(s - m_new[0:1, :]).astype(BF16)
        alpha_ref[h] = jnp.exp2(m_prev - m_new)


def _prompt_pv(vt, p_ref, alpha_ref, acc_ref):
    for h in range(PROMPT_HEADS):
        pv = jnp.dot(vt[h], p_ref[h], preferred_element_type=F32)
        acc_ref[h] = acc_ref[h] + pv if alpha_ref is None else alpha_ref[h, 0:1, :] * acc_ref[h] + pv


def _fox_prompt_kernel(shifted, qi_tab, ki_tab, first_tab, last_tab, q_ref, ka_ref, vt_ref, dq_ref, a_in,
                       o_ref, bad_ref, qa_ref, m_ref, acc_ref, p_ref, alpha_ref, vtp_ref):
    del a_in
    t = pl.program_id(2)
    qi = qi_tab[t]
    ki = ki_tab[t]
    tq = q_ref.shape[0]
    tk = ka_ref.shape[1]

    @pl.when(jnp.logical_and(t == 0, jnp.logical_and(pl.program_id(0) == 0, pl.program_id(1) == 0)))
    def _():
        p_ref[...] = jnp.zeros_like(p_ref)

    @pl.when(first_tab[t] == 1)
    def _():
        m_ref[...] = jnp.full_like(m_ref, -jnp.inf)
        acc_ref[...] = jnp.zeros_like(acc_ref)
        alpha_ref[...] = jnp.zeros_like(alpha_ref)
        vtp_ref[...] = jnp.zeros_like(vtp_ref)
        lane = lax.broadcasted_iota(jnp.int32, (tq, LANES), 1)
        for pp in range(PROMPT_HEADS // 2):
            dqc = jnp.concatenate([dq_ref[pp], jnp.zeros((LANES - SUBLANES, tq), F32)], axis=0).T
            parts = [p.astype(F32) for p in _split3(dqc)]
            q2 = q_ref[:, pp * LANES:(pp + 1) * LANES].astype(F32)
            for h in range(2):
                base = DH_A if h == 0 else 0
                own = (lane < DH_A) if h == 0 else (lane >= DH_A)
                extra = jnp.where(jnp.logical_and(lane >= base + 3, lane < base + 6), 1.0, 0.0)
                for c in range(3):
                    extra = jnp.where(lane == base + c, parts[c][:, h:h + 1], extra)
                qa_ref[2 * pp + h] = jnp.where(own, q2, extra).astype(BF16)

    key_off = qi * tq - ki * tk
    unmasked = (ki + 1) * tk <= qi * tq

    @pl.when(unmasked)
    def _():
        _prompt_tile(qa_ref, ka_ref[...], vtp_ref[...], key_off, False, shifted, m_ref, acc_ref, p_ref, alpha_ref)
        vtp_ref[...] = vt_ref[...]

    @pl.when(jnp.logical_not(unmasked))
    def _():
        _prompt_tile(qa_ref, ka_ref[...], vtp_ref[...], key_off, True, shifted, m_ref, acc_ref, p_ref, alpha_ref)
        vtp_ref[...] = vt_ref[...]

    @pl.when(last_tab[t] == 1)
    def _():
        _prompt_pv(vt_ref[...], p_ref, alpha_ref if shifted else None, acc_ref)
        bad = jnp.zeros((1, tq), F32)
        for h in range(PROMPT_HEADS):
            acc = acc_ref[h]
            bad = jnp.maximum(bad, jnp.max(jnp.where(acc - acc == 0.0, 0.0, 1.0), axis=0, keepdims=True))
            bad = jnp.maximum(bad, jnp.where(acc[DH_A:DH_A + 1, :] > 0.0, 0.0, 1.0))
        bad_ref[0] = jnp.broadcast_to(jnp.max(bad, axis=1, keepdims=True), (SUBLANES, LANES))
        for pp in range(PROMPT_HEADS // 2):
            out_t = jnp.concatenate([acc_ref[2 * pp + h, 0:DH_A, :] / acc_ref[2 * pp + h, DH_A:DH_A + 1, :]
                                     for h in range(2)], axis=0)
            o_ref[:, pp * LANES:(pp + 1) * LANES] = out_t.T.astype(o_ref.dtype)


def _fox_prompt_call(shifted, q, kaug, vta, dpair, batch, seq, tq, tk):
    t_rows = q.shape[0]
    nq, nk = seq // tq, seq // tk
    kv_per_q = tq // tk
    pairs = [(a, b) for a in range(nq) for b in range((a + 1) * kv_per_q)]
    n_steps = len(pairs)
    qi_tab = jnp.asarray([a for a, _ in pairs], jnp.int32)
    ki_tab = jnp.asarray([b for _, b in pairs], jnp.int32)
    first_tab = jnp.asarray([int(i == 0 or pairs[i - 1][0] != pairs[i][0]) for i in range(n_steps)], jnp.int32)
    last_tab = jnp.asarray([int(i == n_steps - 1 or pairs[i + 1][0] != pairs[i][0]) for i in range(n_steps)],
                           jnp.int32)
    nh = PROMPT_HEADS
    groups = H_A // nh
    wq = nh * DH_A
    grid_spec = pltpu.PrefetchScalarGridSpec(
        num_scalar_prefetch=4,
        grid=(batch, groups, n_steps),
        in_specs=[
            pl.BlockSpec((tq, wq), lambda b, g, t, qt, kt, ft, lt: (b * nq + qt[t], g)),
            pl.BlockSpec((nh, tk, LANES), lambda b, g, t, qt, kt, ft, lt: (g, b * nk + kt[t], 0)),
            pl.BlockSpec((nh, V_ROWS, tk), lambda b, g, t, qt, kt, ft, lt: (g, 0, b * nk + kt[t])),
            pl.BlockSpec((nh // 2, SUBLANES, tq), lambda b, g, t, qt, kt, ft, lt: (g, 0, b * nq + qt[t])),
            pl.BlockSpec(memory_space=pl.ANY),
        ],
        out_specs=(pl.BlockSpec((tq, wq), lambda b, g, t, qt, kt, ft, lt: (b * nq + qt[t], g)),
                   pl.BlockSpec((1, SUBLANES, LANES),
                                lambda b, g, t, qt, kt, ft, lt: ((b * groups + g) * nq + qt[t], 0, 0))),
        scratch_shapes=[pltpu.VMEM((nh, tq, LANES), BF16),
                        pltpu.VMEM((nh, SUBLANES, tq), F32),
                        pltpu.VMEM((nh, V_ROWS, tq), F32),
                        pltpu.VMEM((nh, tk, tq), BF16), pltpu.VMEM((nh, SUBLANES, tq), F32),
                        pltpu.VMEM((nh, V_ROWS, tk), BF16)],
    )
    return pl.pallas_call(
        functools.partial(_fox_prompt_kernel, shifted),
        grid_spec=grid_spec,
        out_shape=(jax.ShapeDtypeStruct((t_rows, D_A), BF16),
                   jax.ShapeDtypeStruct((batch * groups * nq, SUBLANES, LANES), F32)),
        input_output_aliases={8: 0},
        compiler_params=_cparams("arbitrary", "arbitrary", "arbitrary"),
        name="fox_prompt_shifted" if shifted else "fox_prompt",
    )(qi_tab, ki_tab, first_tab, last_tab, q, kaug, vta, dpair, jnp.zeros((t_rows, D_A), BF16))


def _fox_prompt(q, kaug, vta, dpair, batch, seq, tq, tk):
    args = (q, kaug, vta, dpair, batch, seq, tq, tk)
    a_fast, bad = _fox_prompt_call(False, *args)
    return lax.cond(jnp.max(bad) > 0.0, lambda: _fox_prompt_call(True, *args)[0], lambda: a_fast)


def _sample_tile(shifted, q2, k2, v2, bias_a, bias_b, mask, m_ref, l_ref, acc_ref):
    tq = q2.shape[0]
    low = lax.broadcasted_iota(jnp.int32, (tq, LANES), 1) < DH_A
    zero = jnp.zeros_like(q2)
    nt = (((1,), (1,)), ((), ()))
    pvs, alphas = [], []
    for h, (qh, bias) in enumerate(((jnp.where(low, q2, zero), bias_a), (jnp.where(low, zero, q2), bias_b))):
        s = lax.dot_general(qh, k2, nt, preferred_element_type=F32) + bias
        if mask is not None:
            s = jnp.where(mask, s, NEG_BIG)
        if not shifted:
            p = jnp.exp2(s)
            l_ref[h] = l_ref[h] + jnp.sum(p, axis=1, keepdims=True)
            pvs.append(jnp.dot(p.astype(BF16), v2, preferred_element_type=F32))
            continue
        m_prev = m_ref[h]
        m_new = jnp.maximum(m_prev, jnp.max(s, axis=1, keepdims=True))
        alpha = jnp.exp2(m_prev - m_new)
        p = jnp.exp2(s - m_new[:, 0:1])
        l_ref[h] = alpha * l_ref[h] + jnp.sum(p, axis=1, keepdims=True)
        m_ref[h] = m_new
        pvs.append(jnp.dot(p.astype(BF16), v2, preferred_element_type=F32))
        alphas.append(alpha)
    if shifted:
        acc_ref[...] = jnp.where(low, alphas[0], alphas[1]) * acc_ref[...] + jnp.where(low, pvs[0], pvs[1])
    else:
        acc_ref[...] = acc_ref[...] + jnp.where(low, pvs[0], pvs[1])


def _fox_sample_kernel(shifted, n_cache_tiles, q_ref, ck_ref, cv_ref, r_ref, nk_ref, nv_ref, dqc_ref, dqr_ref,
                       a_in, o_ref, bad_ref, m_ref, l_ref, acc_ref):
    del a_in
    s_idx = pl.program_id(1)
    tq = q_ref.shape[0]

    @pl.when(s_idx == 0)
    def _():
        m_ref[...] = jnp.full_like(m_ref, -jnp.inf)
        l_ref[...] = jnp.zeros_like(l_ref)
        acc_ref[...] = jnp.zeros_like(acc_ref)

    dq = dqc_ref[...]

    def pair_refs(p):
        cols = pl.ds(p * LANES, LANES)
        return m_ref.at[p], l_ref.at[p], acc_ref.at[:, cols]

    @pl.when(s_idx < n_cache_tiles)
    def _():
        r = r_ref[0, 0]
        for p in range(HEAD_PAIRS):
            cols = slice(p * LANES, (p + 1) * LANES)
            bias_a = dq[:, 2 * p:2 * p + 1] + r[2 * p:2 * p + 1, :]
            bias_b = dq[:, 2 * p + 1:2 * p + 2] + r[2 * p + 1:2 * p + 2, :]
            _sample_tile(shifted, q_ref[:, cols], ck_ref[0, 0, :, cols].astype(BF16),
                         cv_ref[0, 0, :, cols].astype(BF16), bias_a, bias_b, None, *pair_refs(p))

    @pl.when(s_idx == n_cache_tiles)
    def _():
        dk = dqr_ref[0]
        causal = lax.broadcasted_iota(jnp.int32, (tq, tq), 0) >= lax.broadcasted_iota(jnp.int32, (tq, tq), 1)
        for p in range(HEAD_PAIRS):
            cols = slice(p * LANES, (p + 1) * LANES)
            bias_a = dq[:, 2 * p:2 * p + 1] - dk[2 * p:2 * p + 1, :]
            bias_b = dq[:, 2 * p + 1:2 * p + 2] - dk[2 * p + 1:2 * p + 2, :]
            _sample_tile(shifted, q_ref[:, cols], nk_ref[0, :, cols].astype(BF16), nv_ref[0, :, cols].astype(BF16),
                         bias_a, bias_b, causal, *pair_refs(p))
        low = lax.broadcasted_iota(jnp.int32, (tq, LANES), 1) < DH_A
        bad = jnp.zeros((tq, LANES), F32)
        for p in range(HEAD_PAIRS):
            cols = slice(p * LANES, (p + 1) * LANES)
            acc = acc_ref[:, cols]
            den = jnp.where(low, l_ref[p, 0], l_ref[p, 1])
            bad = jnp.maximum(bad, jnp.where(jnp.logical_and(acc - acc == 0.0, den > 0.0), 0.0, 1.0))
            o_ref[:, cols] = (acc / den).astype(o_ref.dtype)
        bad = jnp.max(jnp.max(bad, axis=0, keepdims=True), axis=1, keepdims=True)
        bad_ref[0] = jnp.broadcast_to(bad, (SUBLANES, LANES))


def _fox_sample_call(shifted, layer, q, cache_k, cache_v, rsuf, ks, vs, dcol, drow_s, a_buf, n_prompt, tile):
    depth, dec_batch, past, _ = cache_k.shape
    dec_seq = ks.shape[1] // dec_batch
    nct = past // tile
    q0 = n_prompt // dec_seq
    ci = lambda b, s: (layer, b, jnp.minimum(s, nct - 1), 0)
    return pl.pallas_call(
        functools.partial(_fox_sample_kernel, shifted, nct),
        grid=(dec_batch, nct + 1),
        in_specs=[
            pl.BlockSpec((dec_seq, D_A), lambda b, s: (q0 + b, 0)),
            pl.BlockSpec((1, 1, tile, D_A), ci),
            pl.BlockSpec((1, 1, tile, D_A), ci),
            pl.BlockSpec((1, 1, H_A, tile), lambda b, s: (layer, b, 0, jnp.minimum(s, nct - 1))),
            pl.BlockSpec((1, dec_seq, D_A), lambda b, s: (layer, b, 0)),
            pl.BlockSpec((1, dec_seq, D_A), lambda b, s: (layer, b, 0)),
            pl.BlockSpec((dec_seq, H_A), lambda b, s: (q0 + b, 0)),
            pl.BlockSpec((1, H_A, dec_seq), lambda b, s: (b, 0, 0)),
            pl.BlockSpec(memory_space=pl.ANY),
        ],
        out_specs=(pl.BlockSpec((dec_seq, D_A), lambda b, s: (q0 + b, 0)),
                   pl.BlockSpec((1, SUBLANES, LANES), lambda b, s: (b, 0, 0))),
        out_shape=(jax.ShapeDtypeStruct(a_buf.shape, a_buf.dtype),
                   jax.ShapeDtypeStruct((dec_batch, SUBLANES, LANES), F32)),
        scratch_shapes=[pltpu.VMEM((HEAD_PAIRS, 2, dec_seq, LANES), F32),
                        pltpu.VMEM((HEAD_PAIRS, 2, dec_seq, LANES), F32),
                        pltpu.VMEM((dec_seq, D_A), F32)],
        input_output_aliases={8: 0},
        compiler_params=_cparams("parallel", "arbitrary"),
        name="fox_sample_shifted" if shifted else "fox_sample",
    )(q, cache_k, cache_v, rsuf, ks, vs, dcol, drow_s, a_buf)


def _fox_sample(*args):
    a_fast, bad = _fox_sample_call(False, *args)
    return lax.cond(jnp.max(bad) > 0.0, lambda: _fox_sample_call(True, *args)[0], lambda: a_fast)


def _suffix_sum_kernel(x_ref, u_ref, o_ref, carry_ref):
    j = pl.program_id(0)

    @pl.when(j == 0)
    def _():
        carry_ref[...] = jnp.zeros_like(carry_ref)

    x = x_ref[...]
    hi, mid, lo = _split3(x)
    u = u_ref[...]
    loc = (jnp.dot(hi, u, preferred_element_type=F32) + jnp.dot(mid, u, preferred_element_type=F32)
           + jnp.dot(lo, u, preferred_element_type=F32))
    o_ref[...] = (loc + carry_ref[:, 0:1]) * LOG2E
    carry_ref[...] = carry_ref[...] + jnp.sum(x, axis=1, keepdims=True)


def _suffix_sums(x):
    rows, n = x.shape
    tb = min(512, n)
    nb = n // tb
    u = (lax.broadcasted_iota(jnp.int32, (tb, tb), 0) > lax.broadcasted_iota(jnp.int32, (tb, tb), 1)).astype(BF16)
    return pl.pallas_call(
        _suffix_sum_kernel,
        grid=(nb,),
        in_specs=[pl.BlockSpec((rows, tb), lambda j: (0, nb - 1 - j)),
                  pl.BlockSpec((tb, tb), lambda j: (0, 0))],
        out_specs=pl.BlockSpec((rows, tb), lambda j: (0, nb - 1 - j)),
        out_shape=jax.ShapeDtypeStruct((rows, n), F32),
        scratch_shapes=[pltpu.VMEM((rows, LANES), F32)],
        compiler_params=_cparams("arbitrary"),
        name="cache_suffix_sums",
    )(x, u)


def _out_proj_kernel(alpha, a_ref, bn_ref, x_ref, gna_ref, wo_ref, g1_ref, b1_ref, wrh_ref, wrl_ref, br_ref, ls_ref,
                     x1_ref, x1b_ref, route_ref, counts_ref, carry_ref):
    i = pl.program_id(0)
    tm = x_ref.shape[0]

    @pl.when(i == 0)
    def _():
        carry_ref[...] = jnp.zeros_like(carry_ref)

    an = _rms_norm(a_ref[...].astype(F32), gna_ref[...]).astype(BF16)
    mix = (jnp.dot(an, wo_ref[0:D_A, :], preferred_element_type=F32)
           + jnp.dot(bn_ref[...], wo_ref[D_A:, :], preferred_element_type=F32))
    x1 = _layer_norm(alpha * x_ref[...] + mix, g1_ref[...], b1_ref[...])
    x1_ref[...] = x1
    x1h = x1.astype(BF16)
    x1b_ref[...] = _pack_halves(x1)

    x1l = (x1 - x1h.astype(F32)).astype(BF16)
    logits = (jnp.dot(x1h, wrh_ref[...], preferred_element_type=F32)
              + jnp.dot(x1l, wrh_ref[...], preferred_element_type=F32)
              + jnp.dot(x1h, wrl_ref[...], preferred_element_type=F32)) + br_ref[...]
    lane = lax.broadcasted_iota(jnp.int32, (tm, LANES), 1)
    is_group = lane < N_GROUPS
    gl = jnp.where(is_group, logits, NEG_BIG)
    gmax = jnp.max(gl, axis=1, keepdims=True)
    g_idx = jnp.min(jnp.where(gl == gmax, lane, LANES), axis=1, keepdims=True)
    g_prob = 1.0 / jnp.sum(jnp.exp(gl - gmax), axis=1, keepdims=True)
    in_group = jnp.logical_and(lane >= N_GROUPS, lane < N_GROUPS + N_EXPERTS)
    in_group = jnp.logical_and(in_group, lax.shift_right_arithmetic(lane - N_GROUPS, 3) == g_idx)
    el = jnp.where(in_group, logits, NEG_BIG)
    e1 = jnp.max(el, axis=1, keepdims=True)
    i1 = jnp.min(jnp.where(el == e1, lane, LANES), axis=1, keepdims=True)
    el2 = jnp.where(lane == i1, NEG_BIG, el)
    e2 = jnp.max(el2, axis=1, keepdims=True)
    i2 = jnp.min(jnp.where(el2 == e2, lane, LANES), axis=1, keepdims=True)
    t2 = jnp.exp(e2 - e1)
    w1 = 1.0 / (1.0 + t2)
    gate1 = g_prob * w1
    gate2 = g_prob * (t2 * w1)

    oh1 = lane == i1
    oh2 = lane == i2
    both = (oh1.astype(F32) + oh2.astype(F32)).astype(BF16)
    before = jnp.dot(ls_ref[...], both, preferred_element_type=F32) + carry_ref[...]
    rank1 = jnp.sum(jnp.where(oh1, before, 0.0), axis=1, keepdims=True)
    rank2 = jnp.sum(jnp.where(oh2, before, 0.0), axis=1, keepdims=True)
    carry_ref[...] = carry_ref[...] + jnp.sum(both.astype(F32), axis=0, keepdims=True)
    counts_ref[...] = carry_ref[...]

    route = jnp.where(lane == 0, (i1 - N_GROUPS).astype(F32), 0.0)
    route = jnp.where(lane == 1, (i2 - N_GROUPS).astype(F32), route)
    route = jnp.where(lane == 2, rank1, route)
    route = jnp.where(lane == 3, rank2, route)
    route = jnp.where(lane == 4, gate1, route)
    route = jnp.where(lane == 5, gate2, route)
    route_ref[...] = route


def _out_proj(alpha, a, bn, x, gna, wo, g1, b1, wrh, wrl, br, lstrict):
    t_rows = x.shape[0]
    tm = TOKEN_TILE
    full = lambda arr: pl.BlockSpec(arr.shape, lambda i: (0,) * arr.ndim)
    row = lambda w: pl.BlockSpec((tm, w), lambda i: (i, 0))
    return pl.pallas_call(
        functools.partial(_out_proj_kernel, alpha),
        grid=(t_rows // tm,),
        in_specs=[row(D_A), row(D_B), row(D_MODEL), full(gna), full(wo), full(g1), full(b1), full(wrh), full(wrl),
                  full(br), full(lstrict)],
        out_specs=(row(D_MODEL), row(HALF_MODEL), row(LANES), pl.BlockSpec((1, LANES), lambda i: (0, 0))),
        out_shape=(jax.ShapeDtypeStruct((t_rows, D_MODEL), F32),
                   jax.ShapeDtypeStruct((t_rows, HALF_MODEL), jnp.int32),
                   jax.ShapeDtypeStruct((t_rows, LANES), F32),
                   jax.ShapeDtypeStruct((1, LANES), F32)),
        scratch_shapes=[pltpu.VMEM((1, LANES), F32)],
        compiler_params=_cparams("arbitrary"),
        name="out_proj_router",
    )(a, bn, x, gna, wo, g1, b1, wrh, wrl, br, lstrict)


def _pack_halves(x):
    lo = lax.bitcast_convert_type(x[:, :HALF_MODEL].astype(BF16).astype(F32), jnp.uint32)
    hi = lax.bitcast_convert_type(x[:, HALF_MODEL:].astype(BF16).astype(F32), jnp.uint32)
    return lax.bitcast_convert_type((hi & jnp.uint32(0xFFFF0000)) | (lo >> 16), jnp.int32)


def _unpack_halves(w):
    u = lax.bitcast_convert_type(w, jnp.uint32)
    return (lax.bitcast_convert_type(u << 16, F32), lax.bitcast_convert_type(u & jnp.uint32(0xFFFF0000), F32))


def _expert_kernel(ib_ref, ie_ref, ni_ref, st_ref, en_ref, x_ref, w1_ref, w3_ref, w2_ref, o_ref,
                   w1b_ref, w3b_ref, w2b_ref):
    i = pl.program_id(0)
    bm = x_ref.shape[0]

    @pl.when(i < ni_ref[0])
    def _():
        e = ie_ref[i]
        b = ib_ref[i]
        prev = jnp.maximum(i - 1, 0)
        first_item = i == 0

        @pl.when(jnp.logical_or(first_item, ie_ref[prev] != e))
        def _():
            w1b_ref[...] = w1_ref[0, 0].astype(BF16)
            w3b_ref[...] = w3_ref[0, 0].astype(BF16)
            w2b_ref[...] = w2_ref[0, 0].astype(BF16)

        x_lo, x_hi = (v.astype(BF16) for v in _unpack_halves(x_ref[...]))
        h1 = (jnp.dot(x_lo, w1b_ref[:HALF_MODEL, :], preferred_element_type=F32)
              + jnp.dot(x_hi, w1b_ref[HALF_MODEL:, :], preferred_element_type=F32))
        h3 = (jnp.dot(x_lo, w3b_ref[:HALF_MODEL, :], preferred_element_type=F32)
              + jnp.dot(x_hi, w3b_ref[HALF_MODEL:, :], preferred_element_type=F32))
        h = (h1 * (1.0 / (1.0 + jnp.exp(-h1))) * h3).astype(BF16)
        y = _pack_halves(jnp.dot(h, w2b_ref[...], preferred_element_type=F32))
        first_visit = jnp.logical_or(first_item, ib_ref[prev] != b)

        @pl.when(first_visit)
        def _():
            o_ref[...] = y

        @pl.when(jnp.logical_not(first_visit))
        def _():
            row = b * bm + lax.broadcasted_iota(jnp.int32, (bm, 1), 0)
            mine = jnp.logical_and(row >= st_ref[e], row < en_ref[e])
            o_ref[...] = jnp.where(mine, y, o_ref[...])


def _experts(layer, item_block, item_expert, n_items, starts, ends, xs, w1, w3, w2):
    n_rows = xs.shape[0]
    bm = MOE_ROWS
    n_max = item_block.shape[0]
    item = lambda i, ni: jnp.minimum(i, ni[0] - 1)
    blk = lambda i, ib, ie, ni, st, en: (ib[item(i, ni)], 0)
    wsel = lambda i, ib, ie, ni, st, en: (layer, ie[item(i, ni)], 0, 0)
    grid_spec = pltpu.PrefetchScalarGridSpec(
        num_scalar_prefetch=5,
        grid=(n_max,),
        in_specs=[pl.BlockSpec((bm, HALF_MODEL), blk),
                  pl.BlockSpec((1, 1, D_MODEL, D_EXPERT), wsel),
                  pl.BlockSpec((1, 1, D_MODEL, D_EXPERT), wsel),
                  pl.BlockSpec((1, 1, D_EXPERT, D_MODEL), wsel)],
        out_specs=pl.BlockSpec((bm, HALF_MODEL), blk),
        scratch_shapes=[pltpu.VMEM((D_MODEL, D_EXPERT), BF16), pltpu.VMEM((D_MODEL, D_EXPERT), BF16),
                        pltpu.VMEM((D_EXPERT, D_MODEL), BF16)],
    )
    return pl.pallas_call(
        _expert_kernel,
        grid_spec=grid_spec,
        out_shape=jax.ShapeDtypeStruct((n_rows, HALF_MODEL), jnp.int32),
        compiler_params=_cparams("arbitrary"),
        name="experts",
    )(item_block, item_expert, n_items, starts, ends, xs, w1, w3, w2)


def _combine_kernel(alpha, n_first, x1_ref, y0_ref, y1_ref, route_ref, g2_ref, b2_ref, *o_refs):
    route = route_ref[...]
    g0, g1 = route[:, 4:5], route[:, 5:6]
    y0_lo, y0_hi = _unpack_halves(y0_ref[...])
    y1_lo, y1_hi = _unpack_halves(y1_ref[...])
    z = jnp.concatenate([alpha * x1_ref[:, :HALF_MODEL] + (g0 * y0_lo + g1 * y1_lo),
                         alpha * x1_ref[:, HALF_MODEL:] + (g0 * y0_hi + g1 * y1_hi)], axis=1)
    out = _layer_norm(z, g2_ref[...], b2_ref[...])
    if n_first is None:
        o_refs[0][...] = out
    else:
        first = pl.program_id(0) < n_first

        @pl.when(first)
        def _():
            o_refs[0][...] = out

        @pl.when(jnp.logical_not(first))
        def _():
            o_refs[1][...] = out


def _combine(alpha, x1, y01, route, g2, b2, split_rows=None):
    t_rows = x1.shape[0]
    tm = TOKEN_TILE
    n_tiles = t_rows // tm
    full = lambda arr: pl.BlockSpec(arr.shape, lambda i: (0,) * arr.ndim)
    row = lambda w: pl.BlockSpec((tm, w), lambda i: (i, 0))
    if split_rows is None:
        n_first = None
        out_specs = row(D_MODEL)
        out_shape = jax.ShapeDtypeStruct((t_rows, D_MODEL), F32)
    else:
        n_first = split_rows // tm
        out_specs = (pl.BlockSpec((tm, D_MODEL), lambda i: (jnp.minimum(i, n_first - 1), 0)),
                     pl.BlockSpec((tm, D_MODEL), lambda i: (jnp.maximum(i - n_first, 0), 0)))
        out_shape = (jax.ShapeDtypeStruct((split_rows, D_MODEL), F32),
                     jax.ShapeDtypeStruct((t_rows - split_rows, D_MODEL), F32))
    return pl.pallas_call(
        functools.partial(_combine_kernel, alpha, n_first),
        grid=(n_tiles,),
        in_specs=[row(D_MODEL), row(HALF_MODEL), pl.BlockSpec((tm, HALF_MODEL), lambda i: (i + n_tiles, 0)),
                  row(LANES), full(g2), full(b2)],
        out_specs=out_specs,
        out_shape=out_shape,
        compiler_params=_cparams("arbitrary"),
        name="moe_combine",
    )(x1, y01, y01, route, g2, b2)


def _gather_rows(table, idx):
    n_rows, width = idx.shape[0], table.shape[1]
    workers = SC_CORES * SC_SUBCORES
    step = workers * SC_GATHER_ROWS
    n_pad = -(-n_rows // step) * step
    if n_pad != n_rows:
        idx = jnp.pad(idx, (0, n_pad - n_rows))
    per_worker = n_pad // workers
    n_chunks = per_worker // SC_GATHER_ROWS
    mesh = plsc.VectorSubcoreMesh(core_axis_name="c", subcore_axis_name="s")

    @functools.partial(
        pl.kernel, mesh=mesh,
        out_type=jax.ShapeDtypeStruct((n_pad, width), table.dtype),
        scratch_types=[pltpu.VMEM((SC_GATHER_ROWS,), jnp.int32),
                       pltpu.VMEM((SC_GATHER_ROWS, width), table.dtype),
                       pltpu.SemaphoreType.DMA],
        name="sc_gather_rows",
    )
    def gather(table_hbm, idx_hbm, out_hbm, idx_v, rows_v, sem):
        base = (lax.axis_index("s") * SC_CORES + lax.axis_index("c")) * per_worker

        @pl.loop(0, n_chunks)
        def _(c):
            off = base + c * SC_GATHER_ROWS
            pltpu.sync_copy(idx_hbm.at[pl.ds(off, SC_GATHER_ROWS)], idx_v)
            pltpu.async_copy(table_hbm.at[idx_v], rows_v, sem).wait()
            pltpu.sync_copy(rows_v, out_hbm.at[pl.ds(off, SC_GATHER_ROWS)])

    out = gather(table, idx)
    return out if n_pad == n_rows else out[:n_rows]


def _attention_tile(seq):
    for t in (512, 256, 128):
        if seq % t == 0:
            return t
    raise ValueError("sequence length must be a multiple of 128")


def kernel(x_prompt, x_sample, cache_k, cache_v, cache_logf, w_in, b_f, ln_v_g, ln_v_b, w_s, b_s,
           g_norm_a, g_norm_b, w_out, ln1_g, ln1_b, w_gr, b_gr, w_er, b_er, w1, w3, w2, ln2_g, ln2_b):
    batch, seq, _ = x_prompt.shape
    dec_batch, dec_seq, _ = x_sample.shape
    depth = w_in.shape[0]
    past = cache_k.shape[2]
    n_prompt = batch * seq
    n_sample = dec_batch * dec_seq
    t_rows = n_prompt + n_sample
    alpha = float((2 * depth) ** 0.25)
    tm = TOKEN_TILE
    assert seq % tm == 0 and n_sample % tm == 0 and tm % dec_seq == 0 and dec_seq == GMLP_CHUNK // 2
    assert past % 128 == 0

    x = jnp.concatenate([x_prompt.reshape(n_prompt, D_MODEL), x_sample.reshape(n_sample, D_MODEL)], axis=0)

    sp = (D_A, 2 * D_A, 3 * D_A, 3 * D_A + H_A, 3 * D_A + H_A + D_B)
    wq, wk, wv, wf, wu, wgv = (w_in[..., a:b] for a, b in zip((0,) + sp, sp + (w_in.shape[-1],)))
    wf_pad = jnp.pad(wf, ((0, 0), (0, 0), (0, LANES - H_A)))
    w_cat = jnp.concatenate([wq * (LOG2E * DH_A ** -0.5), wk, wv, wu, wgv, wf_pad], axis=-1).astype(BF16)
    bf_pad = jnp.pad(b_f, ((0, 0), (0, LANES - H_A)))[:, None, :]
    half = GMLP_CHUNK // 2
    wmix = jnp.stack([w_s, jnp.tile(w_s[:, :, :half, :half], (1, 1, 2, 2))], axis=1)
    bs_t = jnp.swapaxes(b_s, 1, 2)
    bs_var = jnp.stack([bs_t, jnp.tile(bs_t[:, :half], (1, 2, 1))], axis=1)
    bmix = jnp.repeat(bs_var, C_B, axis=-1)
    wo_b = w_out.astype(BF16)
    wr = jnp.pad(jnp.concatenate([w_gr, w_er], axis=-1), ((0, 0), (0, 0), (0, LANES - N_GROUPS - N_EXPERTS)))
    wrh = wr.astype(BF16)
    wrl = (wr - wrh.astype(F32)).astype(BF16)
    br = jnp.pad(jnp.concatenate([b_gr, b_er], axis=-1), ((0, 0), (0, LANES - N_GROUPS - N_EXPERTS)))[:, None, :]
    row2 = lambda a: a[:, None, :]

    ri = lax.broadcasted_iota(jnp.int32, (tm, tm), 0)
    ci = lax.broadcasted_iota(jnp.int32, (tm, tm), 1)
    tri = jnp.stack([ri >= ci, jnp.logical_and(ri >= ci, ri // dec_seq == ci // dec_seq)]).astype(BF16)
    lstrict = (ri > ci).astype(BF16)
    prow = lax.broadcasted_iota(jnp.int32, (LANES, H_A * LANES), 0)
    pcol = lax.broadcasted_iota(jnp.int32, (LANES, H_A * LANES), 1)
    phead = pcol // LANES
    poff = pcol % LANES - jnp.where(phead % 2 == 0, DH_A, 0)
    is_one = jnp.logical_and(prow == 3 * H_A, jnp.logical_and(poff >= 0, poff < 3))
    is_part = jnp.logical_and(jnp.logical_and(poff >= 3, poff < 6), prow == (poff - 3) * H_A + phead)
    place = jnp.logical_or(is_one, is_part).astype(BF16)

    clf = jnp.transpose(cache_logf, (0, 1, 3, 2)).reshape(depth * dec_batch * H_A, past)
    rsuf = _suffix_sums(clf).reshape(depth, dec_batch, H_A, past)
    ck = cache_k.reshape(depth, dec_batch, past, D_A)
    cv = cache_v.reshape(depth, dec_batch, past, D_A)

    kp = jnp.zeros((depth, n_prompt, D_A), F32)
    vp = jnp.zeros((depth, n_prompt, D_A), F32)
    ks = jnp.zeros((depth, n_sample, D_A), F32)
    vs = jnp.zeros((depth, n_sample, D_A), F32)
    logfs, gvns = [], []

    bm = MOE_ROWS
    n_assign = 2 * t_rows
    nblk = n_assign // bm
    expert_ids = jnp.arange(N_EXPERTS, dtype=jnp.int32)
    item_ids = jnp.arange(nblk + N_EXPERTS - 1, dtype=jnp.int32)
    q_tile = PROMPT_Q_TILE if seq % PROMPT_Q_TILE == 0 else _attention_tile(seq)
    k_tile = PROMPT_K_TILE if q_tile % PROMPT_K_TILE == 0 else q_tile
    c_tile = SAMPLE_K_TILE if past % SAMPLE_K_TILE == 0 else _attention_tile(past)
    assign_ids = jnp.arange(n_assign, dtype=jnp.int32)

    for l in range(depth):
        q, kp, vp, ks, vs, kaug, vta, logf, dT, dcol, bn, gvn = _in_proj(
            l, x, w_cat[l], bf_pad[l], row2(ln_v_g)[l], row2(ln_v_b)[l], wmix[l], bmix[l], row2(g_norm_b)[l], tri,
            place, kp, vp, ks, vs, n_prompt, seq)
        logfs.append(logf)
        gvns.append(gvn)

        dpair = jnp.pad(dT.reshape(HEAD_PAIRS, 2, t_rows), ((0, 0), (0, SUBLANES - 2), (0, 0)))
        a = _fox_prompt(q, kaug, vta, dpair, batch, seq, q_tile, k_tile)
        drow_s = jnp.transpose(dT[:, n_prompt:].reshape(H_A, dec_batch, dec_seq), (1, 0, 2))
        a = _fox_sample(l, q, ck, cv, rsuf, ks, vs, dcol, drow_s, a, n_prompt, c_tile)

        x1, x1b, route, counts = _out_proj(alpha, a, bn, x, row2(g_norm_a)[l], wo_b[l], row2(ln1_g)[l],
                                           row2(ln1_b)[l], wrh[l], wrl[l], br[l], lstrict)

        cnt = counts[0, N_GROUPS:N_GROUPS + N_EXPERTS].astype(jnp.int32)
        ends = jnp.cumsum(cnt)
        starts = ends - cnt
        eid = route[:, 0:2].astype(jnp.int32)
        rank = route[:, 2:4].astype(jnp.int32)
        onehot = eid[:, :, None] == expert_ids[None, None, :]
        pos = jnp.sum(jnp.where(onehot, starts[None, None, :], 0), axis=-1) + rank
        order = jnp.sort(eid.reshape(n_assign) * n_assign + assign_ids) % n_assign
        first_blk = starts // bm
        n_it = jnp.where(cnt > 0, (ends - 1) // bm - first_blk + 1, 0)
        it_end = jnp.cumsum(n_it)
        item_expert = jnp.minimum(jnp.sum(item_ids[:, None] >= it_end[None, :], axis=1), N_EXPERTS - 1).astype(jnp.int32)
        it_first = jnp.sum(jnp.where(item_expert[:, None] == expert_ids[None, :], (it_end - n_it)[None, :], 0), axis=1)
        it_blk0 = jnp.sum(jnp.where(item_expert[:, None] == expert_ids[None, :], first_blk[None, :], 0), axis=1)
        item_block = jnp.clip(it_blk0 + item_ids - it_first, 0, nblk - 1).astype(jnp.int32)

        xs = _gather_rows(x1b, order // 2)
        yb = _experts(l, item_block, item_expert, it_end[-1:].astype(jnp.int32), starts, ends, xs, w1, w3, w2)
        y01 = _gather_rows(yb, jnp.concatenate([pos[:, 0], pos[:, 1]]))
        x = _combine(alpha, x1, y01, route, row2(ln2_g)[l], row2(ln2_b)[l],
                     split_rows=n_prompt if l == depth - 1 else None)

    y_prompt = x[0].reshape(batch, seq, D_MODEL)
    y_sample = x[1].reshape(dec_batch, dec_seq, D_MODEL)
    logf_all = jnp.stack(logfs)
    return (y_prompt, y_sample,
            kp.reshape(depth, batch, seq, H_A, DH_A), vp.reshape(depth, batch, seq, H_A, DH_A),
            logf_all[:, :n_prompt].reshape(depth, batch, seq, H_A),
            ks.reshape(depth, dec_batch, dec_seq, H_A, DH_A), vs.reshape(depth, dec_batch, dec_seq, H_A, DH_A),
            logf_all[:, n_prompt:].reshape(depth, dec_batch, dec_seq, H_A),
            jnp.stack(gvns).reshape(depth, dec_batch, dec_seq, D_B))
```

```python
import functools

import jax
import jax.numpy as jnp
from jax import lax
from jax.experimental import pallas as pl
from jax.experimental.pallas import tpu as pltpu
from jax.experimental.pallas import tpu_sc as plsc

F32 = jnp.float32
BF16 = jnp.bfloat16

D_MODEL = 1024
D_A = 512
H_A = 8
DH_A = 64
D_B = 512
G_B = 8
C_B = 64
GMLP_CHUNK = 128
N_GROUPS = 4
EXPERTS_PER_GROUP = 8
N_EXPERTS = N_GROUPS * EXPERTS_PER_GROUP
D_EXPERT = 512
LN_EPS = 1e-5
HEAD_PAIRS = H_A // 2
HALF_MODEL = D_MODEL // 2

LANES = 128
SUBLANES = 8
VMEM_LIMIT_BYTES = 56 * 1024 * 1024
SC_CORES = 2
SC_SUBCORES = 16
SC_GATHER_ROWS = 128

TOKEN_TILE = 512
MOE_ROWS = 512
PROMPT_Q_TILE = 1024
PROMPT_K_TILE = 1024
SAMPLE_K_TILE = 2048
PROMPT_HEADS = 8
V_ROWS = LANES
NEG_BIG = -1e30
LOG2E = 1.4426950408889634


def _cparams(*sem):
    return pltpu.CompilerParams(dimension_semantics=sem, vmem_limit_bytes=VMEM_LIMIT_BYTES)


def _split3(x):
    hi = x.astype(BF16)
    r1 = x - hi.astype(F32)
    mid = r1.astype(BF16)
    lo = (r1 - mid.astype(F32)).astype(BF16)
    return hi, mid, lo


def _gelu_tanh(x):
    return 0.5 * x * (1.0 + jnp.tanh(0.7978845608028654 * (x + 0.044715 * (x * x * x))))


def _log_sigmoid(z):
    return jnp.minimum(z, 0.0) - jnp.log(1.0 + jnp.exp(-jnp.abs(z)))


def _layer_norm(x, g, b):
    mu = jnp.mean(x, axis=-1, keepdims=True)
    xc = x - mu
    var = jnp.mean(xc * xc, axis=-1, keepdims=True)
    return xc * lax.rsqrt(var + LN_EPS) * g + b


def _rms_norm(x, g):
    return x * lax.rsqrt(jnp.mean(x * x, axis=-1, keepdims=True) + LN_EPS) * g


def _in_proj_kernel(n_prompt_tiles, tiles_per_seq,
                    x_ref, w_ref, bf_ref, lnvg_ref, lnvb_ref, wmix_ref, bmix_ref, gnb_ref, tri_ref, place_ref,
                    kp_in, vp_in, ks_in, vs_in,
                    q_ref, kp_ref, vp_ref, ks_ref, vs_ref, kaug_ref, vta_ref, logf_ref, dT_ref, dcol_ref, bn_ref,
                    gvn_ref, carry_ref):
    del kp_in, vp_in, ks_in, vs_in
    i = pl.program_id(0)
    tm = x_ref.shape[0]
    is_sample = i >= n_prompt_tiles
    var = is_sample.astype(jnp.int32)

    p = jnp.dot(x_ref[...].astype(BF16), w_ref[...], preferred_element_type=F32)
    q = p[:, 0:D_A]
    k = p[:, D_A:2 * D_A]
    v = p[:, 2 * D_A:3 * D_A]
    u = p[:, 3 * D_A:3 * D_A + D_B]
    gv = p[:, 3 * D_A + D_B:3 * D_A + 2 * D_B]
    fl = p[:, 3 * D_A + 2 * D_B:]

    q_ref[...] = q.astype(BF16)
    tail = jnp.where(lax.broadcasted_iota(jnp.int32, (V_ROWS - DH_A, tm), 0) == 0, 1.0, 0.0)
    for j in range(HEAD_PAIRS):
        vt_pair = v[:, j * LANES:(j + 1) * LANES].T
        vta_ref[2 * j] = jnp.concatenate([vt_pair[0:DH_A], tail], axis=0).astype(BF16)
        vta_ref[2 * j + 1] = jnp.concatenate([vt_pair[DH_A:], tail], axis=0).astype(BF16)

    ks_ref[0] = k
    vs_ref[0] = v

    logf = _log_sigmoid(fl + bf_ref[...])
    logf_ref[...] = logf[:, 0:H_A]
    hi, mid, lo = _split3(logf)
    parts = jnp.concatenate([hi, mid, lo], axis=1)
    cs = jnp.dot(tri_ref[var], parts, preferred_element_type=F32)
    cs = cs[:, 0:LANES] + cs[:, LANES:2 * LANES] + cs[:, 2 * LANES:3 * LANES]

    restart = jnp.logical_or(is_sample, i % tiles_per_seq == 0)
    d = cs + jnp.where(restart, 0.0, carry_ref[...])
    carry_ref[...] = d[tm - 1:tm, :]
    d2 = d * LOG2E
    dcol_ref[...] = d2[:, 0:H_A]
    dT_ref[...] = d2.T[0:H_A, :]

    nh, nm, nl = _split3(-d2)
    lane_t = lax.broadcasted_iota(jnp.int32, (tm, LANES), 1)
    dparts = jnp.where(lane_t < H_A, nh,
                       jnp.where(lane_t < 2 * H_A, pltpu.roll(nm, H_A, 1),
                                 jnp.where(lane_t < 3 * H_A, pltpu.roll(nl, 2 * H_A, 1),
                                           jnp.where(lane_t == 3 * H_A, 1.0, 0.0).astype(BF16))))
    aug = jnp.dot(dparts, place_ref[...], preferred_element_type=F32)
    low_t = lane_t < DH_A
    for h in range(H_A):
        kh = k[:, (h // 2) * LANES:(h // 2 + 1) * LANES]
        own = low_t if h % 2 == 0 else jnp.logical_not(low_t)
        kaug_ref[h] = jnp.where(own, kh, aug[:, h * LANES:(h + 1) * LANES]).astype(BF16)

    ug = _gelu_tanh(u)
    vn = _layer_norm(_gelu_tanh(gv), lnvg_ref[...], lnvb_ref[...])

    gvn_ref[...] = vn

    r_io = lax.broadcasted_iota(jnp.int32, (GMLP_CHUNK, GMLP_CHUNK), 0)
    c_io = lax.broadcasted_iota(jnp.int32, (GMLP_CHUNK, GMLP_CHUNK), 1)
    causal = r_io >= c_io
    half = GMLP_CHUNK // 2
    same_half = (r_io >= half) == (c_io >= half)
    keep = jnp.logical_and(causal, jnp.logical_or(jnp.logical_not(is_sample), same_half))
    lane = lax.broadcasted_iota(jnp.int32, (GMLP_CHUNK, LANES), 1)
    low_lanes = lane < C_B
    vnb = vn.astype(BF16)
    bias = bmix_ref[var]
    mixed_rows = []
    for r in range(tm // GMLP_CHUNK):
        rows = slice(r * GMLP_CHUNK, (r + 1) * GMLP_CHUNK)
        cols_out = []
        for j in range(G_B // 2):
            vj = vnb[rows, j * LANES:(j + 1) * LANES]
            m0 = jnp.where(keep, wmix_ref[var, 2 * j], 0.0).astype(BF16)
            m1 = jnp.where(keep, wmix_ref[var, 2 * j + 1], 0.0).astype(BF16)
            y0 = jnp.dot(m0, vj, preferred_element_type=F32)
            y1 = jnp.dot(m1, vj, preferred_element_type=F32)
            cols_out.append(jnp.where(low_lanes, y0, y1))
        mixed_rows.append(jnp.concatenate(cols_out, axis=1) + bias)
    mixed = jnp.concatenate(mixed_rows, axis=0)
    b_out = ug * mixed
    bn_ref[...] = _rms_norm(b_out, gnb_ref[...]).astype(BF16)

    @pl.when(jnp.logical_not(is_sample))
    def _():
        kp_ref[0] = k
        vp_ref[0] = v


def _in_proj(layer, x, w, bf, lnvg, lnvb, wmix, bmix, gnb, tri, place, kp, vp, ks, vs, n_prompt, seq):
    t_rows = x.shape[0]
    tm = TOKEN_TILE
    n_tiles = t_rows // tm
    npt = n_prompt // tm
    n_sample = t_rows - n_prompt
    tiles_per_seq = seq // tm
    full = lambda a: pl.BlockSpec(a.shape, lambda i: (0,) * a.ndim)
    any_spec = pl.BlockSpec(memory_space=pl.ANY)
    p_idx = lambda i: (layer, jnp.minimum(i, npt - 1), 0)
    s_idx = lambda i: (layer, jnp.maximum(i - npt, 0), 0)
    out_shapes = (
        jax.ShapeDtypeStruct((t_rows, D_A), BF16),
        jax.ShapeDtypeStruct(kp.shape, F32),
        jax.ShapeDtypeStruct(vp.shape, F32),
        jax.ShapeDtypeStruct(ks.shape, F32),
        jax.ShapeDtypeStruct(vs.shape, F32),
        jax.ShapeDtypeStruct((H_A, t_rows, LANES), BF16),
        jax.ShapeDtypeStruct((H_A, V_ROWS, t_rows), BF16),
        jax.ShapeDtypeStruct((t_rows, H_A), F32),
        jax.ShapeDtypeStruct((H_A, t_rows), F32),
        jax.ShapeDtypeStruct((t_rows, H_A), F32),
        jax.ShapeDtypeStruct((t_rows, D_B), BF16),
        jax.ShapeDtypeStruct((n_sample, D_B), F32),
    )
    out_specs = (
        pl.BlockSpec((tm, D_A), lambda i: (i, 0)),
        pl.BlockSpec((1, tm, D_A), p_idx),
        pl.BlockSpec((1, tm, D_A), p_idx),
        pl.BlockSpec((1, tm, D_A), s_idx),
        pl.BlockSpec((1, tm, D_A), s_idx),
        pl.BlockSpec((H_A, tm, LANES), lambda i: (0, i, 0)),
        pl.BlockSpec((H_A, V_ROWS, tm), lambda i: (0, 0, i)),
        pl.BlockSpec((tm, H_A), lambda i: (i, 0)),
        pl.BlockSpec((H_A, tm), lambda i: (0, i)),
        pl.BlockSpec((tm, H_A), lambda i: (i, 0)),
        pl.BlockSpec((tm, D_B), lambda i: (i, 0)),
        pl.BlockSpec((tm, D_B), lambda i: (jnp.maximum(i - npt, 0), 0)),
    )
    in_specs = [pl.BlockSpec((tm, D_MODEL), lambda i: (i, 0)),
                full(w), full(bf), full(lnvg), full(lnvb), full(wmix), full(bmix), full(gnb), full(tri), full(place),
                any_spec, any_spec, any_spec, any_spec]
    return pl.pallas_call(
        functools.partial(_in_proj_kernel, npt, tiles_per_seq),
        grid=(n_tiles,),
        in_specs=in_specs,
        out_specs=out_specs,
        out_shape=out_shapes,
        scratch_shapes=[pltpu.VMEM((1, LANES), F32)],
        input_output_aliases={10: 1, 11: 2, 12: 3, 13: 4},
        compiler_params=_cparams("arbitrary"),
        name="in_proj",
    )(x, w, bf, lnvg, lnvb, wmix, bmix, gnb, tri, place, kp, vp, ks, vs)


def _prompt_tile(qa_ref, ka, vt_prev, key_off, masked, shifted, m_ref, acc_ref, p_ref, alpha_ref):
    tk, tq = ka.shape[1], qa_ref.shape[1]
    nt = (((1,), (1,)), ((), ()))
    if masked:
        rel = lax.broadcasted_iota(jnp.int32, (tk, tq), 0) - lax.broadcasted_iota(jnp.int32, (tk, tq), 1)
        keep = rel <= key_off
    _prompt_pv(vt_prev, p_ref, alpha_ref if shifted else None, acc_ref)
    for h in range(PROMPT_HEADS):
        s = lax.dot_general(ka[h], qa_ref[h], nt, preferred_element_type=F32)
        if masked:
            s = jnp.where(keep, s, NEG_BIG)
        if not shifted:
            p_ref[h] = jnp.exp2(s).astype(BF16)
            continue
        m_prev = m_ref[h]
        m_new = jnp.maximum(m_prev, jnp.max(s, axis=0, keepdims=True))
        m_ref[h] = m_new
        p_ref[h] = jnp.exp2(s - m_new[0:1, :]).astype(BF16)
        alpha_ref[h] = jnp.exp2(m_prev - m_new)


def _prompt_pv(vt, p_ref, alpha_ref, acc_ref):
    for h in range(PROMPT_HEADS):
        pv = jnp.dot(vt[h], p_ref[h], preferred_element_type=F32)
        acc_ref[h] = acc_ref[h] + pv if alpha_ref is None else alpha_ref[h, 0:1, :] * acc_ref[h] + pv


def _fox_prompt_kernel(shifted, qi_tab, ki_tab, first_tab, last_tab, q_ref, ka_ref, vt_ref, dq_ref, a_in,
                       o_ref, bad_ref, qa_ref, m_ref, acc_ref, p_ref, alpha_ref, vtp_ref):
    del a_in
    t = pl.program_id(2)
    qi = qi_tab[t]
    ki = ki_tab[t]
    tq = q_ref.shape[0]
    tk = ka_ref.shape[1]

    @pl.when(jnp.logical_and(t == 0, jnp.logical_and(pl.program_id(0) == 0, pl.program_id(1) == 0)))
    def _():
        p_ref[...] = jnp.zeros_like(p_ref)

    @pl.when(first_tab[t] == 1)
    def _():
        m_ref[...] = jnp.full_like(m_ref, -jnp.inf)
        acc_ref[...] = jnp.zeros_like(acc_ref)
        alpha_ref[...] = jnp.zeros_like(alpha_ref)
        vtp_ref[...] = jnp.zeros_like(vtp_ref)
        lane = lax.broadcasted_iota(jnp.int32, (tq, LANES), 1)
        for pp in range(PROMPT_HEADS // 2):
            dqc = jnp.concatenate([dq_ref[pp], jnp.zeros((LANES - SUBLANES, tq), F32)], axis=0).T
            parts = [p.astype(F32) for p in _split3(dqc)]
            q2 = q_ref[:, pp * LANES:(pp + 1) * LANES].astype(F32)
            for h in range(2):
                base = DH_A if h == 0 else 0
                own = (lane < DH_A) if h == 0 else (lane >= DH_A)
                extra = jnp.where(jnp.logical_and(lane >= base + 3, lane < base + 6), 1.0, 0.0)
                for c in range(3):
                    extra = jnp.where(lane == base + c, parts[c][:, h:h + 1], extra)
                qa_ref[2 * pp + h] = jnp.where(own, q2, extra).astype(BF16)

    key_off = qi * tq - ki * tk
    unmasked = (ki + 1) * tk <= qi * tq

    @pl.when(unmasked)
    def _():
        _prompt_tile(qa_ref, ka_ref[...], vtp_ref[...], key_off, False, shifted, m_ref, acc_ref, p_ref, alpha_ref)
        vtp_ref[...] = vt_ref[...]

    @pl.when(jnp.logical_not(unmasked))
    def _():
        _prompt_tile(qa_ref, ka_ref[...], vtp_ref[...], key_off, True, shifted, m_ref, acc_ref, p_ref, alpha_ref)
        vtp_ref[...] = vt_ref[...]

    @pl.when(last_tab[t] == 1)
    def _():
        _prompt_pv(vt_ref[...], p_ref, alpha_ref if shifted else None, acc_ref)
        bad = jnp.zeros((1, tq), F32)
        for h in range(PROMPT_HEADS):
            acc = acc_ref[h]
            bad = jnp.maximum(bad, jnp.max(jnp.where(acc - acc == 0.0, 0.0, 1.0), axis=0, keepdims=True))
            bad = jnp.maximum(bad, jnp.where(acc[DH_A:DH_A + 1, :] > 0.0, 0.0, 1.0))
        bad_ref[0] = jnp.broadcast_to(jnp.max(bad, axis=1, keepdims=True), (SUBLANES, LANES))
        for pp in range(PROMPT_HEADS // 2):
            out_t = jnp.concatenate([acc_ref[2 * pp + h, 0:DH_A, :] / acc_ref[2 * pp + h, DH_A:DH_A + 1, :]
                                     for h in range(2)], axis=0)
            o_ref[:, pp * LANES:(pp + 1) * LANES] = out_t.T.astype(o_ref.dtype)


def _fox_prompt_call(shifted, q, kaug, vta, dpair, batch, seq, tq, tk):
    t_rows = q.shape[0]
    nq, nk = seq // tq, seq // tk
    kv_per_q = tq // tk
    pairs = [(a, b) for a in range(nq) for b in range((a + 1) * kv_per_q)]
    n_steps = len(pairs)
    qi_tab = jnp.asarray([a for a, _ in pairs], jnp.int32)
    ki_tab = jnp.asarray([b for _, b in pairs], jnp.int32)
    first_tab = jnp.asarray([int(i == 0 or pairs[i - 1][0] != pairs[i][0]) for i in range(n_steps)], jnp.int32)
    last_tab = jnp.asarray([int(i == n_steps - 1 or pairs[i + 1][0] != pairs[i][0]) for i in range(n_steps)],
                           jnp.int32)
    nh = PROMPT_HEADS
    groups = H_A // nh
    wq = nh * DH_A
    grid_spec = pltpu.PrefetchScalarGridSpec(
        num_scalar_prefetch=4,
        grid=(batch, groups, n_steps),
        in_specs=[
            pl.BlockSpec((tq, wq), lambda b, g, t, qt, kt, ft, lt: (b * nq + qt[t], g)),
            pl.BlockSpec((nh, tk, LANES), lambda b, g, t, qt, kt, ft, lt: (g, b * nk + kt[t], 0)),
            pl.BlockSpec((nh, V_ROWS, tk), lambda b, g, t, qt, kt, ft, lt: (g, 0, b * nk + kt[t])),
            pl.BlockSpec((nh // 2, SUBLANES, tq), lambda b, g, t, qt, kt, ft, lt: (g, 0, b * nq + qt[t])),
            pl.BlockSpec(memory_space=pl.ANY),
        ],
        out_specs=(pl.BlockSpec((tq, wq), lambda b, g, t, qt, kt, ft, lt: (b * nq + qt[t], g)),
                   pl.BlockSpec((1, SUBLANES, LANES),
                                lambda b, g, t, qt, kt, ft, lt: ((b * groups + g) * nq + qt[t], 0, 0))),
        scratch_shapes=[pltpu.VMEM((nh, tq, LANES), BF16),
                        pltpu.VMEM((nh, SUBLANES, tq), F32),
                        pltpu.VMEM((nh, V_ROWS, tq), F32),
                        pltpu.VMEM((nh, tk, tq), BF16), pltpu.VMEM((nh, SUBLANES, tq), F32),
                        pltpu.VMEM((nh, V_ROWS, tk), BF16)],
    )
    return pl.pallas_call(
        functools.partial(_fox_prompt_kernel, shifted),
        grid_spec=grid_spec,
        out_shape=(jax.ShapeDtypeStruct((t_rows, D_A), BF16),
                   jax.ShapeDtypeStruct((batch * groups * nq, SUBLANES, LANES), F32)),
        input_output_aliases={8: 0},
        compiler_params=_cparams("arbitrary", "arbitrary", "arbitrary"),
        name="fox_prompt_shifted" if shifted else "fox_prompt",
    )(qi_tab, ki_tab, first_tab, last_tab, q, kaug, vta, dpair, jnp.zeros((t_rows, D_A), BF16))


def _fox_prompt(q, kaug, vta, dpair, batch, seq, tq, tk):
    args = (q, kaug, vta, dpair, batch, seq, tq, tk)
    a_fast, bad = _fox_prompt_call(False, *args)
    return lax.cond(jnp.max(bad) > 0.0, lambda: _fox_prompt_call(True, *args)[0], lambda: a_fast)


def _sample_tile(shifted, q2, k2, v2, bias_a, bias_b, mask, m_ref, l_ref, acc_ref):
    tq = q2.shape[0]
    low = lax.broadcasted_iota(jnp.int32, (tq, LANES), 1) < DH_A
    zero = jnp.zeros_like(q2)
    nt = (((1,), (1,)), ((), ()))
    pvs, alphas = [], []
    for h, (qh, bias) in enumerate(((jnp.where(low, q2, zero), bias_a), (jnp.where(low, zero, q2), bias_b))):
        s = lax.dot_general(qh, k2, nt, preferred_element_type=F32) + bias
        if mask is not None:
            s = jnp.where(mask, s, NEG_BIG)
        if not shifted:
            p = jnp.exp2(s)
            l_ref[h] = l_ref[h] + jnp.sum(p, axis=1, keepdims=True)
            pvs.append(jnp.dot(p.astype(BF16), v2, preferred_element_type=F32))
            continue
        m_prev = m_ref[h]
        m_new = jnp.maximum(m_prev, jnp.max(s, axis=1, keepdims=True))
        alpha = jnp.exp2(m_prev - m_new)
        p = jnp.exp2(s - m_new[:, 0:1])
        l_ref[h] = alpha * l_ref[h] + jnp.sum(p, axis=1, keepdims=True)
        m_ref[h] = m_new
        pvs.append(jnp.dot(p.astype(BF16), v2, preferred_element_type=F32))
        alphas.append(alpha)
    if shifted:
        acc_ref[...] = jnp.where(low, alphas[0], alphas[1]) * acc_ref[...] + jnp.where(low, pvs[0], pvs[1])
    else:
        acc_ref[...] = acc_ref[...] + jnp.where(low, pvs[0], pvs[1])


def _fox_sample_kernel(shifted, n_cache_tiles, q_ref, ck_ref, cv_ref, r_ref, nk_ref, nv_ref, dqc_ref, dqr_ref,
                       a_in, o_ref, bad_ref, m_ref, l_ref, acc_ref):
    del a_in
    s_idx = pl.program_id(1)
    tq = q_ref.shape[0]

    @pl.when(s_idx == 0)
    def _():
        m_ref[...] = jnp.full_like(m_ref, -jnp.inf)
        l_ref[...] = jnp.zeros_like(l_ref)
        acc_ref[...] = jnp.zeros_like(acc_ref)

    dq = dqc_ref[...]

    def pair_refs(p):
        cols = pl.ds(p * LANES, LANES)
        return m_ref.at[p], l_ref.at[p], acc_ref.at[:, cols]

    @pl.when(s_idx < n_cache_tiles)
    def _():
        r = r_ref[0, 0]
        for p in range(HEAD_PAIRS):
            cols = slice(p * LANES, (p + 1) * LANES)
            bias_a = dq[:, 2 * p:2 * p + 1] + r[2 * p:2 * p + 1, :]
            bias_b = dq[:, 2 * p + 1:2 * p + 2] + r[2 * p + 1:2 * p + 2, :]
            _sample_tile(shifted, q_ref[:, cols], ck_ref[0, 0, :, cols].astype(BF16),
                         cv_ref[0, 0, :, cols].astype(BF16), bias_a, bias_b, None, *pair_refs(p))

    @pl.when(s_idx == n_cache_tiles)
    def _():
        dk = dqr_ref[0]
        causal = lax.broadcasted_iota(jnp.int32, (tq, tq), 0) >= lax.broadcasted_iota(jnp.int32, (tq, tq), 1)
        for p in range(HEAD_PAIRS):
            cols = slice(p * LANES, (p + 1) * LANES)
            bias_a = dq[:, 2 * p:2 * p + 1] - dk[2 * p:2 * p + 1, :]
            bias_b = dq[:, 2 * p + 1:2 * p + 2] - dk[2 * p + 1:2 * p + 2, :]
            _sample_tile(shifted, q_ref[:, cols], nk_ref[0, :, cols].astype(BF16), nv_ref[0, :, cols].astype(BF16),
                         bias_a, bias_b, causal, *pair_refs(p))
        low = lax.broadcasted_iota(jnp.int32, (tq, LANES), 1) < DH_A
        bad = jnp.zeros((tq, LANES), F32)
        for p in range(HEAD_PAIRS):
            cols = slice(p * LANES, (p + 1) * LANES)
            acc = acc_ref[:, cols]
            den = jnp.where(low, l_ref[p, 0], l_ref[p, 1])
            bad = jnp.maximum(bad, jnp.where(jnp.logical_and(acc - acc == 0.0, den > 0.0), 0.0, 1.0))
            o_ref[:, cols] = (acc / den).astype(o_ref.dtype)
        bad = jnp.max(jnp.max(bad, axis=0, keepdims=True), axis=1, keepdims=True)
        bad_ref[0] = jnp.broadcast_to(bad, (SUBLANES, LANES))


def _fox_sample_call(shifted, layer, q, cache_k, cache_v, rsuf, ks, vs, dcol, drow_s, a_buf, n_prompt, tile):
    depth, dec_batch, past, _ = cache_k.shape
    dec_seq = ks.shape[1] // dec_batch
    nct = past // tile
    q0 = n_prompt // dec_seq
    ci = lambda b, s: (layer, b, jnp.minimum(s, nct - 1), 0)
    return pl.pallas_call(
        functools.partial(_fox_sample_kernel, shifted, nct),
        grid=(dec_batch, nct + 1),
        in_specs=[
            pl.BlockSpec((dec_seq, D_A), lambda b, s: (q0 + b, 0)),
            pl.BlockSpec((1, 1, tile, D_A), ci),
            pl.BlockSpec((1, 1, tile, D_A), ci),
            pl.BlockSpec((1, 1, H_A, tile), lambda b, s: (layer, b, 0, jnp.minimum(s, nct - 1))),
            pl.BlockSpec((1, dec_seq, D_A), lambda b, s: (layer, b, 0)),
            pl.BlockSpec((1, dec_seq, D_A), lambda b, s: (layer, b, 0)),
            pl.BlockSpec((dec_seq, H_A), lambda b, s: (q0 + b, 0)),
            pl.BlockSpec((1, H_A, dec_seq), lambda b, s: (b, 0, 0)),
            pl.BlockSpec(memory_space=pl.ANY),
        ],
        out_specs=(pl.BlockSpec((dec_seq, D_A), lambda b, s: (q0 + b, 0)),
                   pl.BlockSpec((1, SUBLANES, LANES), lambda b, s: (b, 0, 0))),
        out_shape=(jax.ShapeDtypeStruct(a_buf.shape, a_buf.dtype),
                   jax.ShapeDtypeStruct((dec_batch, SUBLANES, LANES), F32)),
        scratch_shapes=[pltpu.VMEM((HEAD_PAIRS, 2, dec_seq, LANES), F32),
                        pltpu.VMEM((HEAD_PAIRS, 2, dec_seq, LANES), F32),
                        pltpu.VMEM((dec_seq, D_A), F32)],
        input_output_aliases={8: 0},
        compiler_params=_cparams("parallel", "arbitrary"),
        name="fox_sample_shifted" if shifted else "fox_sample",
    )(q, cache_k, cache_v, rsuf, ks, vs, dcol, drow_s, a_buf)


def _fox_sample(*args):
    a_fast, bad = _fox_sample_call(False, *args)
    return lax.cond(jnp.max(bad) > 0.0, lambda: _fox_sample_call(True, *args)[0], lambda: a_fast)


def _suffix_sum_kernel(x_ref, u_ref, o_ref, carry_ref):
    j = pl.program_id(0)

    @pl.when(j == 0)
    def _():
        carry_ref[...] = jnp.zeros_like(carry_ref)

    x = x_ref[...]
    hi, mid, lo = _split3(x)
    u = u_ref[...]
    loc = (jnp.dot(hi, u, preferred_element_type=F32) + jnp.dot(mid, u, preferred_element_type=F32)
           + jnp.dot(lo, u, preferred_element_type=F32))
    o_ref[...] = (loc + carry_ref[:, 0:1]) * LOG2E
    carry_ref[...] = carry_ref[...] + jnp.sum(x, axis=1, keepdims=True)


def _suffix_sums(x):
    rows, n = x.shape
    tb = min(512, n)
    nb = n // tb
    u = (lax.broadcasted_iota(jnp.int32, (tb, tb), 0) > lax.broadcasted_iota(jnp.int32, (tb, tb), 1)).astype(BF16)
    return pl.pallas_call(
        _suffix_sum_kernel,
        grid=(nb,),
        in_specs=[pl.BlockSpec((rows, tb), lambda j: (0, nb - 1 - j)),
                  pl.BlockSpec((tb, tb), lambda j: (0, 0))],
        out_specs=pl.BlockSpec((rows, tb), lambda j: (0, nb - 1 - j)),
        out_shape=jax.ShapeDtypeStruct((rows, n), F32),
        scratch_shapes=[pltpu.VMEM((rows, LANES), F32)],
        compiler_params=_cparams("arbitrary"),
        name="cache_suffix_sums",
    )(x, u)


def _out_proj_kernel(alpha, a_ref, bn_ref, x_ref, gna_ref, wo_ref, g1_ref, b1_ref, wrh_ref, wrl_ref, br_ref, ls_ref,
                     x1_ref, x1b_ref, route_ref, counts_ref, carry_ref):
    i = pl.program_id(0)
    tm = x_ref.shape[0]

    @pl.when(i == 0)
    def _():
        carry_ref[...] = jnp.zeros_like(carry_ref)

    an = _rms_norm(a_ref[...].astype(F32), gna_ref[...]).astype(BF16)
    mix = (jnp.dot(an, wo_ref[0:D_A, :], preferred_element_type=F32)
           + jnp.dot(bn_ref[...], wo_ref[D_A:, :], preferred_element_type=F32))
    x1 = _layer_norm(alpha * x_ref[...] + mix, g1_ref[...], b1_ref[...])
    x1_ref[...] = x1
    x1h = x1.astype(BF16)
    x1b_ref[...] = _pack_halves(x1)

    x1l = (x1 - x1h.astype(F32)).astype(BF16)
    logits = (jnp.dot(x1h, wrh_ref[...], preferred_element_type=F32)
              + jnp.dot(x1l, wrh_ref[...], preferred_element_type=F32)
              + jnp.dot(x1h, wrl_ref[...], preferred_element_type=F32)) + br_ref[...]
    lane = lax.broadcasted_iota(jnp.int32, (tm, LANES), 1)
    is_group = lane < N_GROUPS
    gl = jnp.where(is_group, logits, NEG_BIG)
    gmax = jnp.max(gl, axis=1, keepdims=True)
    g_idx = jnp.min(jnp.where(gl == gmax, lane, LANES), axis=1, keepdims=True)
    g_prob = 1.0 / jnp.sum(jnp.exp(gl - gmax), axis=1, keepdims=True)
    in_group = jnp.logical_and(lane >= N_GROUPS, lane < N_GROUPS + N_EXPERTS)
    in_group = jnp.logical_and(in_group, lax.shift_right_arithmetic(lane - N_GROUPS, 3) == g_idx)
    el = jnp.where(in_group, logits, NEG_BIG)
    e1 = jnp.max(el, axis=1, keepdims=True)
    i1 = jnp.min(jnp.where(el == e1, lane, LANES), axis=1, keepdims=True)
    el2 = jnp.where(lane == i1, NEG_BIG, el)
    e2 = jnp.max(el2, axis=1, keepdims=True)
    i2 = jnp.min(jnp.where(el2 == e2, lane, LANES), axis=1, keepdims=True)
    t2 = jnp.exp(e2 - e1)
    w1 = 1.0 / (1.0 + t2)
    gate1 = g_prob * w1
    gate2 = g_prob * (t2 * w1)

    oh1 = lane == i1
    oh2 = lane == i2
    both = (oh1.astype(F32) + oh2.astype(F32)).astype(BF16)
    before = jnp.dot(ls_ref[...], both, preferred_element_type=F32) + carry_ref[...]
    rank1 = jnp.sum(jnp.where(oh1, before, 0.0), axis=1, keepdims=True)
    rank2 = jnp.sum(jnp.where(oh2, before, 0.0), axis=1, keepdims=True)
    carry_ref[...] = carry_ref[...] + jnp.sum(both.astype(F32), axis=0, keepdims=True)
    counts_ref[...] = carry_ref[...]

    route = jnp.where(lane == 0, (i1 - N_GROUPS).astype(F32), 0.0)
    route = jnp.where(lane == 1, (i2 - N_GROUPS).astype(F32), route)
    route = jnp.where(lane == 2, rank1, route)
    route = jnp.where(lane == 3, rank2, route)
    route = jnp.where(lane == 4, gate1, route)
    route = jnp.where(lane == 5, gate2, route)
    route_ref[...] = route


def _out_proj(alpha, a, bn, x, gna, wo, g1, b1, wrh, wrl, br, lstrict):
    t_rows = x.shape[0]
    tm = TOKEN_TILE
    full = lambda arr: pl.BlockSpec(arr.shape, lambda i: (0,) * arr.ndim)
    row = lambda w: pl.BlockSpec((tm, w), lambda i: (i, 0))
    return pl.pallas_call(
        functools.partial(_out_proj_kernel, alpha),
        grid=(t_rows // tm,),
        in_specs=[row(D_A), row(D_B), row(D_MODEL), full(gna), full(wo), full(g1), full(b1), full(wrh), full(wrl),
                  full(br), full(lstrict)],
        out_specs=(row(D_MODEL), row(HALF_MODEL), row(LANES), pl.BlockSpec((1, LANES), lambda i: (0, 0))),
        out_shape=(jax.ShapeDtypeStruct((t_rows, D_MODEL), F32),
                   jax.ShapeDtypeStruct((t_rows, HALF_MODEL), jnp.int32),
                   jax.ShapeDtypeStruct((t_rows, LANES), F32),
                   jax.ShapeDtypeStruct((1, LANES), F32)),
        scratch_shapes=[pltpu.VMEM((1, LANES), F32)],
        compiler_params=_cparams("arbitrary"),
        name="out_proj_router",
    )(a, bn, x, gna, wo, g1, b1, wrh, wrl, br, lstrict)


def _pack_halves(x):
    lo = lax.bitcast_convert_type(x[:, :HALF_MODEL].astype(BF16).astype(F32), jnp.uint32)
    hi = lax.bitcast_convert_type(x[:, HALF_MODEL:].astype(BF16).astype(F32), jnp.uint32)
    return lax.bitcast_convert_type((hi & jnp.uint32(0xFFFF0000)) | (lo >> 16), jnp.int32)


def _unpack_halves(w):
    u = lax.bitcast_convert_type(w, jnp.uint32)
    return (lax.bitcast_convert_type(u << 16, F32), lax.bitcast_convert_type(u & jnp.uint32(0xFFFF0000), F32))


def _expert_kernel(ib_ref, ie_ref, ni_ref, st_ref, en_ref, x_ref, w1_ref, w3_ref, w2_ref, o_ref,
                   w1b_ref, w3b_ref, w2b_ref):
    i = pl.program_id(0)
    bm = x_ref.shape[0]

    @pl.when(i < ni_ref[0])
    def _():
        e = ie_ref[i]
        b = ib_ref[i]
        prev = jnp.maximum(i - 1, 0)
        first_item = i == 0

        @pl.when(jnp.logical_or(first_item, ie_ref[prev] != e))
        def _():
            w1b_ref[...] = w1_ref[0, 0].astype(BF16)
            w3b_ref[...] = w3_ref[0, 0].astype(BF16)
            w2b_ref[...] = w2_ref[0, 0].astype(BF16)

        x_lo, x_hi = (v.astype(BF16) for v in _unpack_halves(x_ref[...]))
        h1 = (jnp.dot(x_lo, w1b_ref[:HALF_MODEL, :], preferred_element_type=F32)
              + jnp.dot(x_hi, w1b_ref[HALF_MODEL:, :], preferred_element_type=F32))
        h3 = (jnp.dot(x_lo, w3b_ref[:HALF_MODEL, :], preferred_element_type=F32)
              + jnp.dot(x_hi, w3b_ref[HALF_MODEL:, :], preferred_element_type=F32))
        h = (h1 * (1.0 / (1.0 + jnp.exp(-h1))) * h3).astype(BF16)
        y = _pack_halves(jnp.dot(h, w2b_ref[...], preferred_element_type=F32))
        first_visit = jnp.logical_or(first_item, ib_ref[prev] != b)

        @pl.when(first_visit)
        def _():
            o_ref[...] = y

        @pl.when(jnp.logical_not(first_visit))
        def _():
            row = b * bm + lax.broadcasted_iota(jnp.int32, (bm, 1), 0)
            mine = jnp.logical_and(row >= st_ref[e], row < en_ref[e])
            o_ref[...] = jnp.where(mine, y, o_ref[...])


def _experts(layer, item_block, item_expert, n_items, starts, ends, xs, w1, w3, w2):
    n_rows = xs.shape[0]
    bm = MOE_ROWS
    n_max = item_block.shape[0]
    item = lambda i, ni: jnp.minimum(i, ni[0] - 1)
    blk = lambda i, ib, ie, ni, st, en: (ib[item(i, ni)], 0)
    wsel = lambda i, ib, ie, ni, st, en: (layer, ie[item(i, ni)], 0, 0)
    grid_spec = pltpu.PrefetchScalarGridSpec(
        num_scalar_prefetch=5,
        grid=(n_max,),
        in_specs=[pl.BlockSpec((bm, HALF_MODEL), blk),
                  pl.BlockSpec((1, 1, D_MODEL, D_EXPERT), wsel),
                  pl.BlockSpec((1, 1, D_MODEL, D_EXPERT), wsel),
                  pl.BlockSpec((1, 1, D_EXPERT, D_MODEL), wsel)],
        out_specs=pl.BlockSpec((bm, HALF_MODEL), blk),
        scratch_shapes=[pltpu.VMEM((D_MODEL, D_EXPERT), BF16), pltpu.VMEM((D_MODEL, D_EXPERT), BF16),
                        pltpu.VMEM((D_EXPERT, D_MODEL), BF16)],
    )
    return pl.pallas_call(
        _expert_kernel,
        grid_spec=grid_spec,
        out_shape=jax.ShapeDtypeStruct((n_rows, HALF_MODEL), jnp.int32),
        compiler_params=_cparams("arbitrary"),
        name="experts",
    )(item_block, item_expert, n_items, starts, ends, xs, w1, w3, w2)


def _combine_kernel(alpha, n_first, x1_ref, y0_ref, y1_ref, route_ref, g2_ref, b2_ref, *o_refs):
    route = route_ref[...]
    g0, g1 = route[:, 4:5], route[:, 5:6]
    y0_lo, y0_hi = _unpack_halves(y0_ref[...])
    y1_lo, y1_hi = _unpack_halves(y1_ref[...])
    z = jnp.concatenate([alpha * x1_ref[:, :HALF_MODEL] + (g0 * y0_lo + g1 * y1_lo),
                         alpha * x1_ref[:, HALF_MODEL:] + (g0 * y0_hi + g1 * y1_hi)], axis=1)
    out = _layer_norm(z, g2_ref[...], b2_ref[...])
    if n_first is None:
        o_refs[0][...] = out
    else:
        first = pl.program_id(0) < n_first

        @pl.when(first)
        def _():
            o_refs[0][...] = out

        @pl.when(jnp.logical_not(first))
        def _():
            o_refs[1][...] = out


def _combine(alpha, x1, y01, route, g2, b2, split_rows=None):
    t_rows = x1.shape[0]
    tm = TOKEN_TILE
    n_tiles = t_rows // tm
    full = lambda arr: pl.BlockSpec(arr.shape, lambda i: (0,) * arr.ndim)
    row = lambda w: pl.BlockSpec((tm, w), lambda i: (i, 0))
    if split_rows is None:
        n_first = None
        out_specs = row(D_MODEL)
        out_shape = jax.ShapeDtypeStruct((t_rows, D_MODEL), F32)
    else:
        n_first = split_rows // tm
        out_specs = (pl.BlockSpec((tm, D_MODEL), lambda i: (jnp.minimum(i, n_first - 1), 0)),
                     pl.BlockSpec((tm, D_MODEL), lambda i: (jnp.maximum(i - n_first, 0), 0)))
        out_shape = (jax.ShapeDtypeStruct((split_rows, D_MODEL), F32),
                     jax.ShapeDtypeStruct((t_rows - split_rows, D_MODEL), F32))
    return pl.pallas_call(
        functools.partial(_combine_kernel, alpha, n_first),
        grid=(n_tiles,),
        in_specs=[row(D_MODEL), row(HALF_MODEL), pl.BlockSpec((tm, HALF_MODEL), lambda i: (i + n_tiles, 0)),
                  row(LANES), full(g2), full(b2)],
        out_specs=out_specs,
        out_shape=out_shape,
        compiler_params=_cparams("arbitrary"),
        name="moe_combine",
    )(x1, y01, y01, route, g2, b2)


def _gather_rows(table, idx):
    n_rows, width = idx.shape[0], table.shape[1]
    workers = SC_CORES * SC_SUBCORES
    step = workers * SC_GATHER_ROWS
    n_pad = -(-n_rows // step) * step
    if n_pad != n_rows:
        idx = jnp.pad(idx, (0, n_pad - n_rows))
    per_worker = n_pad // workers
    n_chunks = per_worker // SC_GATHER_ROWS
    mesh = plsc.VectorSubcoreMesh(core_axis_name="c", subcore_axis_name="s")

    @functools.partial(
        pl.kernel, mesh=mesh,
        out_type=jax.ShapeDtypeStruct((n_pad, width), table.dtype),
        scratch_types=[pltpu.VMEM((SC_GATHER_ROWS,), jnp.int32),
                       pltpu.VMEM((SC_GATHER_ROWS, width), table.dtype),
                       pltpu.SemaphoreType.DMA],
        name="sc_gather_rows",
    )
    def gather(table_hbm, idx_hbm, out_hbm, idx_v, rows_v, sem):
        base = (lax.axis_index("s") * SC_CORES + lax.axis_index("c")) * per_worker

        @pl.loop(0, n_chunks)
        def _(c):
            off = base + c * SC_GATHER_ROWS
            pltpu.sync_copy(idx_hbm.at[pl.ds(off, SC_GATHER_ROWS)], idx_v)
            pltpu.async_copy(table_hbm.at[idx_v], rows_v, sem).wait()
            pltpu.sync_copy(rows_v, out_hbm.at[pl.ds(off, SC_GATHER_ROWS)])

    out = gather(table, idx)
    return out if n_pad == n_rows else out[:n_rows]


def _attention_tile(seq):
    for t in (512, 256, 128):
        if seq % t == 0:
            return t
    raise ValueError("sequence length must be a multiple of 128")


def kernel(x_prompt, x_sample, cache_k, cache_v, cache_logf, w_in, b_f, ln_v_g, ln_v_b, w_s, b_s,
           g_norm_a, g_norm_b, w_out, ln1_g, ln1_b, w_gr, b_gr, w_er, b_er, w1, w3, w2, ln2_g, ln2_b):
    batch, seq, _ = x_prompt.shape
    dec_batch, dec_seq, _ = x_sample.shape
    depth = w_in.shape[0]
    past = cache_k.shape[2]
    n_prompt = batch * seq
    n_sample = dec_batch * dec_seq
    t_rows = n_prompt + n_sample
    alpha = float((2 * depth) ** 0.25)
    tm = TOKEN_TILE
    assert seq % tm == 0 and n_sample % tm == 0 and tm % dec_seq == 0 and dec_seq == GMLP_CHUNK // 2
    assert past % 128 == 0

    x = jnp.concatenate([x_prompt.reshape(n_prompt, D_MODEL), x_sample.reshape(n_sample, D_MODEL)], axis=0)

    sp = (D_A, 2 * D_A, 3 * D_A, 3 * D_A + H_A, 3 * D_A + H_A + D_B)
    wq, wk, wv, wf, wu, wgv = (w_in[..., a:b] for a, b in zip((0,) + sp, sp + (w_in.shape[-1],)))
    wf_pad = jnp.pad(wf, ((0, 0), (0, 0), (0, LANES - H_A)))
    w_cat = jnp.concatenate([wq * (LOG2E * DH_A ** -0.5), wk, wv, wu, wgv, wf_pad], axis=-1).astype(BF16)
    bf_pad = jnp.pad(b_f, ((0, 0), (0, LANES - H_A)))[:, None, :]
    half = GMLP_CHUNK // 2
    wmix = jnp.stack([w_s, jnp.tile(w_s[:, :, :half, :half], (1, 1, 2, 2))], axis=1)
    bs_t = jnp.swapaxes(b_s, 1, 2)
    bs_var = jnp.stack([bs_t, jnp.tile(bs_t[:, :half], (1, 2, 1))], axis=1)
    bmix = jnp.repeat(bs_var, C_B, axis=-1)
    wo_b = w_out.astype(BF16)
    wr = jnp.pad(jnp.concatenate([w_gr, w_er], axis=-1), ((0, 0), (0, 0), (0, LANES - N_GROUPS - N_EXPERTS)))
    wrh = wr.astype(BF16)
    wrl = (wr - wrh.astype(F32)).astype(BF16)
    br = jnp.pad(jnp.concatenate([b_gr, b_er], axis=-1), ((0, 0), (0, LANES - N_GROUPS - N_EXPERTS)))[:, None, :]
    row2 = lambda a: a[:, None, :]

    ri = lax.broadcasted_iota(jnp.int32, (tm, tm), 0)
    ci = lax.broadcasted_iota(jnp.int32, (tm, tm), 1)
    tri = jnp.stack([ri >= ci, jnp.logical_and(ri >= ci, ri // dec_seq == ci // dec_seq)]).astype(BF16)
    lstrict = (ri > ci).astype(BF16)
    prow = lax.broadcasted_iota(jnp.int32, (LANES, H_A * LANES), 0)
    pcol = lax.broadcasted_iota(jnp.int32, (LANES, H_A * LANES), 1)
    phead = pcol // LANES
    poff = pcol % LANES - jnp.where(phead % 2 == 0, DH_A, 0)
    is_one = jnp.logical_and(prow == 3 * H_A, jnp.logical_and(poff >= 0, poff < 3))
    is_part = jnp.logical_and(jnp.logical_and(poff >= 3, poff < 6), prow == (poff - 3) * H_A + phead)
    place = jnp.logical_or(is_one, is_part).astype(BF16)

    clf = jnp.transpose(cache_logf, (0, 1, 3, 2)).reshape(depth * dec_batch * H_A, past)
    rsuf = _suffix_sums(clf).reshape(depth, dec_batch, H_A, past)
    ck = cache_k.reshape(depth, dec_batch, past, D_A)
    cv = cache_v.reshape(depth, dec_batch, past, D_A)

    kp = jnp.zeros((depth, n_prompt, D_A), F32)
    vp = jnp.zeros((depth, n_prompt, D_A), F32)
    ks = jnp.zeros((depth, n_sample, D_A), F32)
    vs = jnp.zeros((depth, n_sample, D_A), F32)
    logfs, gvns = [], []

    bm = MOE_ROWS
    n_assign = 2 * t_rows
    nblk = n_assign // bm
    expert_ids = jnp.arange(N_EXPERTS, dtype=jnp.int32)
    item_ids = jnp.arange(nblk + N_EXPERTS - 1, dtype=jnp.int32)
    q_tile = PROMPT_Q_TILE if seq % PROMPT_Q_TILE == 0 else _attention_tile(seq)
    k_tile = PROMPT_K_TILE if q_tile % PROMPT_K_TILE == 0 else q_tile
    c_tile = SAMPLE_K_TILE if past % SAMPLE_K_TILE == 0 else _attention_tile(past)
    assign_ids = jnp.arange(n_assign, dtype=jnp.int32)

    for l in range(depth):
        q, kp, vp, ks, vs, kaug, vta, logf, dT, dcol, bn, gvn = _in_proj(
            l, x, w_cat[l], bf_pad[l], row2(ln_v_g)[l], row2(ln_v_b)[l], wmix[l], bmix[l], row2(g_norm_b)[l], tri,
            place, kp, vp, ks, vs, n_prompt, seq)
        logfs.append(logf)
        gvns.append(gvn)

        dpair = jnp.pad(dT.reshape(HEAD_PAIRS, 2, t_rows), ((0, 0), (0, SUBLANES - 2), (0, 0)))
        a = _fox_prompt(q, kaug, vta, dpair, batch, seq, q_tile, k_tile)
        drow_s = jnp.transpose(dT[:, n_prompt:].reshape(H_A, dec_batch, dec_seq), (1, 0, 2))
        a = _fox_sample(l, q, ck, cv, rsuf, ks, vs, dcol, drow_s, a, n_prompt, c_tile)

        x1, x1b, route, counts = _out_proj(alpha, a, bn, x, row2(g_norm_a)[l], wo_b[l], row2(ln1_g)[l],
                                           row2(ln1_b)[l], wrh[l], wrl[l], br[l], lstrict)

        cnt = counts[0, N_GROUPS:N_GROUPS + N_EXPERTS].astype(jnp.int32)
        ends = jnp.cumsum(cnt)
        starts = ends - cnt
        eid = route[:, 0:2].astype(jnp.int32)
        rank = route[:, 2:4].astype(jnp.int32)
        onehot = eid[:, :, None] == expert_ids[None, None, :]
        pos = jnp.sum(jnp.where(onehot, starts[None, None, :], 0), axis=-1) + rank
        order = jnp.sort(eid.reshape(n_assign) * n_assign + assign_ids) % n_assign
        first_blk = starts // bm
        n_it = jnp.where(cnt > 0, (ends - 1) // bm - first_blk + 1, 0)
        it_end = jnp.cumsum(n_it)
        item_expert = jnp.minimum(jnp.sum(item_ids[:, None] >= it_end[None, :], axis=1), N_EXPERTS - 1).astype(jnp.int32)
        it_first = jnp.sum(jnp.where(item_expert[:, None] == expert_ids[None, :], (it_end - n_it)[None, :], 0), axis=1)
        it_blk0 = jnp.sum(jnp.where(item_expert[:, None] == expert_ids[None, :], first_blk[None, :], 0), axis=1)
        item_block = jnp.clip(it_blk0 + item_ids - it_first, 0, nblk - 1).astype(jnp.int32)

        xs = _gather_rows(x1b, order // 2)
        yb = _experts(l, item_block, item_expert, it_end[-1:].astype(jnp.int32), starts, ends, xs, w1, w3, w2)
        y01 = _gather_rows(yb, jnp.concatenate([pos[:, 0], pos[:, 1]]))
        x = _combine(alpha, x1, y01, route, row2(ln2_g)[l], row2(ln2_b)[l],
                     split_rows=n_prompt if l == depth - 1 else None)

    y_prompt = x[0].reshape(batch, seq, D_MODEL)
    y_sample = x[1].reshape(dec_batch, dec_seq, D_MODEL)
    logf_all = jnp.stack(logfs)
    return (y_prompt, y_sample,
            kp.reshape(depth, batch, seq, H_A, DH_A), vp.reshape(depth, batch, seq, H_A, DH_A),
            logf_all[:, :n_prompt].reshape(depth, batch, seq, H_A),
            ks.reshape(depth, dec_batch, dec_seq, H_A, DH_A), vs.reshape(depth, dec_batch, dec_seq, H_A, DH_A),
            logf_all[:, n_prompt:].reshape(depth, dec_batch, dec_seq, H_A),
            jnp.stack(gvns).reshape(depth, dec_batch, dec_seq, D_B))
```

```python
import functools

import jax
import jax.numpy as jnp
from jax import lax
from jax.experimental import pallas as pl
from jax.experimental.pallas import tpu as pltpu
from jax.experimental.pallas import tpu_sc as plsc

F32 = jnp.float32
BF16 = jnp.bfloat16

D_MODEL = 1024
D_A = 512
H_A = 8
DH_A = 64
D_B = 512
G_B = 8
C_B = 64
GMLP_CHUNK = 128
N_GROUPS = 4
EXPERTS_PER_GROUP = 8
N_EXPERTS = N_GROUPS * EXPERTS_PER_GROUP
D_EXPERT = 512
LN_EPS = 1e-5
HEAD_PAIRS = H_A // 2
HALF_MODEL = D_MODEL // 2

LANES = 128
SUBLANES = 8
VMEM_LIMIT_BYTES = 56 * 1024 * 1024
SC_CORES = 2
SC_SUBCORES = 16
SC_GATHER_ROWS = 128

TOKEN_TILE = 512
MOE_ROWS = 512
PROMPT_Q_TILE = 1024
PROMPT_K_TILE = 1024
SAMPLE_K_TILE = 2048
PROMPT_HEADS = 8
V_ROWS = LANES
NEG_BIG = -1e30
LOG2E = 1.4426950408889634


def _cparams(*sem):
    return pltpu.CompilerParams(dimension_semantics=sem, vmem_limit_bytes=VMEM_LIMIT_BYTES)


def _split3(x):
    hi = x.astype(BF16)
    r1 = x - hi.astype(F32)
    mid = r1.astype(BF16)
    lo = (r1 - mid.astype(F32)).astype(BF16)
    return hi, mid, lo


def _gelu_tanh(x):
    return 0.5 * x * (1.0 + jnp.tanh(0.7978845608028654 * (x + 0.044715 * (x * x * x))))


def _log_sigmoid(z):
    return jnp.minimum(z, 0.0) - jnp.log(1.0 + jnp.exp(-jnp.abs(z)))


def _layer_norm(x, g, b):
    mu = jnp.mean(x, axis=-1, keepdims=True)
    xc = x - mu
    var = jnp.mean(xc * xc, axis=-1, keepdims=True)
    return xc * lax.rsqrt(var + LN_EPS) * g + b


def _rms_norm(x, g):
    return x * lax.rsqrt(jnp.mean(x * x, axis=-1, keepdims=True) + LN_EPS) * g


def _in_proj_kernel(n_prompt_tiles, tiles_per_seq,
                    x_ref, w_ref, bf_ref, lnvg_ref, lnvb_ref, wmix_ref, bmix_ref, gnb_ref, tri_ref, place_ref,
                    kp_in, vp_in, ks_in, vs_in,
                    q_ref, kp_ref, vp_ref, ks_ref, vs_ref, kaug_ref, vta_ref, logf_ref, dT_ref, dcol_ref, bn_ref,
                    gvn_ref, carry_ref):
    del kp_in, vp_in, ks_in, vs_in
    i = pl.program_id(0)
    tm = x_ref.shape[0]
    is_sample = i >= n_prompt_tiles
    var = is_sample.astype(jnp.int32)

    p = jnp.dot(x_ref[...].astype(BF16), w_ref[...], preferred_element_type=F32)
    q = p[:, 0:D_A]
    k = p[:, D_A:2 * D_A]
    v = p[:, 2 * D_A:3 * D_A]
    u = p[:, 3 * D_A:3 * D_A + D_B]
    gv = p[:, 3 * D_A + D_B:3 * D_A + 2 * D_B]
    fl = p[:, 3 * D_A + 2 * D_B:]

    q_ref[...] = q.astype(BF16)
    tail = jnp.where(lax.broadcasted_iota(jnp.int32, (V_ROWS - DH_A, tm), 0) == 0, 1.0, 0.0)
    for j in range(HEAD_PAIRS):
        vt_pair = v[:, j * LANES:(j + 1) * LANES].T
        vta_ref[2 * j] = jnp.concatenate([vt_pair[0:DH_A], tail], axis=0).astype(BF16)
        vta_ref[2 * j + 1] = jnp.concatenate([vt_pair[DH_A:], tail], axis=0).astype(BF16)

    ks_ref[0] = k
    vs_ref[0] = v

    logf = _log_sigmoid(fl + bf_ref[...])
    logf_ref[...] = logf[:, 0:H_A]
    hi, mid, lo = _split3(logf)
    parts = jnp.concatenate([hi, mid, lo], axis=1)
    cs = jnp.dot(tri_ref[var], parts, preferred_element_type=F32)
    cs = cs[:, 0:LANES] + cs[:, LANES:2 * LANES] + cs[:, 2 * LANES:3 * LANES]

    restart = jnp.logical_or(is_sample, i % tiles_per_seq == 0)
    d = cs + jnp.where(restart, 0.0, carry_ref[...])
    carry_ref[...] = d[tm - 1:tm, :]
    d2 = d * LOG2E
    dcol_ref[...] = d2[:, 0:H_A]
    dT_ref[...] = d2.T[0:H_A, :]

    nh, nm, nl = _split3(-d2)
    lane_t = lax.broadcasted_iota(jnp.int32, (tm, LANES), 1)
    dparts = jnp.where(lane_t < H_A, nh,
                       jnp.where(lane_t < 2 * H_A, pltpu.roll(nm, H_A, 1),
                                 jnp.where(lane_t < 3 * H_A, pltpu.roll(nl, 2 * H_A, 1),
                                           jnp.where(lane_t == 3 * H_A, 1.0, 0.0).astype(BF16))))
    aug = jnp.dot(dparts, place_ref[...], preferred_element_type=F32)
    low_t = lane_t < DH_A
    for h in range(H_A):
        kh = k[:, (h // 2) * LANES:(h // 2 + 1) * LANES]
        own = low_t if h % 2 == 0 else jnp.logical_not(low_t)
        kaug_ref[h] = jnp.where(own, kh, aug[:, h * LANES:(h + 1) * LANES]).astype(BF16)

    ug = _gelu_tanh(u)
    vn = _layer_norm(_gelu_tanh(gv), lnvg_ref[...], lnvb_ref[...])

    gvn_ref[...] = vn

    r_io = lax.broadcasted_iota(jnp.int32, (GMLP_CHUNK, GMLP_CHUNK), 0)
    c_io = lax.broadcasted_iota(jnp.int32, (GMLP_CHUNK, GMLP_CHUNK), 1)
    causal = r_io >= c_io
    half = GMLP_CHUNK // 2
    same_half = (r_io >= half) == (c_io >= half)
    keep = jnp.logical_and(causal, jnp.logical_or(jnp.logical_not(is_sample), same_half))
    lane = lax.broadcasted_iota(jnp.int32, (GMLP_CHUNK, LANES), 1)
    low_lanes = lane < C_B
    vnb = vn.astype(BF16)
    bias = bmix_ref[var]
    mixed_rows = []
    for r in range(tm // GMLP_CHUNK):
        rows = slice(r * GMLP_CHUNK, (r + 1) * GMLP_CHUNK)
        cols_out = []
        for j in range(G_B // 2):
            vj = vnb[rows, j * LANES:(j + 1) * LANES]
            m0 = jnp.where(keep, wmix_ref[var, 2 * j], 0.0).astype(BF16)
            m1 = jnp.where(keep, wmix_ref[var, 2 * j + 1], 0.0).astype(BF16)
            y0 = jnp.dot(m0, vj, preferred_element_type=F32)
            y1 = jnp.dot(m1, vj, preferred_element_type=F32)
            cols_out.append(jnp.where(low_lanes, y0, y1))
        mixed_rows.append(jnp.concatenate(cols_out, axis=1) + bias)
    mixed = jnp.concatenate(mixed_rows, axis=0)
    b_out = ug * mixed
    bn_ref[...] = _rms_norm(b_out, gnb_ref[...]).astype(BF16)

    @pl.when(jnp.logical_not(is_sample))
    def _():
        kp_ref[0] = k
        vp_ref[0] = v


def _in_proj(layer, x, w, bf, lnvg, lnvb, wmix, bmix, gnb, tri, place, kp, vp, ks, vs, n_prompt, seq):
    t_rows = x.shape[0]
    tm = TOKEN_TILE
    n_tiles = t_rows // tm
    npt = n_prompt // tm
    n_sample = t_rows - n_prompt
    tiles_per_seq = seq // tm
    full = lambda a: pl.BlockSpec(a.shape, lambda i: (0,) * a.ndim)
    any_spec = pl.BlockSpec(memory_space=pl.ANY)
    p_idx = lambda i: (layer, jnp.minimum(i, npt - 1), 0)
    s_idx = lambda i: (layer, jnp.maximum(i - npt, 0), 0)
    out_shapes = (
        jax.ShapeDtypeStruct((t_rows, D_A), BF16),
        jax.ShapeDtypeStruct(kp.shape, F32),
        jax.ShapeDtypeStruct(vp.shape, F32),
        jax.ShapeDtypeStruct(ks.shape, F32),
        jax.ShapeDtypeStruct(vs.shape, F32),
        jax.ShapeDtypeStruct((H_A, t_rows, LANES), BF16),
        jax.ShapeDtypeStruct((H_A, V_ROWS, t_rows), BF16),
        jax.ShapeDtypeStruct((t_rows, H_A), F32),
        jax.ShapeDtypeStruct((H_A, t_rows), F32),
        jax.ShapeDtypeStruct((t_rows, H_A), F32),
        jax.ShapeDtypeStruct((t_rows, D_B), BF16),
        jax.ShapeDtypeStruct((n_sample, D_B), F32),
    )
    out_specs = (
        pl.BlockSpec((tm, D_A), lambda i: (i, 0)),
        pl.BlockSpec((1, tm, D_A), p_idx),
        pl.BlockSpec((1, tm, D_A), p_idx),
        pl.BlockSpec((1, tm, D_A), s_idx),
        pl.BlockSpec((1, tm, D_A), s_idx),
        pl.BlockSpec((H_A, tm, LANES), lambda i: (0, i, 0)),
        pl.BlockSpec((H_A, V_ROWS, tm), lambda i: (0, 0, i)),
        pl.BlockSpec((tm, H_A), lambda i: (i, 0)),
        pl.BlockSpec((H_A, tm), lambda i: (0, i)),
        pl.BlockSpec((tm, H_A), lambda i: (i, 0)),
        pl.BlockSpec((tm, D_B), lambda i: (i, 0)),
        pl.BlockSpec((tm, D_B), lambda i: (jnp.maximum(i - npt, 0), 0)),
    )
    in_specs = [pl.BlockSpec((tm, D_MODEL), lambda i: (i, 0)),
                full(w), full(bf), full(lnvg), full(lnvb), full(wmix), full(bmix), full(gnb), full(tri), full(place),
                any_spec, any_spec, any_spec, any_spec]
    return pl.pallas_call(
        functools.partial(_in_proj_kernel, npt, tiles_per_seq),
        grid=(n_tiles,),
        in_specs=in_specs,
        out_specs=out_specs,
        out_shape=out_shapes,
        scratch_shapes=[pltpu.VMEM((1, LANES), F32)],
        input_output_aliases={10: 1, 11: 2, 12: 3, 13: 4},
        compiler_params=_cparams("arbitrary"),
        name="in_proj",
    )(x, w, bf, lnvg, lnvb, wmix, bmix, gnb, tri, place, kp, vp, ks, vs)


def _prompt_tile(qa_ref, ka, vt_prev, key_off, masked, shifted, m_ref, acc_ref, p_ref, alpha_ref):
    tk, tq = ka.shape[1], qa_ref.shape[1]
    nt = (((1,), (1,)), ((), ()))
    if masked:
        rel = lax.broadcasted_iota(jnp.int32, (tk, tq), 0) - lax.broadcasted_iota(jnp.int32, (tk, tq), 1)
        keep = rel <= key_off
    _prompt_pv(vt_prev, p_ref, alpha_ref if shifted else None, acc_ref)
    for h in range(PROMPT_HEADS):
        s = lax.dot_general(ka[h], qa_ref[h], nt, preferred_element_type=F32)
        if masked:
            s = jnp.where(keep, s, NEG_BIG)
        if not shifted:
            p_ref[h] = jnp.exp2(s).astype(BF16)
            continue
        m_prev = m_ref[h]
        m_new = jnp.maximum(m_prev, jnp.max(s, axis=0, keepdims=True))
        m_ref[h] = m_new
        p_ref[h] = jnp.exp2(s - m_new[0:1, :]).astype(BF16)
        alpha_ref[h] = jnp.exp2(m_prev - m_new)


def _prompt_pv(vt, p_ref, alpha_ref, acc_ref):
    for h in range(PROMPT_HEADS):
        pv = jnp.dot(vt[h], p_ref[h], preferred_element_type=F32)
        acc_ref[h] = acc_ref[h] + pv if alpha_ref is None else alpha_ref[h, 0:1, :] * acc_ref[h] + pv


def _fox_prompt_kernel(shifted, qi_tab, ki_tab, first_tab, last_tab, q_ref, ka_ref, vt_ref, dq_ref, a_in,
                       o_ref, bad_ref, qa_ref, m_ref, acc_ref, p_ref, alpha_ref, vtp_ref):
    del a_in
    t = pl.program_id(2)
    qi = qi_tab[t]
    ki = ki_tab[t]
    tq = q_ref.shape[0]
    tk = ka_ref.shape[1]

    @pl.when(jnp.logical_and(t == 0, jnp.logical_and(pl.program_id(0) == 0, pl.program_id(1) == 0)))
    def _():
        p_ref[...] = jnp.zeros_like(p_ref)

    @pl.when(first_tab[t] == 1)
    def _():
        m_ref[...] = jnp.full_like(m_ref, -jnp.inf)
        acc_ref[...] = jnp.zeros_like(acc_ref)
        alpha_ref[...] = jnp.zeros_like(alpha_ref)
        vtp_ref[...] = jnp.zeros_like(vtp_ref)
        lane = lax.broadcasted_iota(jnp.int32, (tq, LANES), 1)
        for pp in range(PROMPT_HEADS // 2):
            dqc = jnp.concatenate([dq_ref[pp], jnp.zeros((LANES - SUBLANES, tq), F32)], axis=0).T
            parts = [p.astype(F32) for p in _split3(dqc)]
            q2 = q_ref[:, pp * LANES:(pp + 1) * LANES].astype(F32)
            for h in range(2):
                base = DH_A if h == 0 else 0
                own = (lane < DH_A) if h == 0 else (lane >= DH_A)
                extra = jnp.where(jnp.logical_and(lane >= base + 3, lane < base + 6), 1.0, 0.0)
                for c in range(3):
                    extra = jnp.where(lane == base + c, parts[c][:, h:h + 1], extra)
                qa_ref[2 * pp + h] = jnp.where(own, q2, extra).astype(BF16)

    key_off = qi * tq - ki * tk
    unmasked = (ki + 1) * tk <= qi * tq

    @pl.when(unmasked)
    def _():
        _prompt_tile(qa_ref, ka_ref[...], vtp_ref[...], key_off, False, shifted, m_ref, acc_ref, p_ref, alpha_ref)
        vtp_ref[...] = vt_ref[...]

    @pl.when(jnp.logical_not(unmasked))
    def _():
        _prompt_tile(qa_ref, ka_ref[...], vtp_ref[...], key_off, True, shifted, m_ref, acc_ref, p_ref, alpha_ref)
        vtp_ref[...] = vt_ref[...]

    @pl.when(last_tab[t] == 1)
    def _():
        _prompt_pv(vt_ref[...], p_ref, alpha_ref if shifted else None, acc_ref)
        bad = jnp.zeros((1, tq), F32)
        for h in range(PROMPT_HEADS):
            acc = acc_ref[h]
            bad = jnp.maximum(bad, jnp.max(jnp.where(acc - acc == 0.0, 0.0, 1.0), axis=0, keepdims=True))
            bad = jnp.maximum(bad, jnp.where(acc[DH_A:DH_A + 1, :] > 0.0, 0.0, 1.0))
        bad_ref[0] = jnp.broadcast_to(jnp.max(bad, axis=1, keepdims=True), (SUBLANES, LANES))
        for pp in range(PROMPT_HEADS // 2):
            out_t = jnp.concatenate([acc_ref[2 * pp + h, 0:DH_A, :] / acc_ref[2 * pp + h, DH_A:DH_A + 1, :]
                                     for h in range(2)], axis=0)
            o_ref[:, pp * LANES:(pp + 1) * LANES] = out_t.T.astype(o_ref.dtype)


def _fox_prompt_call(shifted, q, kaug, vta, dpair, batch, seq, tq, tk):
    t_rows = q.shape[0]
    nq, nk = seq // tq, seq // tk
    kv_per_q = tq // tk
    pairs = [(a, b) for a in range(nq) for b in range((a + 1) * kv_per_q)]
    n_steps = len(pairs)
    qi_tab = jnp.asarray([a for a, _ in pairs], jnp.int32)
    ki_tab = jnp.asarray([b for _, b in pairs], jnp.int32)
    first_tab = jnp.asarray([int(i == 0 or pairs[i - 1][0] != pairs[i][0]) for i in range(n_steps)], jnp.int32)
    last_tab = jnp.asarray([int(i == n_steps - 1 or pairs[i + 1][0] != pairs[i][0]) for i in range(n_steps)],
                           jnp.int32)
    nh = PROMPT_HEADS
    groups = H_A // nh
    wq = nh * DH_A
    grid_spec = pltpu.PrefetchScalarGridSpec(
        num_scalar_prefetch=4,
        grid=(batch, groups, n_steps),
        in_specs=[
            pl.BlockSpec((tq, wq), lambda b, g, t, qt, kt, ft, lt: (b * nq + qt[t], g)),
            pl.BlockSpec((nh, tk, LANES), lambda b, g, t, qt, kt, ft, lt: (g, b * nk + kt[t], 0)),
            pl.BlockSpec((nh, V_ROWS, tk), lambda b, g, t, qt, kt, ft, lt: (g, 0, b * nk + kt[t])),
            pl.BlockSpec((nh // 2, SUBLANES, tq), lambda b, g, t, qt, kt, ft, lt: (g, 0, b * nq + qt[t])),
            pl.BlockSpec(memory_space=pl.ANY),
        ],
        out_specs=(pl.BlockSpec((tq, wq), lambda b, g, t, qt, kt, ft, lt: (b * nq + qt[t], g)),
                   pl.BlockSpec((1, SUBLANES, LANES),
                                lambda b, g, t, qt, kt, ft, lt: ((b * groups + g) * nq + qt[t], 0, 0))),
        scratch_shapes=[pltpu.VMEM((nh, tq, LANES), BF16),
                        pltpu.VMEM((nh, SUBLANES, tq), F32),
                        pltpu.VMEM((nh, V_ROWS, tq), F32),
                        pltpu.VMEM((nh, tk, tq), BF16), pltpu.VMEM((nh, SUBLANES, tq), F32),
                        pltpu.VMEM((nh, V_ROWS, tk), BF16)],
    )
    return pl.pallas_call(
        functools.partial(_fox_prompt_kernel, shifted),
        grid_spec=grid_spec,
        out_shape=(jax.ShapeDtypeStruct((t_rows, D_A), BF16),
                   jax.ShapeDtypeStruct((batch * groups * nq, SUBLANES, LANES), F32)),
        input_output_aliases={8: 0},
        compiler_params=_cparams("arbitrary", "arbitrary", "arbitrary"),
        name="fox_prompt_shifted" if shifted else "fox_prompt",
    )(qi_tab, ki_tab, first_tab, last_tab, q, kaug, vta, dpair, jnp.zeros((t_rows, D_A), BF16))


def _fox_prompt(q, kaug, vta, dpair, batch, seq, tq, tk):
    args = (q, kaug, vta, dpair, batch, seq, tq, tk)
    a_fast, bad = _fox_prompt_call(False, *args)
    return lax.cond(jnp.max(bad) > 0.0, lambda: _fox_prompt_call(True, *args)[0], lambda: a_fast)


def _sample_tile(shifted, q2, k2, v2, bias_a, bias_b, mask, m_ref, l_ref, acc_ref):
    tq = q2.shape[0]
    low = lax.broadcasted_iota(jnp.int32, (tq, LANES), 1) < DH_A
    zero = jnp.zeros_like(q2)
    nt = (((1,), (1,)), ((), ()))
    pvs, alphas = [], []
    for h, (qh, bias) in enumerate(((jnp.where(low, q2, zero), bias_a), (jnp.where(low, zero, q2), bias_b))):
        s = lax.dot_general(qh, k2, nt, preferred_element_type=F32) + bias
        if mask is not None:
            s = jnp.where(mask, s, NEG_BIG)
        if not shifted:
            p = jnp.exp2(s)
            l_ref[h] = l_ref[h] + jnp.sum(p, axis=1, keepdims=True)
            pvs.append(jnp.dot(p.astype(BF16), v2, preferred_element_type=F32))
            continue
        m_prev = m_ref[h]
        m_new = jnp.maximum(m_prev, jnp.max(s, axis=1, keepdims=True))
        alpha = jnp.exp2(m_prev - m_new)
        p = jnp.exp2(s - m_new[:, 0:1])
        l_ref[h] = alpha * l_ref[h] + jnp.sum(p, axis=1, keepdims=True)
        m_ref[h] = m_new
        pvs.append(jnp.dot(p.astype(BF16), v2, preferred_element_type=F32))
        alphas.append(alpha)
    if shifted:
        acc_ref[...] = jnp.where(low, alphas[0], alphas[1]) * acc_ref[...] + jnp.where(low, pvs[0], pvs[1])
    else:
        acc_ref[...] = acc_ref[...] + jnp.where(low, pvs[0], pvs[1])


def _fox_sample_kernel(shifted, n_cache_tiles, q_ref, ck_ref, cv_ref, r_ref, nk_ref, nv_ref, dqc_ref, dqr_ref,
                       a_in, o_ref, bad_ref, m_ref, l_ref, acc_ref):
    del a_in
    s_idx = pl.program_id(1)
    tq = q_ref.shape[0]

    @pl.when(s_idx == 0)
    def _():
        m_ref[...] = jnp.full_like(m_ref, -jnp.inf)
        l_ref[...] = jnp.zeros_like(l_ref)
        acc_ref[...] = jnp.zeros_like(acc_ref)

    dq = dqc_ref[...]

    def pair_refs(p):
        cols = pl.ds(p * LANES, LANES)
        return m_ref.at[p], l_ref.at[p], acc_ref.at[:, cols]

    @pl.when(s_idx < n_cache_tiles)
    def _():
        r = r_ref[0, 0]
        for p in range(HEAD_PAIRS):
            cols = slice(p * LANES, (p + 1) * LANES)
            bias_a = dq[:, 2 * p:2 * p + 1] + r[2 * p:2 * p + 1, :]
            bias_b = dq[:, 2 * p + 1:2 * p + 2] + r[2 * p + 1:2 * p + 2, :]
            _sample_tile(shifted, q_ref[:, cols], ck_ref[0, 0, :, cols].astype(BF16),
                         cv_ref[0, 0, :, cols].astype(BF16), bias_a, bias_b, None, *pair_refs(p))

    @pl.when(s_idx == n_cache_tiles)
    def _():
        dk = dqr_ref[0]
        causal = lax.broadcasted_iota(jnp.int32, (tq, tq), 0) >= lax.broadcasted_iota(jnp.int32, (tq, tq), 1)
        for p in range(HEAD_PAIRS):
            cols = slice(p * LANES, (p + 1) * LANES)
            bias_a = dq[:, 2 * p:2 * p + 1] - dk[2 * p:2 * p + 1, :]
            bias_b = dq[:, 2 * p + 1:2 * p + 2] - dk[2 * p + 1:2 * p + 2, :]
            _sample_tile(shifted, q_ref[:, cols], nk_ref[0, :, cols].astype(BF16), nv_ref[0, :, cols].astype(BF16),
                         bias_a, bias_b, causal, *pair_refs(p))
        low = lax.broadcasted_iota(jnp.int32, (tq, LANES), 1) < DH_A
        bad = jnp.zeros((tq, LANES), F32)
        for p in range(HEAD_PAIRS):
            cols = slice(p * LANES, (p + 1) * LANES)
            acc = acc_ref[:, cols]
            den = jnp.where(low, l_ref[p, 0], l_ref[p, 1])
            bad = jnp.maximum(bad, jnp.where(jnp.logical_and(acc - acc == 0.0, den > 0.0), 0.0, 1.0))
            o_ref[:, cols] = (acc / den).astype(o_ref.dtype)
        bad = jnp.max(jnp.max(bad, axis=0, keepdims=True), axis=1, keepdims=True)
        bad_ref[0] = jnp.broadcast_to(bad, (SUBLANES, LANES))


def _fox_sample_call(shifted, layer, q, cache_k, cache_v, rsuf, ks, vs, dcol, drow_s, a_buf, n_prompt, tile):
    depth, dec_batch, past, _ = cache_k.shape
    dec_seq = ks.shape[1] // dec_batch
    nct = past // tile
    q0 = n_prompt // dec_seq
    ci = lambda b, s: (layer, b, jnp.minimum(s, nct - 1), 0)
    return pl.pallas_call(
        functools.partial(_fox_sample_kernel, shifted, nct),
        grid=(dec_batch, nct + 1),
        in_specs=[
            pl.BlockSpec((dec_seq, D_A), lambda b, s: (q0 + b, 0)),
            pl.BlockSpec((1, 1, tile, D_A), ci),
            pl.BlockSpec((1, 1, tile, D_A), ci),
            pl.BlockSpec((1, 1, H_A, tile), lambda b, s: (layer, b, 0, jnp.minimum(s, nct - 1))),
            pl.BlockSpec((1, dec_seq, D_A), lambda b, s: (layer, b, 0)),
            pl.BlockSpec((1, dec_seq, D_A), lambda b, s: (layer, b, 0)),
            pl.BlockSpec((dec_seq, H_A), lambda b, s: (q0 + b, 0)),
            pl.BlockSpec((1, H_A, dec_seq), lambda b, s: (b, 0, 0)),
            pl.BlockSpec(memory_space=pl.ANY),
        ],
        out_specs=(pl.BlockSpec((dec_seq, D_A), lambda b, s: (q0 + b, 0)),
                   pl.BlockSpec((1, SUBLANES, LANES), lambda b, s: (b, 0, 0))),
        out_shape=(jax.ShapeDtypeStruct(a_buf.shape, a_buf.dtype),
                   jax.ShapeDtypeStruct((dec_batch, SUBLANES, LANES), F32)),
        scratch_shapes=[pltpu.VMEM((HEAD_PAIRS, 2, dec_seq, LANES), F32),
                        pltpu.VMEM((HEAD_PAIRS, 2, dec_seq, LANES), F32),
                        pltpu.VMEM((dec_seq, D_A), F32)],
        input_output_aliases={8: 0},
        compiler_params=_cparams("parallel", "arbitrary"),
        name="fox_sample_shifted" if shifted else "fox_sample",
    )(q, cache_k, cache_v, rsuf, ks, vs, dcol, drow_s, a_buf)


def _fox_sample(*args):
    a_fast, bad = _fox_sample_call(False, *args)
    return lax.cond(jnp.max(bad) > 0.0, lambda: _fox_sample_call(True, *args)[0], lambda: a_fast)


def _suffix_sum_kernel(x_ref, u_ref, o_ref, carry_ref):
    j = pl.program_id(0)

    @pl.when(j == 0)
    def _():
        carry_ref[...] = jnp.zeros_like(carry_ref)

    x = x_ref[...]
    hi, mid, lo = _split3(x)
    u = u_ref[...]
    loc = (jnp.dot(hi, u, preferred_element_type=F32) + jnp.dot(mid, u, preferred_element_type=F32)
           + jnp.dot(lo, u, preferred_element_type=F32))
    o_ref[...] = (loc + carry_ref[:, 0:1]) * LOG2E
    carry_ref[...] = carry_ref[...] + jnp.sum(x, axis=1, keepdims=True)


def _suffix_sums(x):
    rows, n = x.shape
    tb = min(512, n)
    nb = n // tb
    u = (lax.broadcasted_iota(jnp.int32, (tb, tb), 0) > lax.broadcasted_iota(jnp.int32, (tb, tb), 1)).astype(BF16)
    return pl.pallas_call(
        _suffix_sum_kernel,
        grid=(nb,),
        in_specs=[pl.BlockSpec((rows, tb), lambda j: (0, nb - 1 - j)),
                  pl.BlockSpec((tb, tb), lambda j: (0, 0))],
        out_specs=pl.BlockSpec((rows, tb), lambda j: (0, nb - 1 - j)),
        out_shape=jax.ShapeDtypeStruct((rows, n), F32),
        scratch_shapes=[pltpu.VMEM((rows, LANES), F32)],
        compiler_params=_cparams("arbitrary"),
        name="cache_suffix_sums",
    )(x, u)


def _out_proj_kernel(alpha, a_ref, bn_ref, x_ref, gna_ref, wo_ref, g1_ref, b1_ref, wrh_ref, wrl_ref, br_ref, ls_ref,
                     x1_ref, x1b_ref, route_ref, counts_ref, carry_ref):
    i = pl.program_id(0)
    tm = x_ref.shape[0]

    @pl.when(i == 0)
    def _():
        carry_ref[...] = jnp.zeros_like(carry_ref)

    an = _rms_norm(a_ref[...].astype(F32), gna_ref[...]).astype(BF16)
    mix = (jnp.dot(an, wo_ref[0:D_A, :], preferred_element_type=F32)
           + jnp.dot(bn_ref[...], wo_ref[D_A:, :], preferred_element_type=F32))
    x1 = _layer_norm(alpha * x_ref[...] + mix, g1_ref[...], b1_ref[...])
    x1_ref[...] = x1
    x1h = x1.astype(BF16)
    x1b_ref[...] = _pack_halves(x1)

    x1l = (x1 - x1h.astype(F32)).astype(BF16)
    hi_terms = jnp.dot(x1h, jnp.concatenate([wrh_ref[...], wrl_ref[...]], axis=1), preferred_element_type=F32)
    logits = (hi_terms[:, :LANES] + hi_terms[:, LANES:]
              + jnp.dot(x1l, wrh_ref[...], preferred_element_type=F32)) + br_ref[...]
    lane = lax.broadcasted_iota(jnp.int32, (tm, LANES), 1)
    is_group = lane < N_GROUPS
    gl = jnp.where(is_group, logits, NEG_BIG)
    gmax = jnp.max(gl, axis=1, keepdims=True)
    g_idx = jnp.min(jnp.where(gl == gmax, lane, LANES), axis=1, keepdims=True)
    g_prob = 1.0 / jnp.sum(jnp.exp(gl - gmax), axis=1, keepdims=True)
    in_group = jnp.logical_and(lane >= N_GROUPS, lane < N_GROUPS + N_EXPERTS)
    in_group = jnp.logical_and(in_group, lax.shift_right_arithmetic(lane - N_GROUPS, 3) == g_idx)
    el = jnp.where(in_group, logits, NEG_BIG)
    e1 = jnp.max(el, axis=1, keepdims=True)
    i1 = jnp.min(jnp.where(el == e1, lane, LANES), axis=1, keepdims=True)
    el2 = jnp.where(lane == i1, NEG_BIG, el)
    e2 = jnp.max(el2, axis=1, keepdims=True)
    i2 = jnp.min(jnp.where(el2 == e2, lane, LANES), axis=1, keepdims=True)
    t2 = jnp.exp(e2 - e1)
    w1 = 1.0 / (1.0 + t2)
    gate1 = g_prob * w1
    gate2 = g_prob * (t2 * w1)

    oh1 = lane == i1
    oh2 = lane == i2
    both = (oh1.astype(F32) + oh2.astype(F32)).astype(BF16)
    before = jnp.dot(ls_ref[...], both, preferred_element_type=F32) + carry_ref[...]
    rank1 = jnp.sum(jnp.where(oh1, before, 0.0), axis=1, keepdims=True)
    rank2 = jnp.sum(jnp.where(oh2, before, 0.0), axis=1, keepdims=True)
    carry_ref[...] = carry_ref[...] + jnp.sum(both.astype(F32), axis=0, keepdims=True)
    counts_ref[...] = carry_ref[...]

    route = jnp.where(lane == 0, (i1 - N_GROUPS).astype(F32), 0.0)
    route = jnp.where(lane == 1, (i2 - N_GROUPS).astype(F32), route)
    route = jnp.where(lane == 2, rank1, route)
    route = jnp.where(lane == 3, rank2, route)
    route = jnp.where(lane == 4, gate1, route)
    route = jnp.where(lane == 5, gate2, route)
    route_ref[...] = route


def _out_proj(alpha, a, bn, x, gna, wo, g1, b1, wrh, wrl, br, lstrict):
    t_rows = x.shape[0]
    tm = TOKEN_TILE
    full = lambda arr: pl.BlockSpec(arr.shape, lambda i: (0,) * arr.ndim)
    row = lambda w: pl.BlockSpec((tm, w), lambda i: (i, 0))
    return pl.pallas_call(
        functools.partial(_out_proj_kernel, alpha),
        grid=(t_rows // tm,),
        in_specs=[row(D_A), row(D_B), row(D_MODEL), full(gna), full(wo), full(g1), full(b1), full(wrh), full(wrl),
                  full(br), full(lstrict)],
        out_specs=(row(D_MODEL), row(HALF_MODEL), row(LANES), pl.BlockSpec((1, LANES), lambda i: (0, 0))),
        out_shape=(jax.ShapeDtypeStruct((t_rows, D_MODEL), F32),
                   jax.ShapeDtypeStruct((t_rows, HALF_MODEL), jnp.int32),
                   jax.ShapeDtypeStruct((t_rows, LANES), F32),
                   jax.ShapeDtypeStruct((1, LANES), F32)),
        scratch_shapes=[pltpu.VMEM((1, LANES), F32)],
        compiler_params=_cparams("arbitrary"),
        name="out_proj_router",
    )(a, bn, x, gna, wo, g1, b1, wrh, wrl, br, lstrict)


def _pack_halves(x):
    lo = lax.bitcast_convert_type(x[:, :HALF_MODEL].astype(BF16).astype(F32), jnp.uint32)
    hi = lax.bitcast_convert_type(x[:, HALF_MODEL:].astype(BF16).astype(F32), jnp.uint32)
    return lax.bitcast_convert_type((hi & jnp.uint32(0xFFFF0000)) | (lo >> 16), jnp.int32)


def _unpack_halves(w):
    u = lax.bitcast_convert_type(w, jnp.uint32)
    return (lax.bitcast_convert_type(u << 16, F32), lax.bitcast_convert_type(u & jnp.uint32(0xFFFF0000), F32))


def _expert_kernel(ib_ref, ie_ref, ni_ref, st_ref, en_ref, x_ref, w1_ref, w3_ref, w2_ref, o_ref,
                   w1b_ref, w3b_ref, w2b_ref):
    i = pl.program_id(0)
    bm = x_ref.shape[0]

    @pl.when(i < ni_ref[0])
    def _():
        e = ie_ref[i]
        b = ib_ref[i]
        prev = jnp.maximum(i - 1, 0)
        first_item = i == 0

        @pl.when(jnp.logical_or(first_item, ie_ref[prev] != e))
        def _():
            w1b_ref[...] = w1_ref[0, 0].astype(BF16)
            w3b_ref[...] = w3_ref[0, 0].astype(BF16)
            w2b_ref[...] = w2_ref[0, 0].astype(BF16)

        x_lo, x_hi = (v.astype(BF16) for v in _unpack_halves(x_ref[...]))
        h1 = (jnp.dot(x_lo, w1b_ref[:HALF_MODEL, :], preferred_element_type=F32)
              + jnp.dot(x_hi, w1b_ref[HALF_MODEL:, :], preferred_element_type=F32))
        h3 = (jnp.dot(x_lo, w3b_ref[:HALF_MODEL, :], preferred_element_type=F32)
              + jnp.dot(x_hi, w3b_ref[HALF_MODEL:, :], preferred_element_type=F32))
        h = (h1 * (1.0 / (1.0 + jnp.exp(-h1))) * h3).astype(BF16)
        y = _pack_halves(jnp.dot(h, w2b_ref[...], preferred_element_type=F32))
        first_visit = jnp.logical_or(first_item, ib_ref[prev] != b)

        @pl.when(first_visit)
        def _():
            o_ref[...] = y

        @pl.when(jnp.logical_not(first_visit))
        def _():
            row = b * bm + lax.broadcasted_iota(jnp.int32, (bm, 1), 0)
            mine = jnp.logical_and(row >= st_ref[e], row < en_ref[e])
            o_ref[...] = jnp.where(mine, y, o_ref[...])


def _experts(layer, item_block, item_expert, n_items, starts, ends, xs, w1, w3, w2):
    n_rows = xs.shape[0]
    bm = MOE_ROWS
    n_max = item_block.shape[0]
    item = lambda i, ni: jnp.minimum(i, ni[0] - 1)
    blk = lambda i, ib, ie, ni, st, en: (ib[item(i, ni)], 0)
    wsel = lambda i, ib, ie, ni, st, en: (layer, ie[item(i, ni)], 0, 0)
    grid_spec = pltpu.PrefetchScalarGridSpec(
        num_scalar_prefetch=5,
        grid=(n_max,),
        in_specs=[pl.BlockSpec((bm, HALF_MODEL), blk),
                  pl.BlockSpec((1, 1, D_MODEL, D_EXPERT), wsel),
                  pl.BlockSpec((1, 1, D_MODEL, D_EXPERT), wsel),
                  pl.BlockSpec((1, 1, D_EXPERT, D_MODEL), wsel)],
        out_specs=pl.BlockSpec((bm, HALF_MODEL), blk),
        scratch_shapes=[pltpu.VMEM((D_MODEL, D_EXPERT), BF16), pltpu.VMEM((D_MODEL, D_EXPERT), BF16),
                        pltpu.VMEM((D_EXPERT, D_MODEL), BF16)],
    )
    return pl.pallas_call(
        _expert_kernel,
        grid_spec=grid_spec,
        out_shape=jax.ShapeDtypeStruct((n_rows, HALF_MODEL), jnp.int32),
        compiler_params=_cparams("arbitrary"),
        name="experts",
    )(item_block, item_expert, n_items, starts, ends, xs, w1, w3, w2)


def _combine_kernel(alpha, n_first, x1_ref, y0_ref, y1_ref, route_ref, g2_ref, b2_ref, *o_refs):
    route = route_ref[...]
    g0, g1 = route[:, 4:5], route[:, 5:6]
    y0_lo, y0_hi = _unpack_halves(y0_ref[...])
    y1_lo, y1_hi = _unpack_halves(y1_ref[...])
    z = jnp.concatenate([alpha * x1_ref[:, :HALF_MODEL] + (g0 * y0_lo + g1 * y1_lo),
                         alpha * x1_ref[:, HALF_MODEL:] + (g0 * y0_hi + g1 * y1_hi)], axis=1)
    out = _layer_norm(z, g2_ref[...], b2_ref[...])
    if n_first is None:
        o_refs[0][...] = out
    else:
        first = pl.program_id(0) < n_first

        @pl.when(first)
        def _():
            o_refs[0][...] = out

        @pl.when(jnp.logical_not(first))
        def _():
            o_refs[1][...] = out


def _combine(alpha, x1, y01, route, g2, b2, split_rows=None):
    t_rows = x1.shape[0]
    tm = TOKEN_TILE
    n_tiles = t_rows // tm
    full = lambda arr: pl.BlockSpec(arr.shape, lambda i: (0,) * arr.ndim)
    row = lambda w: pl.BlockSpec((tm, w), lambda i: (i, 0))
    if split_rows is None:
        n_first = None
        out_specs = row(D_MODEL)
        out_shape = jax.ShapeDtypeStruct((t_rows, D_MODEL), F32)
    else:
        n_first = split_rows // tm
        out_specs = (pl.BlockSpec((tm, D_MODEL), lambda i: (jnp.minimum(i, n_first - 1), 0)),
                     pl.BlockSpec((tm, D_MODEL), lambda i: (jnp.maximum(i - n_first, 0), 0)))
        out_shape = (jax.ShapeDtypeStruct((split_rows, D_MODEL), F32),
                     jax.ShapeDtypeStruct((t_rows - split_rows, D_MODEL), F32))
    return pl.pallas_call(
        functools.partial(_combine_kernel, alpha, n_first),
        grid=(n_tiles,),
        in_specs=[row(D_MODEL), row(HALF_MODEL), pl.BlockSpec((tm, HALF_MODEL), lambda i: (i + n_tiles, 0)),
                  row(LANES), full(g2), full(b2)],
        out_specs=out_specs,
        out_shape=out_shape,
        compiler_params=_cparams("arbitrary"),
        name="moe_combine",
    )(x1, y01, y01, route, g2, b2)


def _gather_rows(table, idx):
    n_rows, width = idx.shape[0], table.shape[1]
    workers = SC_CORES * SC_SUBCORES
    step = workers * SC_GATHER_ROWS
    n_pad = -(-n_rows // step) * step
    if n_pad != n_rows:
        idx = jnp.pad(idx, (0, n_pad - n_rows))
    per_worker = n_pad // workers
    n_chunks = per_worker // SC_GATHER_ROWS
    mesh = plsc.VectorSubcoreMesh(core_axis_name="c", subcore_axis_name="s")

    @functools.partial(
        pl.kernel, mesh=mesh,
        out_type=jax.ShapeDtypeStruct((n_pad, width), table.dtype),
        scratch_types=[pltpu.VMEM((SC_GATHER_ROWS,), jnp.int32),
                       pltpu.VMEM((SC_GATHER_ROWS, width), table.dtype),
                       pltpu.SemaphoreType.DMA],
        name="sc_gather_rows",
    )
    def gather(table_hbm, idx_hbm, out_hbm, idx_v, rows_v, sem):
        base = (lax.axis_index("s") * SC_CORES + lax.axis_index("c")) * per_worker

        @pl.loop(0, n_chunks)
        def _(c):
            off = base + c * SC_GATHER_ROWS
            pltpu.sync_copy(idx_hbm.at[pl.ds(off, SC_GATHER_ROWS)], idx_v)
            pltpu.async_copy(table_hbm.at[idx_v], rows_v, sem).wait()
            pltpu.sync_copy(rows_v, out_hbm.at[pl.ds(off, SC_GATHER_ROWS)])

    out = gather(table, idx)
    return out if n_pad == n_rows else out[:n_rows]


def _attention_tile(seq):
    for t in (512, 256, 128):
        if seq % t == 0:
            return t
    raise ValueError("sequence length must be a multiple of 128")


def kernel(x_prompt, x_sample, cache_k, cache_v, cache_logf, w_in, b_f, ln_v_g, ln_v_b, w_s, b_s,
           g_norm_a, g_norm_b, w_out, ln1_g, ln1_b, w_gr, b_gr, w_er, b_er, w1, w3, w2, ln2_g, ln2_b):
    batch, seq, _ = x_prompt.shape
    dec_batch, dec_seq, _ = x_sample.shape
    depth = w_in.shape[0]
    past = cache_k.shape[2]
    n_prompt = batch * seq
    n_sample = dec_batch * dec_seq
    t_rows = n_prompt + n_sample
    alpha = float((2 * depth) ** 0.25)
    tm = TOKEN_TILE
    assert seq % tm == 0 and n_sample % tm == 0 and tm % dec_seq == 0 and dec_seq == GMLP_CHUNK // 2
    assert past % 128 == 0

    x = jnp.concatenate([x_prompt.reshape(n_prompt, D_MODEL), x_sample.reshape(n_sample, D_MODEL)], axis=0)

    sp = (D_A, 2 * D_A, 3 * D_A, 3 * D_A + H_A, 3 * D_A + H_A + D_B)
    wq, wk, wv, wf, wu, wgv = (w_in[..., a:b] for a, b in zip((0,) + sp, sp + (w_in.shape[-1],)))
    wf_pad = jnp.pad(wf, ((0, 0), (0, 0), (0, LANES - H_A)))
    w_cat = jnp.concatenate([wq * (LOG2E * DH_A ** -0.5), wk, wv, wu, wgv, wf_pad], axis=-1).astype(BF16)
    bf_pad = jnp.pad(b_f, ((0, 0), (0, LANES - H_A)))[:, None, :]
    half = GMLP_CHUNK // 2
    wmix = jnp.stack([w_s, jnp.tile(w_s[:, :, :half, :half], (1, 1, 2, 2))], axis=1)
    bs_t = jnp.swapaxes(b_s, 1, 2)
    bs_var = jnp.stack([bs_t, jnp.tile(bs_t[:, :half], (1, 2, 1))], axis=1)
    bmix = jnp.repeat(bs_var, C_B, axis=-1)
    wo_b = w_out.astype(BF16)
    wr = jnp.pad(jnp.concatenate([w_gr, w_er], axis=-1), ((0, 0), (0, 0), (0, LANES - N_GROUPS - N_EXPERTS)))
    wrh = wr.astype(BF16)
    wrl = (wr - wrh.astype(F32)).astype(BF16)
    br = jnp.pad(jnp.concatenate([b_gr, b_er], axis=-1), ((0, 0), (0, LANES - N_GROUPS - N_EXPERTS)))[:, None, :]
    row2 = lambda a: a[:, None, :]

    ri = lax.broadcasted_iota(jnp.int32, (tm, tm), 0)
    ci = lax.broadcasted_iota(jnp.int32, (tm, tm), 1)
    tri = jnp.stack([ri >= ci, jnp.logical_and(ri >= ci, ri // dec_seq == ci // dec_seq)]).astype(BF16)
    lstrict = (ri > ci).astype(BF16)
    prow = lax.broadcasted_iota(jnp.int32, (LANES, H_A * LANES), 0)
    pcol = lax.broadcasted_iota(jnp.int32, (LANES, H_A * LANES), 1)
    phead = pcol // LANES
    poff = pcol % LANES - jnp.where(phead % 2 == 0, DH_A, 0)
    is_one = jnp.logical_and(prow == 3 * H_A, jnp.logical_and(poff >= 0, poff < 3))
    is_part = jnp.logical_and(jnp.logical_and(poff >= 3, poff < 6), prow == (poff - 3) * H_A + phead)
    place = jnp.logical_or(is_one, is_part).astype(BF16)

    clf = jnp.transpose(cache_logf, (0, 1, 3, 2)).reshape(depth * dec_batch * H_A, past)
    rsuf = _suffix_sums(clf).reshape(depth, dec_batch, H_A, past)
    ck = cache_k.reshape(depth, dec_batch, past, D_A)
    cv = cache_v.reshape(depth, dec_batch, past, D_A)

    kp = jnp.zeros((depth, n_prompt, D_A), F32)
    vp = jnp.zeros((depth, n_prompt, D_A), F32)
    ks = jnp.zeros((depth, n_sample, D_A), F32)
    vs = jnp.zeros((depth, n_sample, D_A), F32)
    logfs, gvns = [], []

    bm = MOE_ROWS
    n_assign = 2 * t_rows
    nblk = n_assign // bm
    expert_ids = jnp.arange(N_EXPERTS, dtype=jnp.int32)
    item_ids = jnp.arange(nblk + N_EXPERTS - 1, dtype=jnp.int32)
    q_tile = PROMPT_Q_TILE if seq % PROMPT_Q_TILE == 0 else _attention_tile(seq)
    k_tile = PROMPT_K_TILE if q_tile % PROMPT_K_TILE == 0 else q_tile
    c_tile = SAMPLE_K_TILE if past % SAMPLE_K_TILE == 0 else _attention_tile(past)
    assign_ids = jnp.arange(n_assign, dtype=jnp.int32)

    for l in range(depth):
        q, kp, vp, ks, vs, kaug, vta, logf, dT, dcol, bn, gvn = _in_proj(
            l, x, w_cat[l], bf_pad[l], row2(ln_v_g)[l], row2(ln_v_b)[l], wmix[l], bmix[l], row2(g_norm_b)[l], tri,
            place, kp, vp, ks, vs, n_prompt, seq)
        logfs.append(logf)
        gvns.append(gvn)

        dpair = jnp.pad(dT.reshape(HEAD_PAIRS, 2, t_rows), ((0, 0), (0, SUBLANES - 2), (0, 0)))
        a = _fox_prompt(q, kaug, vta, dpair, batch, seq, q_tile, k_tile)
        drow_s = jnp.transpose(dT[:, n_prompt:].reshape(H_A, dec_batch, dec_seq), (1, 0, 2))
        a = _fox_sample(l, q, ck, cv, rsuf, ks, vs, dcol, drow_s, a, n_prompt, c_tile)

        x1, x1b, route, counts = _out_proj(alpha, a, bn, x, row2(g_norm_a)[l], wo_b[l], row2(ln1_g)[l],
                                           row2(ln1_b)[l], wrh[l], wrl[l], br[l], lstrict)

        cnt = counts[0, N_GROUPS:N_GROUPS + N_EXPERTS].astype(jnp.int32)
        ends = jnp.cumsum(cnt)
        starts = ends - cnt
        eid = route[:, 0:2].astype(jnp.int32)
        rank = route[:, 2:4].astype(jnp.int32)
        onehot = eid[:, :, None] == expert_ids[None, None, :]
        pos = jnp.sum(jnp.where(onehot, starts[None, None, :], 0), axis=-1) + rank
        order = jnp.sort(eid.reshape(n_assign) * n_assign + assign_ids) % n_assign
        first_blk = starts // bm
        n_it = jnp.where(cnt > 0, (ends - 1) // bm - first_blk + 1, 0)
        it_end = jnp.cumsum(n_it)
        item_expert = jnp.minimum(jnp.sum(item_ids[:, None] >= it_end[None, :], axis=1), N_EXPERTS - 1).astype(jnp.int32)
        it_first = jnp.sum(jnp.where(item_expert[:, None] == expert_ids[None, :], (it_end - n_it)[None, :], 0), axis=1)
        it_blk0 = jnp.sum(jnp.where(item_expert[:, None] == expert_ids[None, :], first_blk[None, :], 0), axis=1)
        item_block = jnp.clip(it_blk0 + item_ids - it_first, 0, nblk - 1).astype(jnp.int32)

        xs = _gather_rows(x1b, order // 2)
        yb = _experts(l, item_block, item_expert, it_end[-1:].astype(jnp.int32), starts, ends, xs, w1, w3, w2)
        y01 = _gather_rows(yb, jnp.concatenate([pos[:, 0], pos[:, 1]]))
        x = _combine(alpha, x1, y01, route, row2(ln2_g)[l], row2(ln2_b)[l],
                     split_rows=n_prompt if l == depth - 1 else None)

    y_prompt = x[0].reshape(batch, seq, D_MODEL)
    y_sample = x[1].reshape(dec_batch, dec_seq, D_MODEL)
    logf_all = jnp.stack(logfs)
    return (y_prompt, y_sample,
            kp.reshape(depth, batch, seq, H_A, DH_A), vp.reshape(depth, batch, seq, H_A, DH_A),
            logf_all[:, :n_prompt].reshape(depth, batch, seq, H_A),
            ks.reshape(depth, dec_batch, dec_seq, H_A, DH_A), vs.reshape(depth, dec_batch, dec_seq, H_A, DH_A),
            logf_all[:, n_prompt:].reshape(depth, dec_batch, dec_seq, H_A),
            jnp.stack(gvns).reshape(depth, dec_batch, dec_seq, D_B))
```
